```python
import jax
import jax.numpy as jnp
from jax import lax
import numpy as np

D_MODEL = 1024
BATCH = 32
SEQ = 256
DEPTH = 1
DEC_BATCH = 4
DEC_SEQ = 1024
PAST_LEN = 512

GRID_W = 64
HEAD_DIM = 64
ATTN_HEADS = 8
ATTN_KV_HEADS = 2
ATTN_GROUP = ATTN_HEADS // ATTN_KV_HEADS
ATTN_Q_W = ATTN_HEADS * HEAD_DIM
ATTN_KV_W = ATTN_KV_HEADS * HEAD_DIM
Q_BLOCK = 128
ROPE_THETA = 10000.0
DN_HEADS = 8
DN_DK = 64
DN_DV = 64
DN_K_W = DN_HEADS * DN_DK
DN_V_W = DN_HEADS * DN_DV
DN_DIRS = 2
CONV_W = 3
CONV_CH = 2 * DN_K_W + DN_V_W
CHUNK = 64
N_GROUPS = 4
EXPERTS_PER_GROUP = 4
N_EXPERTS = N_GROUPS * EXPERTS_PER_GROUP
TOP_K_IN_GROUP = 2
D_FF_EXPERT = 256
N_MOD = 6
EPS = 1e-6
IN_SPLITS = (ATTN_Q_W, ATTN_KV_W, ATTN_KV_W, DN_K_W, DN_K_W, DN_V_W,
             DN_DIRS * DN_HEADS, DN_DIRS * DN_HEADS, DN_V_W, D_MODEL, D_MODEL)
IN_W = ATTN_Q_W + 2 * ATTN_KV_W + 2 * DN_K_W + 2 * DN_V_W + 2 * DN_DIRS * DN_HEADS + 2 * D_MODEL

kernel_name = 'hybrid_flow_gqa_deltanet_hmoe_step'


def split_sizes(x, sizes):
    idx = np.cumsum(np.array(sizes))[:-1].tolist()
    return jnp.split(x, idx, axis=-1)


def rmsnorm(x, w):
    xf = x.astype(jnp.float32)
    y = xf * lax.rsqrt(jnp.mean(xf * xf, axis=-1, keepdims=True) + EPS)
    return (y * w.astype(jnp.float32)).astype(x.dtype)


def l2norm(x):
    return x * lax.rsqrt(jnp.sum(x * x, axis=-1, keepdims=True) + EPS)


def heads(x, n, d):
    b, l, _ = x.shape
    return x.reshape(b, l, n, d).transpose(0, 2, 1, 3)


def merge_heads(x):
    b, n, l, d = x.shape
    return x.transpose(0, 2, 1, 3).reshape(b, l, n * d)


def axial_rope(x):
    L = x.shape[2]
    rows = L // GRID_W
    row = jnp.repeat(jnp.arange(rows), GRID_W).astype(jnp.float32)
    col = jnp.tile(jnp.arange(GRID_W), rows).astype(jnp.float32)
    half = HEAD_DIM // 2
    inv = jnp.power(ROPE_THETA, -jnp.arange(0, half, 2, dtype=jnp.float32) / half)

    def rot(xa, pos):
        ang = pos[:, None] * inv[None, :]
        cos, sin = jnp.cos(ang), jnp.sin(ang)
        x1, x2 = xa[..., :half // 2], xa[..., half // 2:]
        return jnp.concatenate([x1 * cos - x2 * sin, x2 * cos + x1 * sin], axis=-1)

    xf = x.astype(jnp.float32)
    out = jnp.concatenate([rot(xf[..., :half], row), rot(xf[..., half:], col)], axis=-1)
    return out.astype(x.dtype)


def block_attention(q, k, v):
    B, H, Lq, Dh = q.shape
    nb = Lq // Q_BLOCK
    qb = jnp.moveaxis(q.reshape(B, ATTN_KV_HEADS, ATTN_GROUP, nb, Q_BLOCK, Dh), 3, 0)
    kf = k.astype(jnp.float32)
    vf = v.astype(jnp.float32)
    scale = Dh ** -0.5

    def one_block(qblk):
        s = jnp.einsum('bkgqd,bksd->bkgqs', qblk.astype(jnp.float32), kf) * scale
        p = jax.nn.softmax(s, axis=-1)
        return jnp.einsum('bkgqs,bksd->bkgqd', p, vf)

    o = lax.map(one_block, qb)
    o = jnp.moveaxis(o, 0, 3).reshape(B, H, Lq, Dh)
    return o.astype(q.dtype)


def centred_conv(x, w):
    L = x.shape[1]
    pad = CONV_W // 2
    xp = jnp.pad(x, ((0, 0), (pad, pad), (0, 0)))
    return sum(xp[:, i:i + L] * w[i] for i in range(CONV_W))


def gated_delta_chunked(q, k, v, beta, g, s0):
    B, H, L, DK = q.shape
    DV = v.shape[-1]
    n = L // CHUNK
    q = q.reshape(B, H, n, CHUNK, DK)
    k = k.reshape(B, H, n, CHUNK, DK)
    v = v.reshape(B, H, n, CHUNK, DV)
    beta = beta.reshape(B, H, n, CHUNK)
    g = jnp.cumsum(g.reshape(B, H, n, CHUNK), axis=-1)
    incl = jnp.tril(jnp.ones((CHUNK, CHUNK), dtype=bool))
    strict = jnp.tril(jnp.ones((CHUNK, CHUNK), dtype=bool), -1)
    diff = g[..., :, None] - g[..., None, :]
    decay = jnp.where(incl, jnp.exp(jnp.where(incl, diff, 0.0)), 0.0)
    k_beta = k * beta[..., None]
    v_beta = v * beta[..., None]
    lmat = jnp.where(strict, jnp.einsum('bhnik,bhnjk->bhnij', k_beta, k) * decay, 0.0)
    eye = jnp.eye(CHUNK, dtype=q.dtype)
    tmat = lax.linalg.triangular_solve(eye + lmat, jnp.broadcast_to(eye, lmat.shape),
                                       left_side=True, lower=True, unit_diagonal=True)
    u = jnp.einsum('bhnij,bhnjv->bhniv', tmat, v_beta)
    w = jnp.einsum('bhnij,bhnjk->bhnik', tmat, k_beta * jnp.exp(g)[..., None])
    intra = jnp.where(incl, jnp.einsum('bhnik,bhnjk->bhnij', q, k) * decay, 0.0)
    g_last = g[..., -1]
    q_dec = q * jnp.exp(g)[..., None]
    k_dec = k * jnp.exp(g_last[..., None] - g)[..., None]

    def step(S, xs):
        q_c, k_c, u_c, w_c, a_c, gl_c = xs
        v_new = u_c - jnp.einsum('bhck,bhkv->bhcv', w_c, S)
        o_c = jnp.einsum('bhck,bhkv->bhcv', q_c, S) + jnp.einsum('bhij,bhjv->bhiv', a_c, v_new)
        S = S * jnp.exp(gl_c)[..., None, None] + jnp.einsum('bhck,bhcv->bhkv', k_c, v_new)
        return S, o_c

    xs = tuple(jnp.moveaxis(t, 2, 0) for t in (q_dec, k_dec, u, w, intra, g_last))
    s_final, o = lax.scan(step, s0, xs)
    o = jnp.moveaxis(o, 0, 2).reshape(B, H, L, DV)
    return o, s_final


def delta_branch(dq, dk, dv, dbeta, dalpha, dgate, conv_w, a_log, dt_bias, dn_norm, s0):
    B, L, _ = dq.shape
    qkv = jax.nn.silu(centred_conv(jnp.concatenate([dq, dk, dv], axis=-1), conv_w))
    q, k, v = split_sizes(qkv, (DN_K_W, DN_K_W, DN_V_W))
    q = l2norm(heads(q, DN_HEADS, DN_DK).astype(jnp.float32)) * (DN_DK ** -0.5)
    k = l2norm(heads(k, DN_HEADS, DN_DK).astype(jnp.float32))
    v = heads(v, DN_HEADS, DN_DV).astype(jnp.float32)
    beta = jax.nn.sigmoid(dbeta.astype(jnp.float32)).reshape(B, L, DN_DIRS, DN_HEADS)
    g = -jnp.exp(a_log.astype(jnp.float32)) * jax.nn.softplus(
        dalpha.astype(jnp.float32).reshape(B, L, DN_DIRS, DN_HEADS) + dt_bias.astype(jnp.float32))
    beta = beta.transpose(2, 0, 3, 1)
    g = g.transpose(2, 0, 3, 1)
    o_f, s_f = gated_delta_chunked(q, k, v, beta[0], g[0], s0[:, 0])
    o_b, s_b = gated_delta_chunked(jnp.flip(q, 2), jnp.flip(k, 2), jnp.flip(v, 2),
                                   jnp.flip(beta[1], -1), jnp.flip(g[1], -1), s0[:, 1])
    o = rmsnorm(o_f + jnp.flip(o_b, 2), dn_norm)
    o = merge_heads(o).astype(dgate.dtype) * jax.nn.silu(dgate)
    return o, jnp.stack([s_f, s_b], axis=1)


def token_mixers(h, p, l, prefix_k, prefix_v, prefix_state):
    B, L, _ = h.shape
    proj = h @ p['w_in'][l]
    aq, ak, av, dq, dk, dv, dbeta, dalpha, dgate, g_attn, g_dn = split_sizes(proj, IN_SPLITS)
    aq = rmsnorm(heads(aq, ATTN_HEADS, HEAD_DIM), p['q_norm'][l])
    ak = rmsnorm(heads(ak, ATTN_KV_HEADS, HEAD_DIM), p['k_norm'][l])
    av = heads(av, ATTN_KV_HEADS, HEAD_DIM)
    if prefix_k is not None:
        keys = jnp.concatenate([prefix_k.astype(ak.dtype), axial_rope(ak)], axis=2)
        vals = jnp.concatenate([prefix_v.astype(av.dtype), av], axis=2)
        o_attn = block_attention(axial_rope(aq), keys, vals)
        s0 = prefix_state.astype(jnp.float32)
    else:
        o_attn = block_attention(aq, ak, av)
        s0 = jnp.zeros((B, DN_DIRS, DN_HEADS, DN_DK, DN_DV), jnp.float32)
    o_attn = merge_heads(o_attn)
    o_dn, s_final = delta_branch(dq, dk, dv, dbeta, dalpha, dgate, p['conv_w'][l], p['a_log'][l],
                                 p['dt_bias'][l], p['dn_norm'][l], s0)
    merged = (jax.nn.sigmoid(g_attn) * (o_attn @ p['w_attn_br'][l])
              + jax.nn.sigmoid(g_dn) * (o_dn @ p['w_dn_br'][l]))
    return merged @ p['w_out'][l], ak, av, s_final.astype(h.dtype)


def hier_moe(h, w_rg, b_rg, w_re, b_re, w_gate, w_up, w_down):
    B, L, D = h.shape
    t = h.reshape(B * L, D)
    T = t.shape[0]
    gp = jax.nn.softmax((t @ w_rg + b_rg).astype(jnp.float32), axis=-1)
    g_top, g_idx = lax.top_k(gp, 1)
    el = (t @ w_re + b_re).astype(jnp.float32).reshape(T, N_GROUPS, EXPERTS_PER_GROUP)
    el = jnp.take_along_axis(el, g_idx[:, :, None], axis=1)[:, 0]
    ep = jax.nn.softmax(el, axis=-1)
    e_top, e_idx = lax.top_k(ep, TOP_K_IN_GROUP)
    wts = e_top / jnp.sum(e_top, axis=-1, keepdims=True) * g_top
    expert = g_idx * EXPERTS_PER_GROUP + e_idx
    combine = jnp.sum(jax.nn.one_hot(expert, N_EXPERTS, dtype=jnp.float32) * wts[..., None], axis=1)
    hid = jax.nn.silu(jnp.einsum('td,edf->tef', t, w_gate)) * jnp.einsum('td,edf->tef', t, w_up)
    hid = hid * combine[..., None].astype(hid.dtype)
    y = jnp.einsum('tef,efd->td', hid, w_down)
    return y.reshape(B, L, D)


def adaln(cond, w_mod, b_mod):
    return (jax.nn.silu(cond) @ w_mod + b_mod)[:, None, :]


def trunk_layer(x, mod, p, l, prefix_k, prefix_v, prefix_state):
    shift1, scale1, gate1, shift2, scale2, gate2 = jnp.split(mod, N_MOD, axis=-1)
    h = rmsnorm(x, p['norm_mix'][l]) * (1.0 + scale1) + shift1
    m, ctx_k, ctx_v, ctx_state = token_mixers(h, p, l, prefix_k, prefix_v, prefix_state)
    x = x + gate1 * m
    h = rmsnorm(x, p['norm_ffn'][l]) * (1.0 + scale2) + shift2
    x = x + gate2 * hier_moe(h, p['w_rg'][l], p['b_rg'][l], p['w_re'][l], p['b_re'][l],
                             p['w_gate_e'][l], p['w_up_e'][l], p['w_down_e'][l])
    return x, ctx_k, ctx_v, ctx_state


def setup_inputs(seed: int = 0) -> dict:
    key = jax.random.key(seed)
    ks = jax.random.split(key, 32)
    f32 = jnp.float32
    D = D_MODEL

    def nrm(k, shape, scale):
        return jax.random.normal(k, shape, f32) * scale

    dt = jnp.exp(jax.random.uniform(ks[15], (DEPTH, DN_DIRS, DN_HEADS), f32, float(np.log(1e-3)), float(np.log(1e-1))))
    return {
        'x_prompt': nrm(ks[0], (BATCH, SEQ, D), 1.0),
        'x_sample': nrm(ks[1], (DEC_BATCH, DEC_SEQ, D), 1.0),
        'c': nrm(ks[2], (DEC_BATCH, D), 1.0),
        'cache_attn_k': nrm(ks[3], (DEC_BATCH, DEPTH, ATTN_KV_HEADS, PAST_LEN, HEAD_DIM), 1.0),
        'cache_attn_v': nrm(ks[4], (DEC_BATCH, DEPTH, ATTN_KV_HEADS, PAST_LEN, HEAD_DIM), 1.0),
        'state_delta': nrm(ks[5], (DEC_BATCH, DEPTH, DN_DIRS, DN_HEADS, DN_DK, DN_DV), 0.1),
        'c_ctx': nrm(ks[6], (D,), 1.0),
        'w_mod': nrm(ks[7], (DEPTH, D, N_MOD * D), D ** -0.5),
        'b_mod': nrm(ks[8], (DEPTH, N_MOD * D), 0.02),
        'norm_mix': 1.0 + nrm(ks[9], (DEPTH, D), 0.02),
        'norm_ffn': 1.0 + nrm(ks[10], (DEPTH, D), 0.02),
        'norm_final': 1.0 + nrm(ks[11], (D,), 0.02),
        'w_in': nrm(ks[12], (DEPTH, D, IN_W), D ** -0.5),
        'q_norm': 1.0 + nrm(ks[13], (DEPTH, HEAD_DIM), 0.02),
        'k_norm': 1.0 + nrm(ks[14], (DEPTH, HEAD_DIM), 0.02),
        'conv_w': nrm(ks[16], (DEPTH, CONV_W, CONV_CH), CONV_W ** -0.5),
        'a_log': jnp.log(jax.random.uniform(ks[17], (DEPTH, DN_DIRS, DN_HEADS), f32, 1.0, 16.0)),
        'dt_bias': jnp.log(jnp.expm1(dt)),
        'dn_norm': 1.0 + nrm(ks[18], (DEPTH, DN_DV), 0.02),
        'w_attn_br': nrm(ks[19], (DEPTH, ATTN_Q_W, D), ATTN_Q_W ** -0.5),
        'w_dn_br': nrm(ks[20], (DEPTH, DN_V_W, D), DN_V_W ** -0.5),
        'w_out': nrm(ks[21], (DEPTH, D, D), D ** -0.5),
        'w_rg': nrm(ks[22], (DEPTH, D, N_GROUPS), D ** -0.5),
        'b_rg': nrm(ks[23], (DEPTH, N_GROUPS), 0.01),
        'w_re': nrm(ks[24], (DEPTH, D, N_EXPERTS), D ** -0.5),
        'b_re': nrm(ks[25], (DEPTH, N_EXPERTS), 0.01),
        'w_gate_e': nrm(ks[26], (DEPTH, N_EXPERTS, D, D_FF_EXPERT), D ** -0.5),
        'w_up_e': nrm(ks[27], (DEPTH, N_EXPERTS, D, D_FF_EXPERT), D ** -0.5),
        'w_down_e': nrm(ks[28], (DEPTH, N_EXPERTS, D_FF_EXPERT, D), D_FF_EXPERT ** -0.5),
    }


def reference(x_prompt, x_sample, c, cache_attn_k, cache_attn_v, state_delta,
              c_ctx, w_mod, b_mod, norm_mix, norm_ffn, norm_final, w_in, q_norm, k_norm,
              conv_w, a_log, dt_bias, dn_norm, w_attn_br, w_dn_br, w_out,
              w_rg, b_rg, w_re, b_re, w_gate_e, w_up_e, w_down_e):
    p = dict(norm_mix=norm_mix, norm_ffn=norm_ffn, w_in=w_in, q_norm=q_norm, k_norm=k_norm,
             conv_w=conv_w, a_log=a_log, dt_bias=dt_bias, dn_norm=dn_norm,
             w_attn_br=w_attn_br, w_dn_br=w_dn_br, w_out=w_out,
             w_rg=w_rg, b_rg=b_rg, w_re=w_re, b_re=b_re,
             w_gate_e=w_gate_e, w_up_e=w_up_e, w_down_e=w_down_e)
    x_ctx = x_prompt
    x_lat = x_sample
    new_k, new_v, new_s = [], [], []
    for l in range(DEPTH):
        mod_ctx = adaln(c_ctx[None, :], w_mod[l], b_mod[l])
        x_ctx, ck, cv, cs = trunk_layer(x_ctx, mod_ctx, p, l, None, None, None)
        new_k.append(ck)
        new_v.append(cv)
        new_s.append(cs)
        mod_lat = adaln(c, w_mod[l], b_mod[l])
        x_lat, _, _, _ = trunk_layer(x_lat, mod_lat, p, l, cache_attn_k[:, l], cache_attn_v[:, l], state_delta[:, l])
    y_prompt = rmsnorm(x_ctx, norm_final)
    y_sample = rmsnorm(x_lat, norm_final)
    new_attn_k = jnp.stack(new_k, axis=1)
    new_attn_v = jnp.stack(new_v, axis=1)
    new_state_delta = jnp.stack(new_s, axis=1)
    return (y_prompt, y_sample, new_attn_k, new_attn_v, new_state_delta)
```

```python
import functools

import numpy as np
import jax
import jax.numpy as jnp
from jax import lax
from jax.experimental import pallas as pl
from jax.experimental.pallas import tpu as pltpu

f32 = jnp.float32
bf16 = jnp.bfloat16

D_MODEL = 1024
HEAD_DIM = 64
ATTN_HEADS = 8
ATTN_KV_HEADS = 2
ATTN_GROUP = ATTN_HEADS // ATTN_KV_HEADS
GRID_W = 64
ROPE_THETA = 10000.0
DN_HEADS = 8
DN_DK = 64
CHUNK = 64
N_GROUPS = 4
EXPERTS_PER_GROUP = 4
N_EXPERTS = 16
D_FF_EXPERT = 256
N_MOD = 6
EPS = 1e-6

ATTN_Q_W = ATTN_HEADS * HEAD_DIM
ATTN_KV_W = ATTN_KV_HEADS * HEAD_DIM
DN_W = DN_HEADS * DN_DK
N_GATES = 2 * DN_HEADS

OFF_Q = 0
OFF_K = ATTN_Q_W
OFF_V = OFF_K + ATTN_KV_W
OFF_DQKV = OFF_V + ATTN_KV_W
OFF_DGATE = OFF_DQKV + 3 * DN_W
OFF_GATTN = OFF_DGATE + DN_W
OFF_GDN = OFF_GATTN + D_MODEL
W_MAIN = OFF_GDN + D_MODEL

ROUTER_LANES = 128
VMEM_LIMIT = 56 * 1024 * 1024

_TRANS_B = (((1,), (1,)), ((), ()))
_TRANS_A = (((0,), (0,)), ((), ()))


def _cparams(sem):
    return pltpu.CompilerParams(dimension_semantics=sem, vmem_limit_bytes=VMEM_LIMIT)


def _dot(a, b):
    return jnp.dot(a, b, preferred_element_type=f32)


def _silu(x):
    return x * jax.nn.sigmoid(x)


def _softplus(x):
    return jnp.maximum(x, 0.0) + jnp.log1p(jnp.exp(-jnp.abs(x)))


def _split2(x):
    hi = x.astype(bf16)
    lo = (x - hi.astype(f32)).astype(bf16)
    return hi, lo


def _split3(x):
    a = x.astype(bf16)
    r = x - a.astype(f32)
    b = r.astype(bf16)
    c = (r - b.astype(f32)).astype(bf16)
    return a, b, c


def _group_sumsq(x, ones_bd):
    hi, lo = _split2(x * x)
    return _dot(hi, ones_bd) + _dot(lo, ones_bd)


def _const_spec(shape):
    nd = len(shape)
    return pl.BlockSpec(shape, lambda *_: (0,) * nd)


def _mod_kernel(c_ref, w_ref, b_ref, o_ref):
    c = c_ref[...]
    o_ref[...] = _dot(_silu(c).astype(bf16), w_ref[...].astype(bf16)) + b_ref[...]


def _mod_call(cond8, w_mod, b_mod):
    tn = 1536
    n = w_mod.shape[1]
    return pl.pallas_call(
        _mod_kernel,
        grid=(n // tn,),
        in_specs=[_const_spec((8, D_MODEL)),
                  pl.BlockSpec((D_MODEL, tn), lambda j: (0, j)),
                  pl.BlockSpec((1, tn), lambda j: (0, j))],
        out_specs=pl.BlockSpec((8, tn), lambda j: (0, j)),
        out_shape=jax.ShapeDtypeStruct((8, n), f32),
        compiler_params=_cparams(("arbitrary",)),
        name="mod",
    )(cond8, w_mod, b_mod)


def _rope(x, cos, sin):
    w = x.shape[-1]
    lane = lax.broadcasted_iota(jnp.int32, x.shape, 1)
    first = (lane % 32) < 16
    swapped = jnp.where(first, pltpu.roll(x, w - 16, 1), pltpu.roll(x, 16, 1))
    return x * cos + swapped * sin


def _proj_kernel(*refs, rope, seq_per_tile):
    (x_ref, mod_ref, nmix_ref, wmain_ref, wba_ref, wbat_ref, qn_ref, kn_ref, ones_ref) = refs[:9]
    pos = 9
    if rope:
        cos_ref, sin_ref = refs[9:11]
        pos = 11
    (q_out, k_out, v_out, dqkv_out, ba_out, bat_out, dgate_out, sga_out, sgd_out) = refs[pos:]

    x = x_ref[...]
    mod = mod_ref[0]
    shift1 = mod[:, 0:D_MODEL]
    scale1 = mod[:, D_MODEL:2 * D_MODEL]
    ms = jnp.mean(x * x, axis=-1, keepdims=True)
    h = x * lax.rsqrt(ms + EPS) * nmix_ref[...]
    h = h * (1.0 + scale1) + shift1
    hb = h.astype(bf16)

    aq = _dot(hb, wmain_ref[:, OFF_Q:OFF_Q + ATTN_Q_W])
    ss = _group_sumsq(aq, ones_ref[...])
    aq = aq * lax.rsqrt(ss * (1.0 / HEAD_DIM) + EPS) * qn_ref[...]
    if rope:
        aq = _rope(aq, cos_ref[...], sin_ref[...])
    q_out[...] = (aq * (HEAD_DIM ** -0.5)).astype(bf16)

    ak = _dot(hb, wmain_ref[:, OFF_K:OFF_K + ATTN_KV_W])
    ss = _group_sumsq(ak, ones_ref[0:ATTN_KV_W, 0:ATTN_KV_W])
    ak = ak * lax.rsqrt(ss * (1.0 / HEAD_DIM) + EPS) * kn_ref[...]
    av = _dot(hb, wmain_ref[:, OFF_V:OFF_V + ATTN_KV_W])
    if rope:
        ak = _rope(ak, cos_ref[:, 0:ATTN_KV_W], sin_ref[:, 0:ATTN_KV_W])
        for kv in range(ATTN_KV_HEADS):
            k_out[0, kv] = ak[:, kv * HEAD_DIM:(kv + 1) * HEAD_DIM].astype(k_out.dtype)
            v_out[0, kv] = av[:, kv * HEAD_DIM:(kv + 1) * HEAD_DIM].astype(v_out.dtype)
    else:
        tm = x.shape[0]
        seq = tm // seq_per_tile
        for kv in range(ATTN_KV_HEADS):
            k_out[:, 0, kv] = ak[:, kv * HEAD_DIM:(kv + 1) * HEAD_DIM].reshape(seq_per_tile, seq, HEAD_DIM)
            v_out[:, 0, kv] = av[:, kv * HEAD_DIM:(kv + 1) * HEAD_DIM].reshape(seq_per_tile, seq, HEAD_DIM)

    dqkv_out[...] = _dot(hb, wmain_ref[:, OFF_DQKV:OFF_DGATE]).astype(bf16)
    ba_out[...] = _dot(hb, wba_ref[...])
    bat_out[...] = lax.dot_general(wbat_ref[...], hb, _TRANS_B, preferred_element_type=f32)
    dgate_out[...] = _dot(hb, wmain_ref[:, OFF_DGATE:OFF_GATTN]).astype(bf16)
    sga_out[...] = jax.nn.sigmoid(_dot(hb, wmain_ref[:, OFF_GATTN:OFF_GDN])).astype(bf16)
    sgd_out[...] = jax.nn.sigmoid(_dot(hb, wmain_ref[:, OFF_GDN:W_MAIN])).astype(bf16)


def _proj_call(x2d, mod3, mod_row, nmix, wmain, wba, wbat, qn, kn, ones_bd, rope_tabs, n_seq, seq_len):
    t = x2d.shape[0]
    tm = 512
    rope = rope_tabs is not None
    tiles_per_seq = max(seq_len // tm, 1)
    seq_per_tile = max(tm // seq_len, 1)
    in_specs = [pl.BlockSpec((tm, D_MODEL), lambda i: (i, 0)),
                pl.BlockSpec((1, 1, N_MOD * D_MODEL), lambda i: (mod_row(i // tiles_per_seq), 0, 0)),
                _const_spec((1, D_MODEL)),
                _const_spec((D_MODEL, W_MAIN)),
                _const_spec((D_MODEL, 2 * N_GATES)),
                _const_spec((2 * N_GATES, D_MODEL)),
                _const_spec((1, ATTN_Q_W)),
                _const_spec((1, ATTN_KV_W)),
                _const_spec((ATTN_Q_W, ATTN_Q_W))]
    args = [x2d, mod3, nmix, wmain, wba, wbat, qn, kn, ones_bd]
    if rope:
        in_specs += [pl.BlockSpec((tm, ATTN_Q_W), lambda i: (i % tiles_per_seq, 0))] * 2
        args += list(rope_tabs)
        kv_shape = jax.ShapeDtypeStruct((n_seq, ATTN_KV_HEADS, seq_len, HEAD_DIM), bf16)
        kv_spec = pl.BlockSpec((1, ATTN_KV_HEADS, tm, HEAD_DIM),
                               lambda i: (i // tiles_per_seq, 0, i % tiles_per_seq, 0))
    else:
        kv_shape = jax.ShapeDtypeStruct((n_seq, 1, ATTN_KV_HEADS, seq_len, HEAD_DIM), f32)
        kv_spec = pl.BlockSpec((seq_per_tile, 1, ATTN_KV_HEADS, seq_len, HEAD_DIM), lambda i: (i, 0, 0, 0, 0))

    def row_spec(w):
        return pl.BlockSpec((tm, w), lambda i: (i, 0))

    out_shape = (jax.ShapeDtypeStruct((t, ATTN_Q_W), bf16), kv_shape, kv_shape,
                 jax.ShapeDtypeStruct((t, 3 * DN_W), bf16),
                 jax.ShapeDtypeStruct((t, 2 * N_GATES), f32),
                 jax.ShapeDtypeStruct((2 * N_GATES, t), f32),
                 jax.ShapeDtypeStruct((t, DN_W), bf16),
                 jax.ShapeDtypeStruct((t, D_MODEL), bf16),
                 jax.ShapeDtypeStruct((t, D_MODEL), bf16))
    out_specs = (row_spec(ATTN_Q_W), kv_spec, kv_spec, row_spec(3 * DN_W), row_spec(2 * N_GATES),
                 pl.BlockSpec((2 * N_GATES, tm), lambda i: (0, i)),
                 row_spec(DN_W), row_spec(D_MODEL), row_spec(D_MODEL))
    return pl.pallas_call(
        functools.partial(_proj_kernel, rope=rope, seq_per_tile=seq_per_tile),
        grid=(t // tm,),
        in_specs=in_specs,
        out_specs=out_specs,
        out_shape=out_shape,
        compiler_params=_cparams(("arbitrary",)),
        name="proj_lat" if rope else "proj_ctx",
    )(*args)


def _attn_kernel(*refs, n_sets):
    q_ref = refs[0]
    kv_refs = refs[1:1 + 2 * n_sets]
    rep_ref = refs[1 + 2 * n_sets]
    o_ref = refs[-1]
    rep = rep_ref[...]
    width = ATTN_GROUP * HEAD_DIM
    blk = lax.broadcasted_iota(jnp.int32, (1, width), 1) // HEAD_DIM

    def head_slab(ref, kv):
        x = ref[0, 0, kv] if len(ref.shape) == 5 else ref[0, kv]
        return _dot(x.astype(bf16), rep)

    for kv in range(ATTN_KV_HEADS):
        q = q_ref[0, :, kv * width:(kv + 1) * width]
        k4 = [head_slab(kv_refs[2 * s], kv) for s in range(n_sets)]
        v4 = [head_slab(kv_refs[2 * s + 1], kv) for s in range(n_sets)]
        acc = jnp.zeros((q.shape[0], width), f32)
        for g in range(ATTN_GROUP):
            sel = blk == g
            scores = [lax.dot_general(q, jnp.where(sel, k, 0.0).astype(bf16), _TRANS_B,
                                      preferred_element_type=f32) for k in k4]
            m = scores[0].max(axis=-1, keepdims=True)
            for s in scores[1:]:
                m = jnp.maximum(m, s.max(axis=-1, keepdims=True))
            probs = [jnp.exp(s - m) for s in scores]
            denom = probs[0].sum(axis=-1, keepdims=True)
            for p in probs[1:]:
                denom = denom + p.sum(axis=-1, keepdims=True)
            og = _dot(probs[0].astype(bf16), jnp.where(sel, v4[0], 0.0).astype(bf16))
            for p, v in zip(probs[1:], v4[1:]):
                og = og + _dot(p.astype(bf16), jnp.where(sel, v, 0.0).astype(bf16))
            acc = acc + og * (1.0 / denom)
        o_ref[0, :, kv * width:(kv + 1) * width] = acc.astype(o_ref.dtype)


def _attn_call(q3, kv_sets, rep, tq, name):
    b, l, _ = q3.shape
    in_specs = [pl.BlockSpec((1, tq, ATTN_Q_W), lambda i, t: (i, t, 0))]
    args = [q3]
    for arr in kv_sets:
        nd = arr.ndim
        blk = (1,) + arr.shape[1:]
        in_specs.append(pl.BlockSpec(blk, lambda i, t, nd=nd: (i,) + (0,) * (nd - 1)))
        args.append(arr)
    in_specs.append(_const_spec(rep.shape))
    args.append(rep)
    return pl.pallas_call(
        functools.partial(_attn_kernel, n_sets=len(kv_sets) // 2),
        grid=(b, l // tq),
        in_specs=in_specs,
        out_specs=pl.BlockSpec((1, tq, ATTN_Q_W), lambda i, t: (i, t, 0)),
        out_shape=jax.ShapeDtypeStruct((b, l, ATTN_Q_W), bf16),
        compiler_params=_cparams(("arbitrary", "arbitrary")),
        name=name,
    )(*args)


def _dn_kernel(*refs, seq_len, has_s0, want_state):
    it = iter(refs)
    dqkv_ref = next(it)
    ba_ref = next(it)
    bar_ref = next(it)
    dgate_ref = next(it)
    s0_ref = next(it) if has_s0 else None
    convw_ref = next(it)
    alog_c = next(it)
    dtb_c = next(it)
    alog_r = next(it)
    dtb_r = next(it)
    dnn_ref = next(it)
    ones_ref = next(it)
    o_ref = next(it)
    st_ref = next(it) if want_state else None
    q_s, k_s, v_s, gc_s, gr_s, be_s, of_s, ob_s, s_s = it

    n = seq_len // CHUNK
    c = CHUNK
    r64 = lax.broadcasted_iota(jnp.int32, (c, c), 0)
    c64 = lax.broadcasted_iota(jnp.int32, (c, c), 1)
    lower = r64 >= c64
    upper = r64 <= c64
    eye = jnp.where(r64 == c64, 1.0, 0.0)
    same8 = (r64 // 8) == (c64 // 8)
    off16 = ((r64 // 16) == (c64 // 16)) & ~same8
    off32 = ((r64 // 32) == (c64 // 32)) & ((r64 // 16) != (c64 // 16))
    off64 = (r64 // 32) != (c64 // 32)
    tril = jnp.where(lower, 1.0, 0.0).astype(bf16)
    triu = jnp.where(upper, 1.0, 0.0).astype(bf16)
    lane16 = lax.broadcasted_iota(jnp.int32, (1, N_GATES), 1)
    sub16 = lax.broadcasted_iota(jnp.int32, (N_GATES, 1), 0)
    row_c = lax.broadcasted_iota(jnp.int32, (c, 1), 0)

    def pre_body(ci, carry):
        r0 = pl.multiple_of(ci * c, c)
        xb = dqkv_ref[0, pl.ds(r0, c), :].astype(f32)
        p0 = pl.multiple_of(jnp.maximum(r0 - 16, 0), 16)
        n0 = pl.multiple_of(jnp.minimum(r0 + c, seq_len - 16), 16)
        prev = dqkv_ref[0, pl.ds(p0, 16), :].astype(f32)[15:16, :]
        nxt = dqkv_ref[0, pl.ds(n0, 16), :].astype(f32)[0:1, :]
        prev = jnp.where(ci > 0, prev, 0.0)
        nxt = jnp.where(ci < n - 1, nxt, 0.0)
        xm = jnp.where(row_c == 0, prev, pltpu.roll(xb, 1, 0))
        xp = jnp.where(row_c == c - 1, nxt, pltpu.roll(xb, c - 1, 0))
        w = convw_ref[...]
        a = _silu(xm * w[0:1, :] + xb * w[1:2, :] + xp * w[2:3, :])
        q = a[:, 0:DN_W]
        k = a[:, DN_W:2 * DN_W]
        ones_bd = ones_ref[...]
        q_s[pl.ds(r0, c), :] = q * lax.rsqrt(_group_sumsq(q, ones_bd) + EPS) * (DN_DK ** -0.5)
        k_s[pl.ds(r0, c), :] = k * lax.rsqrt(_group_sumsq(k, ones_bd) + EPS)
        v_s[pl.ds(r0, c), :] = a[:, 2 * DN_W:3 * DN_W]

        ba = ba_ref[0, pl.ds(r0, c), :]
        be_s[pl.ds(r0, c), :] = jax.nn.sigmoid(ba[:, 0:N_GATES])
        g = -jnp.exp(alog_c[...]) * _softplus(ba[:, N_GATES:2 * N_GATES] + dtb_c[...])
        gsp = _split3(g)
        fwd = sum(_dot(tril, p) for p in gsp)
        bwd = sum(_dot(triu, p) for p in gsp)
        gc_s[pl.ds(r0, c), :] = jnp.where(lane16 < DN_HEADS, fwd, bwd)

        bar = bar_ref[0, ci]
        g_r = -jnp.exp(alog_r[...]) * _softplus(bar[N_GATES:2 * N_GATES, :] + dtb_r[...])
        grsp = _split3(g_r)
        fwd_r = sum(_dot(p, triu) for p in grsp)
        bwd_r = sum(_dot(p, tril) for p in grsp)
        gr_s[ci] = jnp.where(sub16 < DN_HEADS, fwd_r, bwd_r)
        return carry

    lax.fori_loop(0, n, pre_body, 0)

    for d in range(2):
        for hh in range(DN_HEADS):
            j = d * DN_HEADS + hh
            s_s[j] = s0_ref[0, 0, d, hh] if has_s0 else jnp.zeros((DN_DK, DN_DK), f32)

    def chunk_step(ci, d):
        r0 = pl.multiple_of(ci * c, c)
        qc = q_s[pl.ds(r0, c), :]
        kc = k_s[pl.ds(r0, c), :]
        vc = v_s[pl.ds(r0, c), :]
        gcc = gc_s[pl.ds(r0, c), :]
        bec = be_s[pl.ds(r0, c), :]
        grc = gr_s[ci]
        g_last = gcc[c - 1:c, :] if d == 0 else gcc[0:1, :]
        eg = jnp.exp(gcc)
        e_rest = jnp.exp(g_last - gcc)
        e_last = jnp.exp(g_last)
        incl = lower if d == 0 else upper
        strict = (r64 > c64) if d == 0 else (r64 < c64)
        outs = []
        for hh in range(DN_HEADS):
            j = d * DN_HEADS + hh
            sl = slice(hh * DN_DK, (hh + 1) * DN_DK)
            qh, kh, vh = qc[:, sl], kc[:, sl], vc[:, sl]
            gcol = gcc[:, j:j + 1]
            grow = grc[j:j + 1, :]
            decay = jnp.where(incl, jnp.exp(jnp.where(incl, gcol - grow, 0.0)), 0.0)
            bcol = bec[:, j:j + 1]
            egc = eg[:, j:j + 1]
            kq = lax.dot_general(jnp.concatenate([kh, qh], axis=0).astype(bf16), kh.astype(bf16),
                                 _TRANS_B, preferred_element_type=f32)
            lmat = jnp.where(strict, kq[0:c] * bcol * decay, 0.0)
            amat = kq[c:2 * c] * decay
            d1 = jnp.where(same8, lmat, 0.0).astype(bf16)
            d2 = _dot(d1, d1)
            tmat = eye - d1.astype(f32)
            tmat = tmat + _dot(d2.astype(bf16), tmat.astype(bf16))
            d4 = _dot(d2.astype(bf16), d2.astype(bf16)).astype(bf16)
            tmat = tmat + _dot(d4, tmat.astype(bf16))
            for off_mask in (off16, off32, off64):
                tb = tmat.astype(bf16)
                lo = jnp.where(off_mask, lmat, 0.0).astype(bf16)
                tmat = tmat - _dot(_dot(tb, lo).astype(bf16), tb)
            x = _dot(tmat.astype(bf16),
                     jnp.concatenate([vh * bcol, kh * (bcol * egc)], axis=1).astype(bf16))
            u = x[:, 0:DN_DK]
            wmat = x[:, DN_DK:2 * DN_DK]
            s = s_s[j]
            wq = _dot(jnp.concatenate([wmat, qh * egc], axis=0).astype(bf16), s.astype(bf16))
            v_new = u - wq[0:c]
            outs.append(wq[c:2 * c] + _dot(amat.astype(bf16), v_new.astype(bf16)))
            k_dec = kh * e_rest[:, j:j + 1]
            s_s[j] = s * e_last[:, j:j + 1] + lax.dot_general(
                k_dec.astype(bf16), v_new.astype(bf16), _TRANS_A, preferred_element_type=f32)
        dst = of_s if d == 0 else ob_s
        dst[pl.ds(r0, c), :] = jnp.concatenate(outs, axis=1)

    def scan_body(i, carry):
        chunk_step(i, 0)
        chunk_step(n - 1 - i, 1)
        return carry

    lax.fori_loop(0, n, scan_body, 0)

    rb = 128

    def post_body(bi, carry):
        r0 = pl.multiple_of(bi * rb, rb)
        o = of_s[pl.ds(r0, rb), :] + ob_s[pl.ds(r0, rb), :]
        o = o * lax.rsqrt(_group_sumsq(o, ones_ref[...]) * (1.0 / DN_DK) + EPS) * dnn_ref[...]
        gate = dgate_ref[0, pl.ds(r0, rb), :].astype(f32)
        o_ref[0, pl.ds(r0, rb), :] = (o * _silu(gate)).astype(o_ref.dtype)
        return carry

    lax.fori_loop(0, seq_len // rb, post_body, 0)

    if want_state:
        for d in range(2):
            for hh in range(DN_HEADS):
                st_ref[0, 0, d, hh] = s_s[d * DN_HEADS + hh]


def _dn_call(dqkv3, ba3, bar4, dgate3, s0, convw, alog_c, dtb_c, alog_r, dtb_r, dnn, ones_bd, want_state, name):
    b, l, _ = dqkv3.shape
    n = l // CHUNK
    state_blk = (1, 1, 2, DN_HEADS, DN_DK, DN_DK)
    state_spec = pl.BlockSpec(state_blk, lambda i: (i, 0, 0, 0, 0, 0))
    in_specs = [pl.BlockSpec((1, l, 3 * DN_W), lambda i: (i, 0, 0)),
                pl.BlockSpec((1, l, 2 * N_GATES), lambda i: (i, 0, 0)),
                pl.BlockSpec((1, n, 2 * N_GATES, CHUNK), lambda i: (i, 0, 0, 0)),
                pl.BlockSpec((1, l, DN_W), lambda i: (i, 0, 0))]
    args = [dqkv3, ba3, bar4, dgate3]
    if s0 is not None:
        in_specs.append(state_spec)
        args.append(s0)
    consts = [convw, alog_c, dtb_c, alog_r, dtb_r, dnn, ones_bd]
    in_specs += [_const_spec(a.shape) for a in consts]
    args += consts
    out_shape = [jax.ShapeDtypeStruct((b, l, DN_W), bf16)]
    out_specs = [pl.BlockSpec((1, l, DN_W), lambda i: (i, 0, 0))]
    if want_state:
        out_shape.append(jax.ShapeDtypeStruct((b,) + state_blk[1:], f32))
        out_specs.append(state_spec)
    scratch = [pltpu.VMEM((l, DN_W), f32), pltpu.VMEM((l, DN_W), f32), pltpu.VMEM((l, DN_W), f32),
               pltpu.VMEM((l, N_GATES), f32), pltpu.VMEM((n, N_GATES, CHUNK), f32), pltpu.VMEM((l, N_GATES), f32),
               pltpu.VMEM((l, DN_W), f32), pltpu.VMEM((l, DN_W), f32),
               pltpu.VMEM((N_GATES, DN_DK, DN_DK), f32)]
    res = pl.pallas_call(
        functools.partial(_dn_kernel, seq_len=l, has_s0=s0 is not None, want_state=want_state),
        grid=(b,),
        in_specs=in_specs,
        out_specs=tuple(out_specs),
        out_shape=tuple(out_shape),
        scratch_shapes=scratch,
        compiler_params=_cparams(("arbitrary",)),
        name=name,
    )(*args)
    return res if want_state else (res[0], None)


def _merge_kernel(oa_ref, od_ref, sga_ref, sgd_ref, x_ref, mod_ref, wa_ref, wd_ref, wo_ref, nffn_ref,
                  wr_ref, br_ref, x1_out, h2_out, comb_out):
    mod = mod_ref[0]
    gate1 = mod[:, 2 * D_MODEL:3 * D_MODEL]
    shift2 = mod[:, 3 * D_MODEL:4 * D_MODEL]
    scale2 = mod[:, 4 * D_MODEL:5 * D_MODEL]
    merged = (sga_ref[...].astype(f32) * _dot(oa_ref[...], wa_ref[...])
              + sgd_ref[...].astype(f32) * _dot(od_ref[...], wd_ref[...]))
    m = _dot(merged.astype(bf16), wo_ref[...])
    x1 = x_ref[...] + gate1 * m
    x1_out[...] = x1
    ms = jnp.mean(x1 * x1, axis=-1, keepdims=True)
    h2 = x1 * lax.rsqrt(ms + EPS) * nffn_ref[...]
    h2 = h2 * (1.0 + scale2) + shift2
    h2_out[...] = h2.astype(bf16)

    h_hi, h_lo = _split2(h2)
    w_hi, w_lo = _split2(wr_ref[...])
    logits = _dot(h_hi, w_hi) + _dot(h_hi, w_lo) + _dot(h_lo, w_hi) + br_ref[...]
    lane = lax.broadcasted_iota(jnp.int32, logits.shape, 1)
    big = ROUTER_LANES
    neg = -jnp.inf

    def first_lane(hit):
        return jnp.min(jnp.where(hit, lane, big), axis=-1, keepdims=True)

    is_g = lane < N_GROUPS
    gl = jnp.where(is_g, logits, neg)
    gexp = jnp.where(is_g, jnp.exp(gl - gl.max(axis=-1, keepdims=True)), 0.0)
    gp = gexp / gexp.sum(axis=-1, keepdims=True)
    g_top = gp.max(axis=-1, keepdims=True)
    g_idx = first_lane(is_g & (gp == g_top))
    lo_lane = N_GROUPS + g_idx * EXPERTS_PER_GROUP
    is_e = (lane >= lo_lane) & (lane < lo_lane + EXPERTS_PER_GROUP)
    el = jnp.where(is_e, logits, neg)
    eexp = jnp.where(is_e, jnp.exp(el - el.max(axis=-1, keepdims=True)), 0.0)
    ep = jnp.where(is_e, eexp / eexp.sum(axis=-1, keepdims=True), -1.0)
    p1 = ep.max(axis=-1, keepdims=True)
    i1 = first_lane(ep == p1)
    ep2 = jnp.where(lane == i1, -1.0, ep)
    p2 = ep2.max(axis=-1, keepdims=True)
    i2 = first_lane(ep2 == p2)
    tot = p1 + p2
    comb_out[...] = (jnp.where(lane == i1, p1 / tot * g_top, 0.0)
                     + jnp.where(lane == i2, p2 / tot * g_top, 0.0))


def _merge_call(oa, od, sga, sgd, x2d, mod3, mod_row, wa, wd, wo, nffn, wr, br, seq_len, name):
    t = x2d.shape[0]
    tm = 512
    tiles_per_seq = max(seq_len // tm, 1)

    def row_spec(w):
        return pl.BlockSpec((tm, w), lambda i: (i, 0))

    return pl.pallas_call(
        _merge_kernel,
        grid=(t // tm,),
        in_specs=[row_spec(ATTN_Q_W), row_spec(DN_W), row_spec(D_MODEL), row_spec(D_MODEL), row_spec(D_MODEL),
                  pl.BlockSpec((1, 1, N_MOD * D_MODEL), lambda i: (mod_row(i // tiles_per_seq), 0, 0)),
                  _const_spec(wa.shape), _const_spec(wd.shape), _const_spec(wo.shape),
                  _const_spec((1, D_MODEL)), _const_spec(wr.shape), _const_spec(br.shape)],
        out_specs=(row_spec(D_MODEL), row_spec(D_MODEL), row_spec(ROUTER_LANES)),
        out_shape=(jax.ShapeDtypeStruct((t, D_MODEL), f32),
                   jax.ShapeDtypeStruct((t, D_MODEL), bf16),
                   jax.ShapeDtypeStruct((t, ROUTER_LANES), f32)),
        compiler_params=_cparams(("arbitrary",)),
        name=name,
    )(oa, od, sga, sgd, x2d, mod3, wa, wd, wo, nffn, wr, br)


def _moe_kernel(h2_ref, comb_ref, x1_ref, mod_ref, wgu_ref, wdn_ref, nfin_ref, y_ref, acc_ref):
    e = pl.program_id(1)

    @pl.when(e == 0)
    def _():
        acc_ref[...] = jnp.zeros_like(acc_ref)

    gu = _dot(h2_ref[...], wgu_ref[0])
    hid = _silu(gu[:, 0:D_FF_EXPERT]) * gu[:, D_FF_EXPERT:2 * D_FF_EXPERT]
    comb = comb_ref[...]
    lane = lax.broadcasted_iota(jnp.int32, comb.shape, 1)
    ce = jnp.sum(jnp.where(lane == e + N_GROUPS, comb, 0.0), axis=-1, keepdims=True)
    acc_ref[...] += _dot((hid * ce).astype(bf16), wdn_ref[0])

    @pl.when(e == N_EXPERTS - 1)
    def _():
        gate2 = mod_ref[0][:, 5 * D_MODEL:6 * D_MODEL]
        x2 = x1_ref[...] + gate2 * acc_ref[...]
        ms = jnp.mean(x2 * x2, axis=-1, keepdims=True)
        y_ref[...] = x2 * lax.rsqrt(ms + EPS) * nfin_ref[...]


def _moe_call(h2, comb, x1, mod3, mod_row, wgu, wdn, nfin, seq_len, name):
    t = h2.shape[0]
    tm = 1024
    tiles_per_seq = max(seq_len // tm, 1)
    return pl.pallas_call(
        _moe_kernel,
        grid=(t // tm, N_EXPERTS),
        in_specs=[pl.BlockSpec((tm, D_MODEL), lambda i, e: (i, 0)),
                  pl.BlockSpec((tm, ROUTER_LANES), lambda i, e: (i, 0)),
                  pl.BlockSpec((tm, D_MODEL), lambda i, e: (i, 0)),
                  pl.BlockSpec((1, 1, N_MOD * D_MODEL), lambda i, e: (mod_row(i // tiles_per_seq), 0, 0)),
                  pl.BlockSpec((1, D_MODEL, 2 * D_FF_EXPERT), lambda i, e: (e, 0, 0)),
                  pl.BlockSpec((1, D_FF_EXPERT, D_MODEL), lambda i, e: (e, 0, 0)),
                  pl.BlockSpec((1, D_MODEL), lambda i, e: (0, 0))],
        out_specs=pl.BlockSpec((tm, D_MODEL), lambda i, e: (i, 0)),
        out_shape=jax.ShapeDtypeStruct((t, D_MODEL), f32),
        scratch_shapes=[pltpu.VMEM((tm, D_MODEL), f32)],
        compiler_params=_cparams(("arbitrary", "arbitrary")),
        name=name,
    )(h2, comb, x1, mod3, wgu, wdn, nfin)


def _rope_tables(seq_len):
    t = np.arange(seq_len)
    row = (t // GRID_W).astype(np.float64)
    col = (t % GRID_W).astype(np.float64)
    half = HEAD_DIM // 2
    inv = np.power(ROPE_THETA, -np.arange(0, half, 2, dtype=np.float64) / half)
    d = np.arange(HEAD_DIM)
    freq = inv[d % (half // 2)]
    pos = np.where(d[None, :] < half, row[:, None], col[:, None])
    ang = pos * freq[None, :]
    sign = np.where((d % half) < half // 2, -1.0, 1.0)
    cos = np.tile(np.cos(ang), (1, ATTN_HEADS)).astype(np.float32)
    sin = np.tile(np.sin(ang) * sign[None, :], (1, ATTN_HEADS)).astype(np.float32)
    return jnp.asarray(cos), jnp.asarray(sin)


def _ones_block_diag():
    i = np.arange(ATTN_Q_W)
    return jnp.asarray((i[:, None] // HEAD_DIM == i[None, :] // HEAD_DIM).astype(np.float32), dtype=bf16)


def _lane_replicator():
    i = np.arange(HEAD_DIM)
    j = np.arange(ATTN_GROUP * HEAD_DIM)
    return jnp.asarray((i[:, None] == j[None, :] % HEAD_DIM).astype(np.float32), dtype=bf16)


def _trunk(x3, mod3, mod_row, wts, prefix):
    n_seq, seq_len, _ = x3.shape
    t = n_seq * seq_len
    x2d = x3.reshape(t, D_MODEL)
    is_lat = prefix is not None
    rope_tabs = _rope_tables(seq_len) if is_lat else None

    q, k, v, dqkv, ba, bat, dgate, sga, sgd = _proj_call(
        x2d, mod3, mod_row, wts["nmix"], wts["wmain"], wts["wba"], wts["wbat"], wts["qn"], wts["kn"],
        wts["ones_bd"], rope_tabs, n_seq, seq_len)

    q3 = q.reshape(n_seq, seq_len, ATTN_Q_W)
    if is_lat:
        pk, pv, s0 = prefix
        o_attn = _attn_call(q3, [pk, pv, k, v], wts["rep"], 256, "attn_lat")
    else:
        s0 = None
        o_attn = _attn_call(q3, [k, v], wts["rep"], seq_len, "attn_ctx")

    n_chunks = seq_len // CHUNK
    bar4 = bat.reshape(2 * N_GATES, n_seq, n_chunks, CHUNK).transpose(1, 2, 0, 3)
    o_dn, state = _dn_call(
        dqkv.reshape(n_seq, seq_len, 3 * DN_W), ba.reshape(n_seq, seq_len, 2 * N_GATES), bar4,
        dgate.reshape(n_seq, seq_len, DN_W), s0, wts["convw"], wts["alog_c"], wts["dtb_c"],
        wts["alog_r"], wts["dtb_r"], wts["dnn"], wts["ones_bd"], not is_lat, "dn_lat" if is_lat else "dn_ctx")

    x1, h2, comb = _merge_call(
        o_attn.reshape(t, ATTN_Q_W), o_dn.reshape(t, DN_W), sga, sgd, x2d, mod3, mod_row,
        wts["wa"], wts["wd"], wts["wo"], wts["nffn"], wts["wr"], wts["br"], seq_len,
        "merge_lat" if is_lat else "merge_ctx")

    y = _moe_call(h2, comb, x1, mod3, mod_row, wts["wgu"], wts["wdn"], wts["nfin"], seq_len,
                  "moe_lat" if is_lat else "moe_ctx")
    return y.reshape(n_seq, seq_len, D_MODEL), k, v, state


def kernel(x_prompt, x_sample, c, cache_attn_k, cache_attn_v, state_delta, c_ctx, w_mod, b_mod, norm_mix, norm_ffn, norm_final, w_in, q_norm, k_norm, conv_w, a_log, dt_bias, dn_norm, w_attn_br, w_dn_br, w_out, w_rg, b_rg, w_re, b_re, w_gate_e, w_up_e, w_down_e):
    layer = 0
    n_lat = x_sample.shape[0]
    w_in_l = w_in[layer]
    ba_lo = OFF_DGATE
    wts = {
        "nmix": norm_mix[layer][None, :],
        "wmain": jnp.concatenate([w_in_l[:, :ba_lo], w_in_l[:, ba_lo + 2 * N_GATES:]], axis=1).astype(bf16),
        "wba": w_in_l[:, ba_lo:ba_lo + 2 * N_GATES].astype(bf16),
        "wbat": w_in_l[:, ba_lo:ba_lo + 2 * N_GATES].T.astype(bf16),
        "qn": jnp.tile(q_norm[layer], ATTN_HEADS)[None, :],
        "kn": jnp.tile(k_norm[layer], ATTN_KV_HEADS)[None, :],
        "ones_bd": _ones_block_diag(),
        "rep": _lane_replicator(),
        "convw": conv_w[layer],
        "alog_c": a_log[layer].reshape(1, N_GATES),
        "dtb_c": dt_bias[layer].reshape(1, N_GATES),
        "alog_r": a_log[layer].reshape(N_GATES, 1),
        "dtb_r": dt_bias[layer].reshape(N_GATES, 1),
        "dnn": jnp.tile(dn_norm[layer], DN_HEADS)[None, :],
        "wa": w_attn_br[layer].astype(bf16),
        "wd": w_dn_br[layer].astype(bf16),
        "wo": w_out[layer].astype(bf16),
        "nffn": norm_ffn[layer][None, :],
        "wr": jnp.concatenate([w_rg[layer], w_re[layer],
                               jnp.zeros((D_MODEL, ROUTER_LANES - N_GROUPS - N_EXPERTS), f32)], axis=1),
        "br": jnp.concatenate([b_rg[layer], b_re[layer],
                               jnp.zeros((ROUTER_LANES - N_GROUPS - N_EXPERTS,), f32)])[None, :],
        "wgu": jnp.concatenate([w_gate_e[layer], w_up_e[layer]], axis=-1).astype(bf16),
        "wdn": w_down_e[layer].astype(bf16),
        "nfin": norm_final[None, :],
    }
    cond8 = jnp.concatenate([c_ctx[None, :], c, jnp.zeros((8 - 1 - n_lat, D_MODEL), f32)], axis=0)
    mod3 = _mod_call(cond8, w_mod[layer], b_mod[layer][None, :])[:, None, :]

    y_prompt, new_k, new_v, new_state = _trunk(x_prompt, mod3, lambda s: 0, wts, None)
    y_sample, _, _, _ = _trunk(x_sample, mod3, lambda s: s + 1, wts,
                               (cache_attn_k, cache_attn_v, state_delta))
    return (y_prompt, y_sample, new_k, new_v, new_state)
```

```python
import functools

import numpy as np
import jax
import jax.numpy as jnp
from jax import lax
from jax.experimental import pallas as pl
from jax.experimental.pallas import tpu as pltpu

f32 = jnp.float32
bf16 = jnp.bfloat16

D_MODEL = 1024
HEAD_DIM = 64
ATTN_HEADS = 8
ATTN_KV_HEADS = 2
ATTN_GROUP = ATTN_HEADS // ATTN_KV_HEADS
GRID_W = 64
ROPE_THETA = 10000.0
DN_HEADS = 8
DN_DK = 64
CHUNK = 64
N_GROUPS = 4
EXPERTS_PER_GROUP = 4
N_EXPERTS = 16
D_FF_EXPERT = 256
N_MOD = 6
EPS = 1e-6

ATTN_Q_W = ATTN_HEADS * HEAD_DIM
ATTN_KV_W = ATTN_KV_HEADS * HEAD_DIM
DN_W = DN_HEADS * DN_DK
N_GATES = 2 * DN_HEADS

OFF_Q = 0
OFF_K = ATTN_Q_W
OFF_V = OFF_K + ATTN_KV_W
OFF_DQKV = OFF_V + ATTN_KV_W
OFF_DGATE = OFF_DQKV + 3 * DN_W
OFF_GATTN = OFF_DGATE + DN_W
OFF_GDN = OFF_GATTN + D_MODEL
W_MAIN = OFF_GDN + D_MODEL

ROUTER_LANES = 128
VMEM_LIMIT = 56 * 1024 * 1024

_TRANS_B = (((1,), (1,)), ((), ()))
_TRANS_A = (((0,), (0,)), ((), ()))


def _cparams(sem):
    return pltpu.CompilerParams(dimension_semantics=sem, vmem_limit_bytes=VMEM_LIMIT)


def _dot(a, b):
    return jnp.dot(a, b, preferred_element_type=f32)


def _silu(x):
    return x * jax.nn.sigmoid(x)


def _softplus(x):
    return jnp.maximum(x, 0.0) + jnp.log1p(jnp.exp(-jnp.abs(x)))


def _split2(x):
    hi = x.astype(bf16)
    lo = (x - hi.astype(f32)).astype(bf16)
    return hi, lo


def _split3(x):
    a = x.astype(bf16)
    r = x - a.astype(f32)
    b = r.astype(bf16)
    c = (r - b.astype(f32)).astype(bf16)
    return a, b, c


def _group_sumsq(x, ones_bd):
    hi, lo = _split2(x * x)
    return _dot(hi, ones_bd) + _dot(lo, ones_bd)


def _const_spec(shape):
    nd = len(shape)
    return pl.BlockSpec(shape, lambda *_: (0,) * nd)


def _mod_kernel(c_ref, w_ref, b_ref, o_ref):
    c = c_ref[...]
    o_ref[...] = _dot(_silu(c).astype(bf16), w_ref[...].astype(bf16)) + b_ref[...]


def _mod_call(cond8, w_mod, b_mod):
    tn = 1536
    n = w_mod.shape[1]
    return pl.pallas_call(
        _mod_kernel,
        grid=(n // tn,),
        in_specs=[_const_spec((8, D_MODEL)),
                  pl.BlockSpec((D_MODEL, tn), lambda j: (0, j)),
                  pl.BlockSpec((1, tn), lambda j: (0, j))],
        out_specs=pl.BlockSpec((8, tn), lambda j: (0, j)),
        out_shape=jax.ShapeDtypeStruct((8, n), f32),
        compiler_params=_cparams(("arbitrary",)),
        name="mod",
    )(cond8, w_mod, b_mod)


def _rope(x, cos, sin):
    w = x.shape[-1]
    lane = lax.broadcasted_iota(jnp.int32, x.shape, 1)
    first = (lane % 32) < 16
    swapped = jnp.where(first, pltpu.roll(x, w - 16, 1), pltpu.roll(x, 16, 1))
    return x * cos + swapped * sin


def _proj_kernel(*refs, rope, seq_per_tile):
    (x_ref, mod_ref, nmix_ref, wmain_ref, wba_ref, wbat_ref, qn_ref, kn_ref, ones_ref) = refs[:9]
    pos = 9
    if rope:
        cos_ref, sin_ref = refs[9:11]
        pos = 11
    (q_out, k_out, v_out, dqkv_out, ba_out, bat_out, dgate_out, sga_out, sgd_out) = refs[pos:]

    x = x_ref[...]
    mod = mod_ref[0]
    shift1 = mod[:, 0:D_MODEL]
    scale1 = mod[:, D_MODEL:2 * D_MODEL]
    ms = jnp.mean(x * x, axis=-1, keepdims=True)
    h = x * lax.rsqrt(ms + EPS) * nmix_ref[...]
    h = h * (1.0 + scale1) + shift1
    hb = h.astype(bf16)

    aq = _dot(hb, wmain_ref[:, OFF_Q:OFF_Q + ATTN_Q_W])
    ss = _group_sumsq(aq, ones_ref[...])
    aq = aq * lax.rsqrt(ss * (1.0 / HEAD_DIM) + EPS) * qn_ref[...]
    if rope:
        aq = _rope(aq, cos_ref[...], sin_ref[...])
    q_out[...] = (aq * (HEAD_DIM ** -0.5)).astype(bf16)

    ak = _dot(hb, wmain_ref[:, OFF_K:OFF_K + ATTN_KV_W])
    ss = _group_sumsq(ak, ones_ref[0:ATTN_KV_W, 0:ATTN_KV_W])
    ak = ak * lax.rsqrt(ss * (1.0 / HEAD_DIM) + EPS) * kn_ref[...]
    av = _dot(hb, wmain_ref[:, OFF_V:OFF_V + ATTN_KV_W])
    if rope:
        ak = _rope(ak, cos_ref[:, 0:ATTN_KV_W], sin_ref[:, 0:ATTN_KV_W])
        for kv in range(ATTN_KV_HEADS):
            k_out[0, kv] = ak[:, kv * HEAD_DIM:(kv + 1) * HEAD_DIM].astype(k_out.dtype)
            v_out[0, kv] = av[:, kv * HEAD_DIM:(kv + 1) * HEAD_DIM].astype(v_out.dtype)
    else:
        tm = x.shape[0]
        seq = tm // seq_per_tile
        for kv in range(ATTN_KV_HEADS):
            k_out[:, 0, kv] = ak[:, kv * HEAD_DIM:(kv + 1) * HEAD_DIM].reshape(seq_per_tile, seq, HEAD_DIM)
            v_out[:, 0, kv] = av[:, kv * HEAD_DIM:(kv + 1) * HEAD_DIM].reshape(seq_per_tile, seq, HEAD_DIM)

    dqkv_out[...] = _dot(hb, wmain_ref[:, OFF_DQKV:OFF_DGATE]).astype(bf16)
    ba_out[...] = _dot(hb, wba_ref[...])
    bat_out[...] = lax.dot_general(wbat_ref[...], hb, _TRANS_B, preferred_element_type=f32)
    dgate_out[...] = _dot(hb, wmain_ref[:, OFF_DGATE:OFF_GATTN]).astype(bf16)
    sga_out[...] = jax.nn.sigmoid(_dot(hb, wmain_ref[:, OFF_GATTN:OFF_GDN])).astype(bf16)
    sgd_out[...] = jax.nn.sigmoid(_dot(hb, wmain_ref[:, OFF_GDN:W_MAIN])).astype(bf16)


def _proj_call(x2d, mod3, mod_row, nmix, wmain, wba, wbat, qn, kn, ones_bd, rope_tabs, n_seq, seq_len):
    t = x2d.shape[0]
    tm = 512
    rope = rope_tabs is not None
    tiles_per_seq = max(seq_len // tm, 1)
    seq_per_tile = max(tm // seq_len, 1)
    in_specs = [pl.BlockSpec((tm, D_MODEL), lambda i: (i, 0)),
                pl.BlockSpec((1, 1, N_MOD * D_MODEL), lambda i: (mod_row(i // tiles_per_seq), 0, 0)),
                _const_spec((1, D_MODEL)),
                _const_spec((D_MODEL, W_MAIN)),
                _const_spec((D_MODEL, 2 * N_GATES)),
                _const_spec((2 * N_GATES, D_MODEL)),
                _const_spec((1, ATTN_Q_W)),
                _const_spec((1, ATTN_KV_W)),
                _const_spec((ATTN_Q_W, ATTN_Q_W))]
    args = [x2d, mod3, nmix, wmain, wba, wbat, qn, kn, ones_bd]
    if rope:
        in_specs += [pl.BlockSpec((tm, ATTN_Q_W), lambda i: (i % tiles_per_seq, 0))] * 2
        args += list(rope_tabs)
        kv_shape = jax.ShapeDtypeStruct((n_seq, ATTN_KV_HEADS, seq_len, HEAD_DIM), bf16)
        kv_spec = pl.BlockSpec((1, ATTN_KV_HEADS, tm, HEAD_DIM),
                               lambda i: (i // tiles_per_seq, 0, i % tiles_per_seq, 0))
    else:
        kv_shape = jax.ShapeDtypeStruct((n_seq, 1, ATTN_KV_HEADS, seq_len, HEAD_DIM), f32)
        kv_spec = pl.BlockSpec((seq_per_tile, 1, ATTN_KV_HEADS, seq_len, HEAD_DIM), lambda i: (i, 0, 0, 0, 0))

    def row_spec(w):
        return pl.BlockSpec((tm, w), lambda i: (i, 0))

    out_shape = (jax.ShapeDtypeStruct((t, ATTN_Q_W), bf16), kv_shape, kv_shape,
                 jax.ShapeDtypeStruct((t, 3 * DN_W), bf16),
                 jax.ShapeDtypeStruct((t, 2 * N_GATES), f32),
                 jax.ShapeDtypeStruct((2 * N_GATES, t), f32),
                 jax.ShapeDtypeStruct((t, DN_W), bf16),
                 jax.ShapeDtypeStruct((t, D_MODEL), bf16),
                 jax.ShapeDtypeStruct((t, D_MODEL), bf16))
    out_specs = (row_spec(ATTN_Q_W), kv_spec, kv_spec, row_spec(3 * DN_W), row_spec(2 * N_GATES),
                 pl.BlockSpec((2 * N_GATES, tm), lambda i: (0, i)),
                 row_spec(DN_W), row_spec(D_MODEL), row_spec(D_MODEL))
    return pl.pallas_call(
        functools.partial(_proj_kernel, rope=rope, seq_per_tile=seq_per_tile),
        grid=(t // tm,),
        in_specs=in_specs,
        out_specs=out_specs,
        out_shape=out_shape,
        compiler_params=_cparams(("arbitrary",)),
        name="proj_lat" if rope else "proj_ctx",
    )(*args)


def _attn_kernel(*refs, n_sets):
    q_ref = refs[0]
    kv_refs = refs[1:1 + 2 * n_sets]
    rep_ref = refs[1 + 2 * n_sets]
    o_ref = refs[-1]
    rep = rep_ref[...]
    width = ATTN_GROUP * HEAD_DIM
    blk = lax.broadcasted_iota(jnp.int32, (1, width), 1) // HEAD_DIM

    def head_slab(ref, kv):
        x = ref[0, 0, kv] if len(ref.shape) == 5 else ref[0, kv]
        return _dot(x.astype(bf16), rep)

    for kv in range(ATTN_KV_HEADS):
        q = q_ref[0, :, kv * width:(kv + 1) * width]
        k4 = [head_slab(kv_refs[2 * s], kv) for s in range(n_sets)]
        v4 = [head_slab(kv_refs[2 * s + 1], kv) for s in range(n_sets)]
        acc = jnp.zeros((q.shape[0], width), f32)
        for g in range(ATTN_GROUP):
            sel = blk == g
            scores = [lax.dot_general(q, jnp.where(sel, k, 0.0).astype(bf16), _TRANS_B,
                                      preferred_element_type=f32) for k in k4]
            m = scores[0].max(axis=-1, keepdims=True)
            for s in scores[1:]:
                m = jnp.maximum(m, s.max(axis=-1, keepdims=True))
            probs = [jnp.exp(s - m) for s in scores]
            denom = probs[0].sum(axis=-1, keepdims=True)
            for p in probs[1:]:
                denom = denom + p.sum(axis=-1, keepdims=True)
            og = _dot(probs[0].astype(bf16), jnp.where(sel, v4[0], 0.0).astype(bf16))
            for p, v in zip(probs[1:], v4[1:]):
                og = og + _dot(p.astype(bf16), jnp.where(sel, v, 0.0).astype(bf16))
            acc = acc + og * (1.0 / denom)
        o_ref[0, :, kv * width:(kv + 1) * width] = acc.astype(o_ref.dtype)


def _attn_call(q3, kv_sets, rep, tq, name):
    b, l, _ = q3.shape
    in_specs = [pl.BlockSpec((1, tq, ATTN_Q_W), lambda i, t: (i, t, 0))]
    args = [q3]
    for arr in kv_sets:
        nd = arr.ndim
        blk = (1,) + arr.shape[1:]
        in_specs.append(pl.BlockSpec(blk, lambda i, t, nd=nd: (i,) + (0,) * (nd - 1)))
        args.append(arr)
    in_specs.append(_const_spec(rep.shape))
    args.append(rep)
    return pl.pallas_call(
        functools.partial(_attn_kernel, n_sets=len(kv_sets) // 2),
        grid=(b, l // tq),
        in_specs=in_specs,
        out_specs=pl.BlockSpec((1, tq, ATTN_Q_W), lambda i, t: (i, t, 0)),
        out_shape=jax.ShapeDtypeStruct((b, l, ATTN_Q_W), bf16),
        compiler_params=_cparams(("arbitrary", "arbitrary")),
        name=name,
    )(*args)


def _dn_kernel(*refs, seq_len, has_s0, want_state):
    it = iter(refs)
    dqkv_ref = next(it)
    ba_ref = next(it)
    bar_ref = next(it)
    dgate_ref = next(it)
    s0_ref = next(it) if has_s0 else None
    convw_ref = next(it)
    alog_c = next(it)
    dtb_c = next(it)
    alog_r = next(it)
    dtb_r = next(it)
    dnn_ref = next(it)
    ones_ref = next(it)
    o_ref = next(it)
    st_ref = next(it) if want_state else None
    q_s, k_s, v_s, gc_s, gr_s, be_s, of_s, ob_s, s_s = it

    n = seq_len // CHUNK
    c = CHUNK
    r64 = lax.broadcasted_iota(jnp.int32, (c, c), 0)
    c64 = lax.broadcasted_iota(jnp.int32, (c, c), 1)
    lower = r64 >= c64
    upper = r64 <= c64
    eye = jnp.where(r64 == c64, 1.0, 0.0)
    same8 = (r64 // 8) == (c64 // 8)
    off16 = ((r64 // 16) == (c64 // 16)) & ~same8
    off32 = ((r64 // 32) == (c64 // 32)) & ((r64 // 16) != (c64 // 16))
    off64 = (r64 // 32) != (c64 // 32)
    tril = jnp.where(lower, 1.0, 0.0).astype(bf16)
    triu = jnp.where(upper, 1.0, 0.0).astype(bf16)
    lane16 = lax.broadcasted_iota(jnp.int32, (1, N_GATES), 1)
    sub16 = lax.broadcasted_iota(jnp.int32, (N_GATES, 1), 0)
    row_c = lax.broadcasted_iota(jnp.int32, (c, 1), 0)

    def pre_body(ci, carry):
        r0 = pl.multiple_of(ci * c, c)
        xb = dqkv_ref[0, pl.ds(r0, c), :].astype(f32)
        p0 = pl.multiple_of(jnp.maximum(r0 - 16, 0), 16)
        n0 = pl.multiple_of(jnp.minimum(r0 + c, seq_len - 16), 16)
        prev = dqkv_ref[0, pl.ds(p0, 16), :].astype(f32)[15:16, :]
        nxt = dqkv_ref[0, pl.ds(n0, 16), :].astype(f32)[0:1, :]
        prev = jnp.where(ci > 0, prev, 0.0)
        nxt = jnp.where(ci < n - 1, nxt, 0.0)
        xm = jnp.where(row_c == 0, prev, pltpu.roll(xb, 1, 0))
        xp = jnp.where(row_c == c - 1, nxt, pltpu.roll(xb, c - 1, 0))
        w = convw_ref[...]
        a = _silu(xm * w[0:1, :] + xb * w[1:2, :] + xp * w[2:3, :])
        q = a[:, 0:DN_W]
        k = a[:, DN_W:2 * DN_W]
        ones_bd = ones_ref[...]
        q_s[pl.ds(r0, c), :] = q * lax.rsqrt(_group_sumsq(q, ones_bd) + EPS) * (DN_DK ** -0.5)
        k_s[pl.ds(r0, c), :] = k * lax.rsqrt(_group_sumsq(k, ones_bd) + EPS)
        v_s[pl.ds(r0, c), :] = a[:, 2 * DN_W:3 * DN_W]

        ba = ba_ref[0, pl.ds(r0, c), :]
        be_s[pl.ds(r0, c), :] = jax.nn.sigmoid(ba[:, 0:N_GATES])
        g = -jnp.exp(alog_c[...]) * _softplus(ba[:, N_GATES:2 * N_GATES] + dtb_c[...])
        gsp = _split3(g)
        fwd = sum(_dot(tril, p) for p in gsp)
        bwd = sum(_dot(triu, p) for p in gsp)
        gc_s[pl.ds(r0, c), :] = jnp.where(lane16 < DN_HEADS, fwd, bwd)

        bar = bar_ref[0, ci]
        g_r = -jnp.exp(alog_r[...]) * _softplus(bar[N_GATES:2 * N_GATES, :] + dtb_r[...])
        grsp = _split3(g_r)
        fwd_r = sum(_dot(p, triu) for p in grsp)
        bwd_r = sum(_dot(p, tril) for p in grsp)
        gr_s[ci] = jnp.where(sub16 < DN_HEADS, fwd_r, bwd_r)
        return carry

    lax.fori_loop(0, n, pre_body, 0)

    for d in range(2):
        for hh in range(DN_HEADS):
            j = d * DN_HEADS + hh
            s_s[j] = s0_ref[0, 0, d, hh] if has_s0 else jnp.zeros((DN_DK, DN_DK), f32)

    chains = [(d, hh) for d in range(2) for hh in range(DN_HEADS)]

    def each(fn, *lists):
        return [fn(*vals) for vals in zip(*lists)]

    def scan_body(i, carry):
        qh, kh, vh, decay, bcol, egc, erc, elc, strict = [], [], [], [], [], [], [], [], []
        rows = []
        for d in range(2):
            ci = i if d == 0 else n - 1 - i
            r0 = pl.multiple_of(ci * c, c)
            rows.append(r0)
            qc = q_s[pl.ds(r0, c), :]
            kc = k_s[pl.ds(r0, c), :]
            vc = v_s[pl.ds(r0, c), :]
            gcc = gc_s[pl.ds(r0, c), :]
            bec = be_s[pl.ds(r0, c), :]
            grc = gr_s[ci]
            g_last = gcc[c - 1:c, :] if d == 0 else gcc[0:1, :]
            eg = jnp.exp(gcc)
            e_rest = jnp.exp(g_last - gcc)
            e_last = jnp.exp(g_last)
            incl = lower if d == 0 else upper
            for hh in range(DN_HEADS):
                j = d * DN_HEADS + hh
                sl = slice(hh * DN_DK, (hh + 1) * DN_DK)
                qh.append(qc[:, sl])
                kh.append(kc[:, sl])
                vh.append(vc[:, sl])
                diff = gcc[:, j:j + 1] - grc[j:j + 1, :]
                decay.append(jnp.where(incl, jnp.exp(jnp.where(incl, diff, 0.0)), 0.0))
                bcol.append(bec[:, j:j + 1])
                egc.append(eg[:, j:j + 1])
                erc.append(e_rest[:, j:j + 1])
                elc.append(e_last[:, j:j + 1])
                strict.append((r64 > c64) if d == 0 else (r64 < c64))

        kq = each(lambda k, q: lax.dot_general(jnp.concatenate([k, q], axis=0).astype(bf16), k.astype(bf16),
                                               _TRANS_B, preferred_element_type=f32), kh, qh)
        lmat = each(lambda m, x, b, dc: jnp.where(m, x[0:c] * b * dc, 0.0), strict, kq, bcol, decay)
        amat = each(lambda x, dc: (x[c:2 * c] * dc).astype(bf16), kq, decay)
        d1 = each(lambda l: jnp.where(same8, l, 0.0).astype(bf16), lmat)
        d2 = each(lambda a: _dot(a, a).astype(bf16), d1)
        tmat = each(lambda a: eye - a.astype(f32), d1)
        tmat = each(lambda t, a: t + _dot(a, t.astype(bf16)), tmat, d2)
        d4 = each(lambda a: _dot(a, a).astype(bf16), d2)
        tmat = each(lambda t, a: t + _dot(a, t.astype(bf16)), tmat, d4)
        for off_mask in (off16, off32, off64):
            tb = each(lambda t: t.astype(bf16), tmat)
            tl = each(lambda t, l: _dot(t, jnp.where(off_mask, l, 0.0).astype(bf16)).astype(bf16), tb, lmat)
            tmat = each(lambda t, a, b: t - _dot(a, b), tmat, tl, tb)
        x = each(lambda t, v, k, b, e: _dot(t.astype(bf16), jnp.concatenate([v * b, k * (b * e)], axis=1).astype(bf16)),
                 tmat, vh, kh, bcol, egc)
        s_old = [s_s[j] for j in range(N_GATES)]
        wq = each(lambda xx, q, e, s: _dot(jnp.concatenate([xx[:, DN_DK:2 * DN_DK], q * e], axis=0).astype(bf16),
                                           s.astype(bf16)), x, qh, egc, s_old)
        v_new = each(lambda xx, y: (xx[:, 0:DN_DK] - y[0:c]).astype(bf16), x, wq)
        outs = each(lambda y, a, vn: y[c:2 * c] + _dot(a, vn), wq, amat, v_new)
        s_new = each(lambda s, e, k, er, vn: s * e + lax.dot_general((k * er).astype(bf16), vn, _TRANS_A,
                                                                     preferred_element_type=f32),
                     s_old, elc, kh, erc, v_new)
        for j in range(N_GATES):
            s_s[j] = s_new[j]
        of_s[pl.ds(rows[0], c), :] = jnp.concatenate(outs[0:DN_HEADS], axis=1)
        ob_s[pl.ds(rows[1], c), :] = jnp.concatenate(outs[DN_HEADS:2 * DN_HEADS], axis=1)
        return carry

    lax.fori_loop(0, n, scan_body, 0)

    rb = 128

    def post_body(bi, carry):
        r0 = pl.multiple_of(bi * rb, rb)
        o = of_s[pl.ds(r0, rb), :] + ob_s[pl.ds(r0, rb), :]
        o = o * lax.rsqrt(_group_sumsq(o, ones_ref[...]) * (1.0 / DN_DK) + EPS) * dnn_ref[...]
        gate = dgate_ref[0, pl.ds(r0, rb), :].astype(f32)
        o_ref[0, pl.ds(r0, rb), :] = (o * _silu(gate)).astype(o_ref.dtype)
        return carry

    lax.fori_loop(0, seq_len // rb, post_body, 0)

    if want_state:
        for d in range(2):
            for hh in range(DN_HEADS):
                st_ref[0, 0, d, hh] = s_s[d * DN_HEADS + hh]


def _dn_call(dqkv3, ba3, bar4, dgate3, s0, convw, alog_c, dtb_c, alog_r, dtb_r, dnn, ones_bd, want_state, name):
    b, l, _ = dqkv3.shape
    n = l // CHUNK
    state_blk = (1, 1, 2, DN_HEADS, DN_DK, DN_DK)
    state_spec = pl.BlockSpec(state_blk, lambda i: (i, 0, 0, 0, 0, 0))
    in_specs = [pl.BlockSpec((1, l, 3 * DN_W), lambda i: (i, 0, 0)),
                pl.BlockSpec((1, l, 2 * N_GATES), lambda i: (i, 0, 0)),
                pl.BlockSpec((1, n, 2 * N_GATES, CHUNK), lambda i: (i, 0, 0, 0)),
                pl.BlockSpec((1, l, DN_W), lambda i: (i, 0, 0))]
    args = [dqkv3, ba3, bar4, dgate3]
    if s0 is not None:
        in_specs.append(state_spec)
        args.append(s0)
    consts = [convw, alog_c, dtb_c, alog_r, dtb_r, dnn, ones_bd]
    in_specs += [_const_spec(a.shape) for a in consts]
    args += consts
    out_shape = [jax.ShapeDtypeStruct((b, l, DN_W), bf16)]
    out_specs = [pl.BlockSpec((1, l, DN_W), lambda i: (i, 0, 0))]
    if want_state:
        out_shape.append(jax.ShapeDtypeStruct((b,) + state_blk[1:], f32))
        out_specs.append(state_spec)
    scratch = [pltpu.VMEM((l, DN_W), f32), pltpu.VMEM((l, DN_W), f32), pltpu.VMEM((l, DN_W), f32),
               pltpu.VMEM((l, N_GATES), f32), pltpu.VMEM((n, N_GATES, CHUNK), f32), pltpu.VMEM((l, N_GATES), f32),
               pltpu.VMEM((l, DN_W), f32), pltpu.VMEM((l, DN_W), f32),
               pltpu.VMEM((N_GATES, DN_DK, DN_DK), f32)]
    res = pl.pallas_call(
        functools.partial(_dn_kernel, seq_len=l, has_s0=s0 is not None, want_state=want_state),
        grid=(b,),
        in_specs=in_specs,
        out_specs=tuple(out_specs),
        out_shape=tuple(out_shape),
        scratch_shapes=scratch,
        compiler_params=_cparams(("arbitrary",)),
        name=name,
    )(*args)
    return res if want_state else (res[0], None)


def _merge_kernel(oa_ref, od_ref, sga_ref, sgd_ref, x_ref, mod_ref, wa_ref, wd_ref, wo_ref, nffn_ref,
                  wr_ref, br_ref, x1_out, h2_out, comb_out):
    mod = mod_ref[0]
    gate1 = mod[:, 2 * D_MODEL:3 * D_MODEL]
    shift2 = mod[:, 3 * D_MODEL:4 * D_MODEL]
    scale2 = mod[:, 4 * D_MODEL:5 * D_MODEL]
    merged = (sga_ref[...].astype(f32) * _dot(oa_ref[...], wa_ref[...])
              + sgd_ref[...].astype(f32) * _dot(od_ref[...], wd_ref[...]))
    m = _dot(merged.astype(bf16), wo_ref[...])
    x1 = x_ref[...] + gate1 * m
    x1_out[...] = x1
    ms = jnp.mean(x1 * x1, axis=-1, keepdims=True)
    h2 = x1 * lax.rsqrt(ms + EPS) * nffn_ref[...]
    h2 = h2 * (1.0 + scale2) + shift2
    h2_out[...] = h2.astype(bf16)

    h_hi, h_lo = _split2(h2)
    w_hi, w_lo = _split2(wr_ref[...])
    logits = _dot(h_hi, w_hi) + _dot(h_hi, w_lo) + _dot(h_lo, w_hi) + br_ref[...]
    lane = lax.broadcasted_iota(jnp.int32, logits.shape, 1)
    big = ROUTER_LANES
    neg = -jnp.inf

    def first_lane(hit):
        return jnp.min(jnp.where(hit, lane, big), axis=-1, keepdims=True)

    is_g = lane < N_GROUPS
    gl = jnp.where(is_g, logits, neg)
    gexp = jnp.where(is_g, jnp.exp(gl - gl.max(axis=-1, keepdims=True)), 0.0)
    gp = gexp / gexp.sum(axis=-1, keepdims=True)
    g_top = gp.max(axis=-1, keepdims=True)
    g_idx = first_lane(is_g & (gp == g_top))
    lo_lane = N_GROUPS + g_idx * EXPERTS_PER_GROUP
    is_e = (lane >= lo_lane) & (lane < lo_lane + EXPERTS_PER_GROUP)
    el = jnp.where(is_e, logits, neg)
    eexp = jnp.where(is_e, jnp.exp(el - el.max(axis=-1, keepdims=True)), 0.0)
    ep = jnp.where(is_e, eexp / eexp.sum(axis=-1, keepdims=True), -1.0)
    p1 = ep.max(axis=-1, keepdims=True)
    i1 = first_lane(ep == p1)
    ep2 = jnp.where(lane == i1, -1.0, ep)
    p2 = ep2.max(axis=-1, keepdims=True)
    i2 = first_lane(ep2 == p2)
    tot = p1 + p2
    comb_out[...] = (jnp.where(lane == i1, p1 / tot * g_top, 0.0)
                     + jnp.where(lane == i2, p2 / tot * g_top, 0.0))


def _merge_call(oa, od, sga, sgd, x2d, mod3, mod_row, wa, wd, wo, nffn, wr, br, seq_len, name):
    t = x2d.shape[0]
    tm = 512
    tiles_per_seq = max(seq_len // tm, 1)

    def row_spec(w):
        return pl.BlockSpec((tm, w), lambda i: (i, 0))

    return pl.pallas_call(
        _merge_kernel,
        grid=(t // tm,),
        in_specs=[row_spec(ATTN_Q_W), row_spec(DN_W), row_spec(D_MODEL), row_spec(D_MODEL), row_spec(D_MODEL),
                  pl.BlockSpec((1, 1, N_MOD * D_MODEL), lambda i: (mod_row(i // tiles_per_seq), 0, 0)),
                  _const_spec(wa.shape), _const_spec(wd.shape), _const_spec(wo.shape),
                  _const_spec((1, D_MODEL)), _const_spec(wr.shape), _const_spec(br.shape)],
        out_specs=(row_spec(D_MODEL), row_spec(D_MODEL), row_spec(ROUTER_LANES)),
        out_shape=(jax.ShapeDtypeStruct((t, D_MODEL), f32),
                   jax.ShapeDtypeStruct((t, D_MODEL), bf16),
                   jax.ShapeDtypeStruct((t, ROUTER_LANES), f32)),
        compiler_params=_cparams(("arbitrary",)),
        name=name,
    )(oa, od, sga, sgd, x2d, mod3, wa, wd, wo, nffn, wr, br)


def _moe_kernel(h2_ref, comb_ref, x1_ref, mod_ref, wgu_ref, wdn_ref, nfin_ref, y_ref, acc_ref):
    e = pl.program_id(1)

    @pl.when(e == 0)
    def _():
        acc_ref[...] = jnp.zeros_like(acc_ref)

    gu = _dot(h2_ref[...], wgu_ref[0])
    hid = _silu(gu[:, 0:D_FF_EXPERT]) * gu[:, D_FF_EXPERT:2 * D_FF_EXPERT]
    comb = comb_ref[...]
    lane = lax.broadcasted_iota(jnp.int32, comb.shape, 1)
    ce = jnp.sum(jnp.where(lane == e + N_GROUPS, comb, 0.0), axis=-1, keepdims=True)
    acc_ref[...] += _dot((hid * ce).astype(bf16), wdn_ref[0])

    @pl.when(e == N_EXPERTS - 1)
    def _():
        gate2 = mod_ref[0][:, 5 * D_MODEL:6 * D_MODEL]
        x2 = x1_ref[...] + gate2 * acc_ref[...]
        ms = jnp.mean(x2 * x2, axis=-1, keepdims=True)
        y_ref[...] = x2 * lax.rsqrt(ms + EPS) * nfin_ref[...]


def _moe_call(h2, comb, x1, mod3, mod_row, wgu, wdn, nfin, seq_len, name):
    t = h2.shape[0]
    tm = 1024
    tiles_per_seq = max(seq_len // tm, 1)
    return pl.pallas_call(
        _moe_kernel,
        grid=(t // tm, N_EXPERTS),
        in_specs=[pl.BlockSpec((tm, D_MODEL), lambda i, e: (i, 0)),
                  pl.BlockSpec((tm, ROUTER_LANES), lambda i, e: (i, 0)),
                  pl.BlockSpec((tm, D_MODEL), lambda i, e: (i, 0)),
                  pl.BlockSpec((1, 1, N_MOD * D_MODEL), lambda i, e: (mod_row(i // tiles_per_seq), 0, 0)),
                  pl.BlockSpec((1, D_MODEL, 2 * D_FF_EXPERT), lambda i, e: (e, 0, 0)),
                  pl.BlockSpec((1, D_FF_EXPERT, D_MODEL), lambda i, e: (e, 0, 0)),
                  pl.BlockSpec((1, D_MODEL), lambda i, e: (0, 0))],
        out_specs=pl.BlockSpec((tm, D_MODEL), lambda i, e: (i, 0)),
        out_shape=jax.ShapeDtypeStruct((t, D_MODEL), f32),
        scratch_shapes=[pltpu.VMEM((tm, D_MODEL), f32)],
        compiler_params=_cparams(("arbitrary", "arbitrary")),
        name=name,
    )(h2, comb, x1, mod3, wgu, wdn, nfin)


def _rope_tables(seq_len):
    t = np.arange(seq_len)
    row = (t // GRID_W).astype(np.float64)
    col = (t % GRID_W).astype(np.float64)
    half = HEAD_DIM // 2
    inv = np.power(ROPE_THETA, -np.arange(0, half, 2, dtype=np.float64) / half)
    d = np.arange(HEAD_DIM)
    freq = inv[d % (half // 2)]
    pos = np.where(d[None, :] < half, row[:, None], col[:, None])
    ang = pos * freq[None, :]
    sign = np.where((d % half) < half // 2, -1.0, 1.0)
    cos = np.tile(np.cos(ang), (1, ATTN_HEADS)).astype(np.float32)
    sin = np.tile(np.sin(ang) * sign[None, :], (1, ATTN_HEADS)).astype(np.float32)
    return jnp.asarray(cos), jnp.asarray(sin)


def _ones_block_diag():
    i = np.arange(ATTN_Q_W)
    return jnp.asarray((i[:, None] // HEAD_DIM == i[None, :] // HEAD_DIM).astype(np.float32), dtype=bf16)


def _lane_replicator():
    i = np.arange(HEAD_DIM)
    j = np.arange(ATTN_GROUP * HEAD_DIM)
    return jnp.asarray((i[:, None] == j[None, :] % HEAD_DIM).astype(np.float32), dtype=bf16)


def _trunk(x3, mod3, mod_row, wts, prefix):
    n_seq, seq_len, _ = x3.shape
    t = n_seq * seq_len
    x2d = x3.reshape(t, D_MODEL)
    is_lat = prefix is not None
    rope_tabs = _rope_tables(seq_len) if is_lat else None

    q, k, v, dqkv, ba, bat, dgate, sga, sgd = _proj_call(
        x2d, mod3, mod_row, wts["nmix"], wts["wmain"], wts["wba"], wts["wbat"], wts["qn"], wts["kn"],
        wts["ones_bd"], rope_tabs, n_seq, seq_len)

    q3 = q.reshape(n_seq, seq_len, ATTN_Q_W)
    if is_lat:
        pk, pv, s0 = prefix
        o_attn = _attn_call(q3, [pk, pv, k, v], wts["rep"], 256, "attn_lat")
    else:
        s0 = None
        o_attn = _attn_call(q3, [k, v], wts["rep"], seq_len, "attn_ctx")

    n_chunks = seq_len // CHUNK
    bar4 = bat.reshape(2 * N_GATES, n_seq, n_chunks, CHUNK).transpose(1, 2, 0, 3)
    o_dn, state = _dn_call(
        dqkv.reshape(n_seq, seq_len, 3 * DN_W), ba.reshape(n_seq, seq_len, 2 * N_GATES), bar4,
        dgate.reshape(n_seq, seq_len, DN_W), s0, wts["convw"], wts["alog_c"], wts["dtb_c"],
        wts["alog_r"], wts["dtb_r"], wts["dnn"], wts["ones_bd"], not is_lat, "dn_lat" if is_lat else "dn_ctx")

    x1, h2, comb = _merge_call(
        o_attn.reshape(t, ATTN_Q_W), o_dn.reshape(t, DN_W), sga, sgd, x2d, mod3, mod_row,
        wts["wa"], wts["wd"], wts["wo"], wts["nffn"], wts["wr"], wts["br"], seq_len,
        "merge_lat" if is_lat else "merge_ctx")

    y = _moe_call(h2, comb, x1, mod3, mod_row, wts["wgu"], wts["wdn"], wts["nfin"], seq_len,
                  "moe_lat" if is_lat else "moe_ctx")
    return y.reshape(n_seq, seq_len, D_MODEL), k, v, state


def kernel(x_prompt, x_sample, c, cache_attn_k, cache_attn_v, state_delta, c_ctx, w_mod, b_mod, norm_mix, norm_ffn, norm_final, w_in, q_norm, k_norm, conv_w, a_log, dt_bias, dn_norm, w_attn_br, w_dn_br, w_out, w_rg, b_rg, w_re, b_re, w_gate_e, w_up_e, w_down_e):
    layer = 0
    n_lat = x_sample.shape[0]
    w_in_l = w_in[layer]
    ba_lo = OFF_DGATE
    wts = {
        "nmix": norm_mix[layer][None, :],
        "wmain": jnp.concatenate([w_in_l[:, :ba_lo], w_in_l[:, ba_lo + 2 * N_GATES:]], axis=1).astype(bf16),
        "wba": w_in_l[:, ba_lo:ba_lo + 2 * N_GATES].astype(bf16),
        "wbat": w_in_l[:, ba_lo:ba_lo + 2 * N_GATES].T.astype(bf16),
        "qn": jnp.tile(q_norm[layer], ATTN_HEADS)[None, :],
        "kn": jnp.tile(k_norm[layer], ATTN_KV_HEADS)[None, :],
        "ones_bd": _ones_block_diag(),
        "rep": _lane_replicator(),
        "convw": conv_w[layer],
        "alog_c": a_log[layer].reshape(1, N_GATES),
        "dtb_c": dt_bias[layer].reshape(1, N_GATES),
        "alog_r": a_log[layer].reshape(N_GATES, 1),
        "dtb_r": dt_bias[layer].reshape(N_GATES, 1),
        "dnn": jnp.tile(dn_norm[layer], DN_HEADS)[None, :],
        "wa": w_attn_br[layer].astype(bf16),
        "wd": w_dn_br[layer].astype(bf16),
        "wo": w_out[layer].astype(bf16),
        "nffn": norm_ffn[layer][None, :],
        "wr": jnp.concatenate([w_rg[layer], w_re[layer],
                               jnp.zeros((D_MODEL, ROUTER_LANES - N_GROUPS - N_EXPERTS), f32)], axis=1),
        "br": jnp.concatenate([b_rg[layer], b_re[layer],
                               jnp.zeros((ROUTER_LANES - N_GROUPS - N_EXPERTS,), f32)])[None, :],
        "wgu": jnp.concatenate([w_gate_e[layer], w_up_e[layer]], axis=-1).astype(bf16),
        "wdn": w_down_e[layer].astype(bf16),
        "nfin": norm_final[None, :],
    }
    cond8 = jnp.concatenate([c_ctx[None, :], c, jnp.zeros((8 - 1 - n_lat, D_MODEL), f32)], axis=0)
    mod3 = _mod_call(cond8, w_mod[layer], b_mod[layer][None, :])[:, None, :]

    y_prompt, new_k, new_v, new_state = _trunk(x_prompt, mod3, lambda s: 0, wts, None)
    y_sample, _, _, _ = _trunk(x_sample, mod3, lambda s: s + 1, wts,
                               (cache_attn_k, cache_attn_v, state_delta))
    return (y_prompt, y_sample, new_k, new_v, new_state)
```

```python
import functools

import numpy as np
import jax
import jax.numpy as jnp
from jax import lax
from jax.experimental import pallas as pl
from jax.experimental.pallas import tpu as pltpu

f32 = jnp.float32
bf16 = jnp.bfloat16

D_MODEL = 1024
HEAD_DIM = 64
ATTN_HEADS = 8
ATTN_KV_HEADS = 2
ATTN_GROUP = ATTN_HEADS // ATTN_KV_HEADS
GRID_W = 64
ROPE_THETA = 10000.0
DN_HEADS = 8
DN_DK = 64
CHUNK = 64
N_GROUPS = 4
EXPERTS_PER_GROUP = 4
N_EXPERTS = 16
D_FF_EXPERT = 256
N_MOD = 6
EPS = 1e-6

ATTN_Q_W = ATTN_HEADS * HEAD_DIM
ATTN_KV_W = ATTN_KV_HEADS * HEAD_DIM
DN_W = DN_HEADS * DN_DK
N_GATES = 2 * DN_HEADS

OFF_Q = 0
OFF_K = ATTN_Q_W
OFF_V = OFF_K + ATTN_KV_W
OFF_DQKV = OFF_V + ATTN_KV_W
OFF_DGATE = OFF_DQKV + 3 * DN_W
OFF_GATTN = OFF_DGATE + DN_W
OFF_GDN = OFF_GATTN + D_MODEL
W_MAIN = OFF_GDN + D_MODEL

ROUTER_LANES = 128
VMEM_LIMIT = 56 * 1024 * 1024

_TRANS_B = (((1,), (1,)), ((), ()))
_TRANS_A = (((0,), (0,)), ((), ()))


def _cparams(sem):
    return pltpu.CompilerParams(dimension_semantics=sem, vmem_limit_bytes=VMEM_LIMIT)


def _dot(a, b):
    return jnp.dot(a, b, preferred_element_type=f32)


def _silu(x):
    return x * jax.nn.sigmoid(x)


def _softplus(x):
    return jnp.maximum(x, 0.0) + jnp.log1p(jnp.exp(-jnp.abs(x)))


def _split2(x):
    hi = x.astype(bf16)
    lo = (x - hi.astype(f32)).astype(bf16)
    return hi, lo


def _split3(x):
    a = x.astype(bf16)
    r = x - a.astype(f32)
    b = r.astype(bf16)
    c = (r - b.astype(f32)).astype(bf16)
    return a, b, c


def _group_sumsq(x, ones_bd):
    hi, lo = _split2(x * x)
    return _dot(hi, ones_bd) + _dot(lo, ones_bd)


def _const_spec(shape):
    nd = len(shape)
    return pl.BlockSpec(shape, lambda *_: (0,) * nd)


def _mod_kernel(c_ref, w_ref, b_ref, o_ref):
    c = c_ref[...]
    o_ref[...] = _dot(_silu(c).astype(bf16), w_ref[...].astype(bf16)) + b_ref[...]


def _mod_call(cond8, w_mod, b_mod):
    tn = 1536
    n = w_mod.shape[1]
    return pl.pallas_call(
        _mod_kernel,
        grid=(n // tn,),
        in_specs=[_const_spec((8, D_MODEL)),
                  pl.BlockSpec((D_MODEL, tn), lambda j: (0, j)),
                  pl.BlockSpec((1, tn), lambda j: (0, j))],
        out_specs=pl.BlockSpec((8, tn), lambda j: (0, j)),
        out_shape=jax.ShapeDtypeStruct((8, n), f32),
        compiler_params=_cparams(("arbitrary",)),
        name="mod",
    )(cond8, w_mod, b_mod)


def _rope(x, cos, sin):
    w = x.shape[-1]
    lane = lax.broadcasted_iota(jnp.int32, x.shape, 1)
    first = (lane % 32) < 16
    swapped = jnp.where(first, pltpu.roll(x, w - 16, 1), pltpu.roll(x, 16, 1))
    return x * cos + swapped * sin


def _proj_kernel(*refs, rope, seq_per_tile):
    (x_ref, mod_ref, nmix_ref, wmain_ref, wba_ref, wbat_ref, qn_ref, kn_ref, ones_ref) = refs[:9]
    pos = 9
    if rope:
        cos_ref, sin_ref = refs[9:11]
        pos = 11
    (q_out, k_out, v_out, dqkv_out, ba_out, bat_out, dgate_out, sga_out, sgd_out) = refs[pos:]

    x = x_ref[...]
    mod = mod_ref[0]
    shift1 = mod[:, 0:D_MODEL]
    scale1 = mod[:, D_MODEL:2 * D_MODEL]
    ms = jnp.mean(x * x, axis=-1, keepdims=True)
    h = x * lax.rsqrt(ms + EPS) * nmix_ref[...]
    h = h * (1.0 + scale1) + shift1
    hb = h.astype(bf16)

    aq = _dot(hb, wmain_ref[:, OFF_Q:OFF_Q + ATTN_Q_W])
    ss = _group_sumsq(aq, ones_ref[...])
    aq = aq * lax.rsqrt(ss * (1.0 / HEAD_DIM) + EPS) * qn_ref[...]
    if rope:
        aq = _rope(aq, cos_ref[...], sin_ref[...])
    q_out[...] = (aq * (HEAD_DIM ** -0.5)).astype(bf16)

    ak = _dot(hb, wmain_ref[:, OFF_K:OFF_K + ATTN_KV_W])
    ss = _group_sumsq(ak, ones_ref[0:ATTN_KV_W, 0:ATTN_KV_W])
    ak = ak * lax.rsqrt(ss * (1.0 / HEAD_DIM) + EPS) * kn_ref[...]
    av = _dot(hb, wmain_ref[:, OFF_V:OFF_V + ATTN_KV_W])
    if rope:
        ak = _rope(ak, cos_ref[:, 0:ATTN_KV_W], sin_ref[:, 0:ATTN_KV_W])
        for kv in range(ATTN_KV_HEADS):
            k_out[0, kv] = ak[:, kv * HEAD_DIM:(kv + 1) * HEAD_DIM].astype(k_out.dtype)
            v_out[0, kv] = av[:, kv * HEAD_DIM:(kv + 1) * HEAD_DIM].astype(v_out.dtype)
    else:
        tm = x.shape[0]
        seq = tm // seq_per_tile
        for kv in range(ATTN_KV_HEADS):
            k_out[:, 0, kv] = ak[:, kv * HEAD_DIM:(kv + 1) * HEAD_DIM].reshape(seq_per_tile, seq, HEAD_DIM)
            v_out[:, 0, kv] = av[:, kv * HEAD_DIM:(kv + 1) * HEAD_DIM].reshape(seq_per_tile, seq, HEAD_DIM)

    dqkv_out[...] = _dot(hb, wmain_ref[:, OFF_DQKV:OFF_DGATE]).astype(bf16)
    ba_out[...] = _dot(hb, wba_ref[...])
    bat_out[...] = lax.dot_general(wbat_ref[...], hb, _TRANS_B, preferred_element_type=f32)
    dgate_out[...] = _dot(hb, wmain_ref[:, OFF_DGATE:OFF_GATTN]).astype(bf16)
    sga_out[...] = jax.nn.sigmoid(_dot(hb, wmain_ref[:, OFF_GATTN:OFF_GDN])).astype(bf16)
    sgd_out[...] = jax.nn.sigmoid(_dot(hb, wmain_ref[:, OFF_GDN:W_MAIN])).astype(bf16)


def _proj_call(x2d, mod3, mod_row, nmix, wmain, wba, wbat, qn, kn, ones_bd, rope_tabs, n_seq, seq_len):
    t = x2d.shape[0]
    tm = 512
    rope = rope_tabs is not None
    tiles_per_seq = max(seq_len // tm, 1)
    seq_per_tile = max(tm // seq_len, 1)
    in_specs = [pl.BlockSpec((tm, D_MODEL), lambda i: (i, 0)),
                pl.BlockSpec((1, 1, N_MOD * D_MODEL), lambda i: (mod_row(i // tiles_per_seq), 0, 0)),
                _const_spec((1, D_MODEL)),
                _const_spec((D_MODEL, W_MAIN)),
                _const_spec((D_MODEL, 2 * N_GATES)),
                _const_spec((2 * N_GATES, D_MODEL)),
                _const_spec((1, ATTN_Q_W)),
                _const_spec((1, ATTN_KV_W)),
                _const_spec((ATTN_Q_W, ATTN_Q_W))]
    args = [x2d, mod3, nmix, wmain, wba, wbat, qn, kn, ones_bd]
    if rope:
        in_specs += [pl.BlockSpec((tm, ATTN_Q_W), lambda i: (i % tiles_per_seq, 0))] * 2
        args += list(rope_tabs)
        kv_shape = jax.ShapeDtypeStruct((n_seq, ATTN_KV_HEADS, seq_len, HEAD_DIM), bf16)
        kv_spec = pl.BlockSpec((1, ATTN_KV_HEADS, tm, HEAD_DIM),
                               lambda i: (i // tiles_per_seq, 0, i % tiles_per_seq, 0))
    else:
        kv_shape = jax.ShapeDtypeStruct((n_seq, 1, ATTN_KV_HEADS, seq_len, HEAD_DIM), f32)
        kv_spec = pl.BlockSpec((seq_per_tile, 1, ATTN_KV_HEADS, seq_len, HEAD_DIM), lambda i: (i, 0, 0, 0, 0))

    def row_spec(w):
        return pl.BlockSpec((tm, w), lambda i: (i, 0))

    out_shape = (jax.ShapeDtypeStruct((t, ATTN_Q_W), bf16), kv_shape, kv_shape,
                 jax.ShapeDtypeStruct((t, 3 * DN_W), bf16),
                 jax.ShapeDtypeStruct((t, 2 * N_GATES), f32),
                 jax.ShapeDtypeStruct((2 * N_GATES, t), f32),
                 jax.ShapeDtypeStruct((t, DN_W), bf16),
                 jax.ShapeDtypeStruct((t, D_MODEL), bf16),
                 jax.ShapeDtypeStruct((t, D_MODEL), bf16))
    out_specs = (row_spec(ATTN_Q_W), kv_spec, kv_spec, row_spec(3 * DN_W), row_spec(2 * N_GATES),
                 pl.BlockSpec((2 * N_GATES, tm), lambda i: (0, i)),
                 row_spec(DN_W), row_spec(D_MODEL), row_spec(D_MODEL))
    return pl.pallas_call(
        functools.partial(_proj_kernel, rope=rope, seq_per_tile=seq_per_tile),
        grid=(t // tm,),
        in_specs=in_specs,
        out_specs=out_specs,
        out_shape=out_shape,
        compiler_params=_cparams(("arbitrary",)),
        name="proj_lat" if rope else "proj_ctx",
    )(*args)


def _attn_kernel(*refs, n_sets):
    q_ref = refs[0]
    kv_refs = refs[1:1 + 2 * n_sets]
    rep_ref = refs[1 + 2 * n_sets]
    o_ref = refs[-1]
    rep = rep_ref[...]
    width = ATTN_GROUP * HEAD_DIM
    blk = lax.broadcasted_iota(jnp.int32, (1, width), 1) // HEAD_DIM

    def head_slab(ref, kv):
        x = ref[0, 0, kv] if len(ref.shape) == 5 else ref[0, kv]
        return _dot(x.astype(bf16), rep)

    for kv in range(ATTN_KV_HEADS):
        q = q_ref[0, :, kv * width:(kv + 1) * width]
        k4 = [head_slab(kv_refs[2 * s], kv) for s in range(n_sets)]
        v4 = [head_slab(kv_refs[2 * s + 1], kv) for s in range(n_sets)]
        acc = jnp.zeros((q.shape[0], width), f32)
        for g in range(ATTN_GROUP):
            sel = blk == g
            scores = [lax.dot_general(q, jnp.where(sel, k, 0.0).astype(bf16), _TRANS_B,
                                      preferred_element_type=f32) for k in k4]
            m = scores[0].max(axis=-1, keepdims=True)
            for s in scores[1:]:
                m = jnp.maximum(m, s.max(axis=-1, keepdims=True))
            probs = [jnp.exp(s - m) for s in scores]
            denom = probs[0].sum(axis=-1, keepdims=True)
            for p in probs[1:]:
                denom = denom + p.sum(axis=-1, keepdims=True)
            og = _dot(probs[0].astype(bf16), jnp.where(sel, v4[0], 0.0).astype(bf16))
            for p, v in zip(probs[1:], v4[1:]):
                og = og + _dot(p.astype(bf16), jnp.where(sel, v, 0.0).astype(bf16))
            acc = acc + og * (1.0 / denom)
        o_ref[0, :, kv * width:(kv + 1) * width] = acc.astype(o_ref.dtype)


def _attn_call(q3, kv_sets, rep, tq, name):
    b, l, _ = q3.shape
    in_specs = [pl.BlockSpec((1, tq, ATTN_Q_W), lambda i, t: (i, t, 0))]
    args = [q3]
    for arr in kv_sets:
        nd = arr.ndim
        blk = (1,) + arr.shape[1:]
        in_specs.append(pl.BlockSpec(blk, lambda i, t, nd=nd: (i,) + (0,) * (nd - 1)))
        args.append(arr)
    in_specs.append(_const_spec(rep.shape))
    args.append(rep)
    return pl.pallas_call(
        functools.partial(_attn_kernel, n_sets=len(kv_sets) // 2),
        grid=(b, l // tq),
        in_specs=in_specs,
        out_specs=pl.BlockSpec((1, tq, ATTN_Q_W), lambda i, t: (i, t, 0)),
        out_shape=jax.ShapeDtypeStruct((b, l, ATTN_Q_W), bf16),
        compiler_params=_cparams(("arbitrary", "arbitrary")),
        name=name,
    )(*args)


def _dn_kernel(*refs, seq_len, has_s0, want_state):
    it = iter(refs)
    dqkv_ref = next(it)
    ba_ref = next(it)
    bar_ref = next(it)
    barp_ref = next(it)
    dgate_ref = next(it)
    s0_ref = next(it) if has_s0 else None
    convw_ref = next(it)
    alog_c = next(it)
    dtb_c = next(it)
    alog_r = next(it)
    dtb_r = next(it)
    alog_p = next(it)
    dtb_p = next(it)
    dnn_ref = next(it)
    ones_ref = next(it)
    eye_ref = next(it)
    expand_ref = next(it)
    o_ref = next(it)
    st_ref = next(it) if want_state else None
    q_s, k_s, v_s, kt_s, gx_s, bx_s, grp_s, gr_s, of_s, ob_s, s_s = it

    n = seq_len // CHUNK
    c = CHUNK
    pw = 2 * DN_DK
    n_pairs = DN_HEADS // 2
    r64 = lax.broadcasted_iota(jnp.int32, (c, c), 0)
    c64 = lax.broadcasted_iota(jnp.int32, (c, c), 1)
    tril = jnp.where(r64 >= c64, 1.0, 0.0).astype(bf16)
    triu = jnp.where(r64 <= c64, 1.0, 0.0).astype(bf16)
    rp = lax.broadcasted_iota(jnp.int32, (c, pw), 0)
    cp = lax.broadcasted_iota(jnp.int32, (c, pw), 1) % c
    lower_p = rp >= cp
    upper_p = rp <= cp
    eye_p = jnp.where(rp == cp, 1.0, 0.0)
    same8 = (rp // 8) == (cp // 8)
    off16 = ((rp // 16) == (cp // 16)) & ~same8
    off32 = ((rp // 32) == (cp // 32)) & ((rp // 16) != (cp // 16))
    off64 = (rp // 32) != (cp // 32)
    rb2 = lax.broadcasted_iota(jnp.int32, (pw, pw), 0)
    cb2 = lax.broadcasted_iota(jnp.int32, (pw, pw), 1)
    bd_mask = (rb2 // c) == (cb2 // c)
    bd_triu = jnp.where(bd_mask & (rb2 % c <= cb2 % c), 1.0, 0.0).astype(bf16)
    bd_tril = jnp.where(bd_mask & (rb2 % c >= cb2 % c), 1.0, 0.0).astype(bf16)
    lane16 = lax.broadcasted_iota(jnp.int32, (1, N_GATES), 1)
    sub16 = lax.broadcasted_iota(jnp.int32, (N_GATES, 1), 0)
    sub8 = lax.broadcasted_iota(jnp.int32, (2 * n_pairs, 1), 0)
    sub_pw = lax.broadcasted_iota(jnp.int32, (pw, 1), 0)
    row_c = lax.broadcasted_iota(jnp.int32, (c, 1), 0)

    def bdiag(x):
        xb = x.astype(bf16)
        return jnp.where(bd_mask, jnp.concatenate([xb, xb], axis=0), jnp.zeros((), bf16))

    def pre_body(ci, carry):
        r0 = pl.multiple_of(ci * c, c)
        xb = dqkv_ref[0, pl.ds(r0, c), :].astype(f32)
        p0 = pl.multiple_of(jnp.maximum(r0 - 16, 0), 16)
        n0 = pl.multiple_of(jnp.minimum(r0 + c, seq_len - 16), 16)
        prev = dqkv_ref[0, pl.ds(p0, 16), :].astype(f32)[15:16, :]
        nxt = dqkv_ref[0, pl.ds(n0, 16), :].astype(f32)[0:1, :]
        prev = jnp.where(ci > 0, prev, 0.0)
        nxt = jnp.where(ci < n - 1, nxt, 0.0)
        xm = jnp.where(row_c == 0, prev, pltpu.roll(xb, 1, 0))
        xp = jnp.where(row_c == c - 1, nxt, pltpu.roll(xb, c - 1, 0))
        w = convw_ref[...]
        a = _silu(xm * w[0:1, :] + xb * w[1:2, :] + xp * w[2:3, :])
        q = a[:, 0:DN_W]
        k = a[:, DN_W:2 * DN_W]
        ones_bd = ones_ref[...]
        q_s[pl.ds(r0, c), :] = q * lax.rsqrt(_group_sumsq(q, ones_bd) + EPS) * (DN_DK ** -0.5)
        kn = k * lax.rsqrt(_group_sumsq(k, ones_bd) + EPS)
        k_s[pl.ds(r0, c), :] = kn
        v_s[pl.ds(r0, c), :] = a[:, 2 * DN_W:3 * DN_W]
        kb = kn.astype(bf16)
        kt_s[ci] = lax.dot_general(eye_ref[...], jnp.concatenate([kb, kb], axis=0), _TRANS_B,
                                   preferred_element_type=f32).astype(bf16)

        ba = ba_ref[0, pl.ds(r0, c), :]
        beta = jax.nn.sigmoid(ba[:, 0:N_GATES])
        g = -jnp.exp(alog_c[...]) * _softplus(ba[:, N_GATES:2 * N_GATES] + dtb_c[...])
        gsp = _split3(g)
        fwd = sum(_dot(tril, p) for p in gsp)
        bwd = sum(_dot(triu, p) for p in gsp)
        gcum = jnp.where(lane16 < DN_HEADS, fwd, bwd)
        expand = expand_ref[...]
        gx_s[pl.ds(r0, c), :] = sum(_dot(p, expand) for p in _split3(gcum))
        bx_s[pl.ds(r0, c), :] = sum(_dot(p, expand) for p in _split3(beta))

        bar = bar_ref[0, ci]
        g_r = -jnp.exp(alog_r[...]) * _softplus(bar[N_GATES:2 * N_GATES, :] + dtb_r[...])
        grsp = _split3(g_r)
        fwd_r = sum(_dot(p, triu) for p in grsp)
        bwd_r = sum(_dot(p, tril) for p in grsp)
        gr_s[ci] = jnp.where(sub16 < DN_HEADS, fwd_r, bwd_r)

        g_p = -jnp.exp(alog_p[...]) * _softplus(barp_ref[0, ci] + dtb_p[...])
        gpsp = _split3(g_p)
        fwd_p = sum(_dot(p, bd_triu) for p in gpsp)
        bwd_p = sum(_dot(p, bd_tril) for p in gpsp)
        grp_s[ci] = jnp.where(sub8 < n_pairs, fwd_p, bwd_p)
        return carry

    lax.fori_loop(0, n, pre_body, 0)

    zero_blk = jnp.zeros((DN_DK, DN_DK), f32)
    for d in range(2):
        for p in range(n_pairs):
            if has_s0:
                top = jnp.concatenate([s0_ref[0, 0, d, 2 * p], zero_blk], axis=1)
                bot = jnp.concatenate([zero_blk, s0_ref[0, 0, d, 2 * p + 1]], axis=1)
                s_s[d * n_pairs + p] = jnp.concatenate([top, bot], axis=0)
            else:
                s_s[d * n_pairs + p] = jnp.zeros((pw, pw), f32)

    def each(fn, *lists):
        return [fn(*vals) for vals in zip(*lists)]

    def scan_body(i, carry):
        qs, ks, vs, bx, eg, el, decay, strict, ktbd, ktd = [], [], [], [], [], [], [], [], [], []
        rows = []
        for d in range(2):
            ci = i if d == 0 else n - 1 - i
            r0 = pl.multiple_of(ci * c, c)
            rows.append(r0)
            grp = grp_s[ci]
            gr16 = gr_s[ci]
            g_end = gr16[:, c - 1:c] if d == 0 else gr16[:, 0:1]
            e_rest_t = jnp.exp(g_end - gr16)
            incl = lower_p if d == 0 else upper_p
            for p in range(n_pairs):
                pp = d * n_pairs + p
                j0 = d * DN_HEADS + 2 * p
                sl = slice(p * pw, (p + 1) * pw)
                xl = slice(pp * pw, (pp + 1) * pw)
                qs.append(q_s[pl.ds(r0, c), sl])
                ks.append(k_s[pl.ds(r0, c), sl])
                vs.append(v_s[pl.ds(r0, c), sl])
                gx = gx_s[pl.ds(r0, c), xl]
                bx.append(bx_s[pl.ds(r0, c), xl])
                g_last = gx[c - 1:c, :] if d == 0 else gx[0:1, :]
                eg.append(jnp.exp(gx))
                el.append(jnp.exp(g_last))
                diff = gx - grp[pp:pp + 1, :]
                decay.append(jnp.where(incl, jnp.exp(jnp.where(incl, diff, 0.0)), 0.0))
                strict.append((rp > cp) if d == 0 else (rp < cp))
                kt = kt_s[ci, p * pw:(p + 1) * pw, :]
                ktbd.append(jnp.where(bd_mask, kt, jnp.zeros((), bf16)))
                fac = jnp.where(sub_pw < c, e_rest_t[j0:j0 + 1, :], e_rest_t[j0 + 1:j0 + 2, :])
                ktd.append((kt[:, 0:c].astype(f32) * fac).astype(bf16))

        kq = each(lambda k, q, kt: _dot(jnp.concatenate([k, q], axis=0).astype(bf16), kt), ks, qs, ktbd)
        lmat = each(lambda m, x, b, dc: jnp.where(m, x[0:c] * b * dc, 0.0), strict, kq, bx, decay)
        amat = each(lambda x, dc: (x[c:2 * c] * dc).astype(bf16), kq, decay)
        d1 = each(lambda l: jnp.where(same8, l, 0.0), lmat)
        d2 = each(lambda a: _dot(a.astype(bf16), bdiag(a)), d1)
        tmat = each(lambda a: eye_p - a, d1)
        tmat = each(lambda t, a: t + _dot(a.astype(bf16), bdiag(t)), tmat, d2)
        d4 = each(lambda a: _dot(a.astype(bf16), bdiag(a)).astype(bf16), d2)
        tmat = each(lambda t, a: t + _dot(a, bdiag(t)), tmat, d4)
        for off_mask in (off16, off32, off64):
            tbd = each(bdiag, tmat)
            tl = each(lambda t, l: _dot(t.astype(bf16), bdiag(jnp.where(off_mask, l, 0.0))).astype(bf16), tmat, lmat)
            tmat = each(lambda t, a, b: t - _dot(a, b), tmat, tl, tbd)
        uw = each(lambda t, v, k, b, e: _dot(t.astype(bf16), jnp.concatenate([bdiag(v * b), bdiag(k * (b * e))], axis=1)),
                  tmat, vs, ks, bx, eg)
        s_old = [s_s[pp] for pp in range(2 * n_pairs)]
        wq = each(lambda y, q, e, s: _dot(jnp.concatenate([y[:, pw:2 * pw], q * e], axis=0).astype(bf16),
                                          s.astype(bf16)), uw, qs, eg, s_old)
        v_new = each(lambda y, z: y[:, 0:pw] - z[0:c], uw, wq)
        outs = each(lambda z, a, vn: z[c:2 * c] + _dot(a, bdiag(vn)), wq, amat, v_new)
        s_new = each(lambda s, e, kd, vn: s * e + jnp.where(bd_mask, _dot(kd, vn.astype(bf16)), 0.0),
                     s_old, el, ktd, v_new)
        for pp in range(2 * n_pairs):
            s_s[pp] = s_new[pp]
        of_s[pl.ds(rows[0], c), :] = jnp.concatenate(outs[0:n_pairs], axis=1)
        ob_s[pl.ds(rows[1], c), :] = jnp.concatenate(outs[n_pairs:2 * n_pairs], axis=1)
        return carry

    lax.fori_loop(0, n, scan_body, 0)

    rb = 128

    def post_body(bi, carry):
        r0 = pl.multiple_of(bi * rb, rb)
        o = of_s[pl.ds(r0, rb), :] + ob_s[pl.ds(r0, rb), :]
        o = o * lax.rsqrt(_group_sumsq(o, ones_ref[...]) * (1.0 / DN_DK) + EPS) * dnn_ref[...]
        gate = dgate_ref[0, pl.ds(r0, rb), :].astype(f32)
        o_ref[0, pl.ds(r0, rb), :] = (o * _silu(gate)).astype(o_ref.dtype)
        return carry

    lax.fori_loop(0, seq_len // rb, post_body, 0)

    if want_state:
        for d in range(2):
            for p in range(n_pairs):
                s = s_s[d * n_pairs + p]
                st_ref[0, 0, d, 2 * p] = s[0:DN_DK, 0:DN_DK]
                st_ref[0, 0, d, 2 * p + 1] = s[DN_DK:pw, DN_DK:pw]


def _dn_call(dqkv3, ba3, bar4, barp4, dgate3, s0, consts, want_state, name):
    b, l, _ = dqkv3.shape
    n = l // CHUNK
    n_pairs = DN_HEADS // 2
    pw = 2 * DN_DK
    state_blk = (1, 1, 2, DN_HEADS, DN_DK, DN_DK)
    state_spec = pl.BlockSpec(state_blk, lambda i: (i, 0, 0, 0, 0, 0))
    in_specs = [pl.BlockSpec((1, l, 3 * DN_W), lambda i: (i, 0, 0)),
                pl.BlockSpec((1, l, 2 * N_GATES), lambda i: (i, 0, 0)),
                pl.BlockSpec((1, n, 2 * N_GATES, CHUNK), lambda i: (i, 0, 0, 0)),
                pl.BlockSpec((1, n, 2 * n_pairs, pw), lambda i: (i, 0, 0, 0)),
                pl.BlockSpec((1, l, DN_W), lambda i: (i, 0, 0))]
    args = [dqkv3, ba3, bar4, barp4, dgate3]
    if s0 is not None:
        in_specs.append(state_spec)
        args.append(s0)
    in_specs += [_const_spec(a.shape) for a in consts]
    args += consts
    out_shape = [jax.ShapeDtypeStruct((b, l, DN_W), bf16)]
    out_specs = [pl.BlockSpec((1, l, DN_W), lambda i: (i, 0, 0))]
    if want_state:
        out_shape.append(jax.ShapeDtypeStruct((b,) + state_blk[1:], f32))
        out_specs.append(state_spec)
    scratch = [pltpu.VMEM((l, DN_W), f32), pltpu.VMEM((l, DN_W), f32), pltpu.VMEM((l, DN_W), f32),
               pltpu.VMEM((n, DN_W, pw), bf16),
               pltpu.VMEM((l, N_GATES * DN_DK), f32), pltpu.VMEM((l, N_GATES * DN_DK), f32),
               pltpu.VMEM((n, 2 * n_pairs, pw), f32), pltpu.VMEM((n, N_GATES, CHUNK), f32),
               pltpu.VMEM((l, DN_W), f32), pltpu.VMEM((l, DN_W), f32),
               pltpu.VMEM((2 * n_pairs, pw, pw), f32)]
    res = pl.pallas_call(
        functools.partial(_dn_kernel, seq_len=l, has_s0=s0 is not None, want_state=want_state),
        grid=(b,),
        in_specs=in_specs,
        out_specs=tuple(out_specs),
        out_shape=tuple(out_shape),
        scratch_shapes=scratch,
        compiler_params=_cparams(("arbitrary",)),
        name=name,
    )(*args)
    return res if want_state else (res[0], None)


def _merge_kernel(oa_ref, od_ref, sga_ref, sgd_ref, x_ref, mod_ref, wa_ref, wd_ref, wo_ref, nffn_ref,
                  wr_ref, br_ref, x1_out, h2_out, comb_out):
    mod = mod_ref[0]
    gate1 = mod[:, 2 * D_MODEL:3 * D_MODEL]
    shift2 = mod[:, 3 * D_MODEL:4 * D_MODEL]
    scale2 = mod[:, 4 * D_MODEL:5 * D_MODEL]
    merged = (sga_ref[...].astype(f32) * _dot(oa_ref[...], wa_ref[...])
              + sgd_ref[...].astype(f32) * _dot(od_ref[...], wd_ref[...]))
    m = _dot(merged.astype(bf16), wo_ref[...])
    x1 = x_ref[...] + gate1 * m
    x1_out[...] = x1
    ms = jnp.mean(x1 * x1, axis=-1, keepdims=True)
    h2 = x1 * lax.rsqrt(ms + EPS) * nffn_ref[...]
    h2 = h2 * (1.0 + scale2) + shift2
    h2_out[...] = h2.astype(bf16)

    h_hi, h_lo = _split2(h2)
    w_hi, w_lo = _split2(wr_ref[...])
    logits = _dot(h_hi, w_hi) + _dot(h_hi, w_lo) + _dot(h_lo, w_hi) + br_ref[...]
    lane = lax.broadcasted_iota(jnp.int32, logits.shape, 1)
    big = ROUTER_LANES
    neg = -jnp.inf

    def first_lane(hit):
        return jnp.min(jnp.where(hit, lane, big), axis=-1, keepdims=True)

    is_g = lane < N_GROUPS
    gl = jnp.where(is_g, logits, neg)
    gexp = jnp.where(is_g, jnp.exp(gl - gl.max(axis=-1, keepdims=True)), 0.0)
    gp = gexp / gexp.sum(axis=-1, keepdims=True)
    g_top = gp.max(axis=-1, keepdims=True)
    g_idx = first_lane(is_g & (gp == g_top))
    lo_lane = N_GROUPS + g_idx * EXPERTS_PER_GROUP
    is_e = (lane >= lo_lane) & (lane < lo_lane + EXPERTS_PER_GROUP)
    el = jnp.where(is_e, logits, neg)
    eexp = jnp.where(is_e, jnp.exp(el - el.max(axis=-1, keepdims=True)), 0.0)
    ep = jnp.where(is_e, eexp / eexp.sum(axis=-1, keepdims=True), -1.0)
    p1 = ep.max(axis=-1, keepdims=True)
    i1 = first_lane(ep == p1)
    ep2 = jnp.where(lane == i1, -1.0, ep)
    p2 = ep2.max(axis=-1, keepdims=True)
    i2 = first_lane(ep2 == p2)
    tot = p1 + p2
    comb_out[...] = (jnp.where(lane == i1, p1 / tot * g_top, 0.0)
                     + jnp.where(lane == i2, p2 / tot * g_top, 0.0))


def _merge_call(oa, od, sga, sgd, x2d, mod3, mod_row, wa, wd, wo, nffn, wr, br, seq_len, name):
    t = x2d.shape[0]
    tm = 512
    tiles_per_seq = max(seq_len // tm, 1)

    def row_spec(w):
        return pl.BlockSpec((tm, w), lambda i: (i, 0))

    return pl.pallas_call(
        _merge_kernel,
        grid=(t // tm,),
        in_specs=[row_spec(ATTN_Q_W), row_spec(DN_W), row_spec(D_MODEL), row_spec(D_MODEL), row_spec(D_MODEL),
                  pl.BlockSpec((1, 1, N_MOD * D_MODEL), lambda i: (mod_row(i // tiles_per_seq), 0, 0)),
                  _const_spec(wa.shape), _const_spec(wd.shape), _const_spec(wo.shape),
                  _const_spec((1, D_MODEL)), _const_spec(wr.shape), _const_spec(br.shape)],
        out_specs=(row_spec(D_MODEL), row_spec(D_MODEL), row_spec(ROUTER_LANES)),
        out_shape=(jax.ShapeDtypeStruct((t, D_MODEL), f32),
                   jax.ShapeDtypeStruct((t, D_MODEL), bf16),
                   jax.ShapeDtypeStruct((t, ROUTER_LANES), f32)),
        compiler_params=_cparams(("arbitrary",)),
        name=name,
    )(oa, od, sga, sgd, x2d, mod3, wa, wd, wo, nffn, wr, br)


def _moe_kernel(h2_ref, comb_ref, x1_ref, mod_ref, wgu_ref, wdn_ref, nfin_ref, y_ref, acc_ref):
    e = pl.program_id(1)

    @pl.when(e == 0)
    def _():
        acc_ref[...] = jnp.zeros_like(acc_ref)

    gu = _dot(h2_ref[...], wgu_ref[0])
    hid = _silu(gu[:, 0:D_FF_EXPERT]) * gu[:, D_FF_EXPERT:2 * D_FF_EXPERT]
    comb = comb_ref[...]
    lane = lax.broadcasted_iota(jnp.int32, comb.shape, 1)
    ce = jnp.sum(jnp.where(lane == e + N_GROUPS, comb, 0.0), axis=-1, keepdims=True)
    acc_ref[...] += _dot((hid * ce).astype(bf16), wdn_ref[0])

    @pl.when(e == N_EXPERTS - 1)
    def _():
        gate2 = mod_ref[0][:, 5 * D_MODEL:6 * D_MODEL]
        x2 = x1_ref[...] + gate2 * acc_ref[...]
        ms = jnp.mean(x2 * x2, axis=-1, keepdims=True)
        y_ref[...] = x2 * lax.rsqrt(ms + EPS) * nfin_ref[...]


def _moe_call(h2, comb, x1, mod3, mod_row, wgu, wdn, nfin, seq_len, name):
    t = h2.shape[0]
    tm = 1024
    tiles_per_seq = max(seq_len // tm, 1)
    return pl.pallas_call(
        _moe_kernel,
        grid=(t // tm, N_EXPERTS),
        in_specs=[pl.BlockSpec((tm, D_MODEL), lambda i, e: (i, 0)),
                  pl.BlockSpec((tm, ROUTER_LANES), lambda i, e: (i, 0)),
                  pl.BlockSpec((tm, D_MODEL), lambda i, e: (i, 0)),
                  pl.BlockSpec((1, 1, N_MOD * D_MODEL), lambda i, e: (mod_row(i // tiles_per_seq), 0, 0)),
                  pl.BlockSpec((1, D_MODEL, 2 * D_FF_EXPERT), lambda i, e: (e, 0, 0)),
                  pl.BlockSpec((1, D_FF_EXPERT, D_MODEL), lambda i, e: (e, 0, 0)),
                  pl.BlockSpec((1, D_MODEL), lambda i, e: (0, 0))],
        out_specs=pl.BlockSpec((tm, D_MODEL), lambda i, e: (i, 0)),
        out_shape=jax.ShapeDtypeStruct((t, D_MODEL), f32),
        scratch_shapes=[pltpu.VMEM((tm, D_MODEL), f32)],
        compiler_params=_cparams(("arbitrary", "arbitrary")),
        name=name,
    )(h2, comb, x1, mod3, wgu, wdn, nfin)


def _rope_tables(seq_len):
    t = np.arange(seq_len)
    row = (t // GRID_W).astype(np.float64)
    col = (t % GRID_W).astype(np.float64)
    half = HEAD_DIM // 2
    inv = np.power(ROPE_THETA, -np.arange(0, half, 2, dtype=np.float64) / half)
    d = np.arange(HEAD_DIM)
    freq = inv[d % (half // 2)]
    pos = np.where(d[None, :] < half, row[:, None], col[:, None])
    ang = pos * freq[None, :]
    sign = np.where((d % half) < half // 2, -1.0, 1.0)
    cos = np.tile(np.cos(ang), (1, ATTN_HEADS)).astype(np.float32)
    sin = np.tile(np.sin(ang) * sign[None, :], (1, ATTN_HEADS)).astype(np.float32)
    return jnp.asarray(cos), jnp.asarray(sin)


def _ones_block_diag():
    i = np.arange(ATTN_Q_W)
    return jnp.asarray((i[:, None] // HEAD_DIM == i[None, :] // HEAD_DIM).astype(np.float32), dtype=bf16)


def _lane_replicator():
    i = np.arange(HEAD_DIM)
    j = np.arange(ATTN_GROUP * HEAD_DIM)
    return jnp.asarray((i[:, None] == j[None, :] % HEAD_DIM).astype(np.float32), dtype=bf16)


def _trunk(x3, mod3, mod_row, wts, prefix):
    n_seq, seq_len, _ = x3.shape
    t = n_seq * seq_len
    x2d = x3.reshape(t, D_MODEL)
    is_lat = prefix is not None
    rope_tabs = _rope_tables(seq_len) if is_lat else None

    q, k, v, dqkv, ba, bat, dgate, sga, sgd = _proj_call(
        x2d, mod3, mod_row, wts["nmix"], wts["wmain"], wts["wba"], wts["wbat"], wts["qn"], wts["kn"],
        wts["ones_bd"], rope_tabs, n_seq, seq_len)

    q3 = q.reshape(n_seq, seq_len, ATTN_Q_W)
    if is_lat:
        pk, pv, s0 = prefix
        o_attn = _attn_call(q3, [pk, pv, k, v], wts["rep"], 256, "attn_lat")
    else:
        s0 = None
        o_attn = _attn_call(q3, [k, v], wts["rep"], seq_len, "attn_ctx")

    n_chunks = seq_len // CHUNK
    bar4 = bat.reshape(2 * N_GATES, n_seq, n_chunks, CHUNK).transpose(1, 2, 0, 3)
    barp4 = bat[N_GATES:].reshape(N_GATES // 2, 2, n_seq, n_chunks, CHUNK).transpose(2, 3, 0, 1, 4).reshape(
        n_seq, n_chunks, N_GATES // 2, 2 * CHUNK)
    dn_consts = [wts[name] for name in ("convw", "alog_c", "dtb_c", "alog_r", "dtb_r", "alog_p", "dtb_p", "dnn",
                                        "ones_bd", "eye", "expand")]
    o_dn, state = _dn_call(
        dqkv.reshape(n_seq, seq_len, 3 * DN_W), ba.reshape(n_seq, seq_len, 2 * N_GATES), bar4, barp4,
        dgate.reshape(n_seq, seq_len, DN_W), s0, dn_consts, not is_lat, "dn_lat" if is_lat else "dn_ctx")

    x1, h2, comb = _merge_call(
        o_attn.reshape(t, ATTN_Q_W), o_dn.reshape(t, DN_W), sga, sgd, x2d, mod3, mod_row,
        wts["wa"], wts["wd"], wts["wo"], wts["nffn"], wts["wr"], wts["br"], seq_len,
        "merge_lat" if is_lat else "merge_ctx")

    y = _moe_call(h2, comb, x1, mod3, mod_row, wts["wgu"], wts["wdn"], wts["nfin"], seq_len,
                  "moe_lat" if is_lat else "moe_ctx")
    return y.reshape(n_seq, seq_len, D_MODEL), k, v, state


def kernel(x_prompt, x_sample, c, cache_attn_k, cache_attn_v, state_delta, c_ctx, w_mod, b_mod, norm_mix, norm_ffn, norm_final, w_in, q_norm, k_norm, conv_w, a_log, dt_bias, dn_norm, w_attn_br, w_dn_br, w_out, w_rg, b_rg, w_re, b_re, w_gate_e, w_up_e, w_down_e):
    layer = 0
    n_lat = x_sample.shape[0]
    w_in_l = w_in[layer]
    ba_lo = OFF_DGATE
    wts = {
        "nmix": norm_mix[layer][None, :],
        "wmain": jnp.concatenate([w_in_l[:, :ba_lo], w_in_l[:, ba_lo + 2 * N_GATES:]], axis=1).astype(bf16),
        "wba": w_in_l[:, ba_lo:ba_lo + 2 * N_GATES].astype(bf16),
        "wbat": w_in_l[:, ba_lo:ba_lo + 2 * N_GATES].T.astype(bf16),
        "qn": jnp.tile(q_norm[layer], ATTN_HEADS)[None, :],
        "kn": jnp.tile(k_norm[layer], ATTN_KV_HEADS)[None, :],
        "ones_bd": _ones_block_diag(),
        "rep": _lane_replicator(),
        "convw": conv_w[layer],
        "alog_c": a_log[layer].reshape(1, N_GATES),
        "dtb_c": dt_bias[layer].reshape(1, N_GATES),
        "alog_r": a_log[layer].reshape(N_GATES, 1),
        "dtb_r": dt_bias[layer].reshape(N_GATES, 1),
        "alog_p": jnp.repeat(a_log[layer].reshape(N_GATES // 2, 2), CHUNK, axis=1),
        "dtb_p": jnp.repeat(dt_bias[layer].reshape(N_GATES // 2, 2), CHUNK, axis=1),
        "eye": jnp.eye(DN_W, dtype=bf16),
        "expand": jnp.asarray(np.repeat(np.eye(N_GATES, dtype=np.float32), DN_DK, axis=1), dtype=bf16),
        "dnn": jnp.tile(dn_norm[layer], DN_HEADS)[None, :],
        "wa": w_attn_br[layer].astype(bf16),
        "wd": w_dn_br[layer].astype(bf16),
        "wo": w_out[layer].astype(bf16),
        "nffn": norm_ffn[layer][None, :],
        "wr": jnp.concatenate([w_rg[layer], w_re[layer],
                               jnp.zeros((D_MODEL, ROUTER_LANES - N_GROUPS - N_EXPERTS), f32)], axis=1),
        "br": jnp.concatenate([b_rg[layer], b_re[layer],
                               jnp.zeros((ROUTER_LANES - N_GROUPS - N_EXPERTS,), f32)])[None, :],
        "wgu": jnp.concatenate([w_gate_e[layer], w_up_e[layer]], axis=-1).astype(bf16),
        "wdn": w_down_e[layer].astype(bf16),
        "nfin": norm_final[None, :],
    }
    cond8 = jnp.concatenate([c_ctx[None, :], c, jnp.zeros((8 - 1 - n_lat, D_MODEL), f32)], axis=0)
    mod3 = _mod_call(cond8, w_mod[layer], b_mod[layer][None, :])[:, None, :]

    y_prompt, new_k, new_v, new_state = _trunk(x_prompt, mod3, lambda s: 0, wts, None)
    y_sample, _, _, _ = _trunk(x_sample, mod3, lambda s: s + 1, wts,
                               (cache_attn_k, cache_attn_v, state_delta))
    return (y_prompt, y_sample, new_k, new_v, new_state)
```

```python
import functools

import numpy as np
import jax
import jax.numpy as jnp
from jax import lax
from jax.experimental import pallas as pl
from jax.experimental.pallas import tpu as pltpu

f32 = jnp.float32
bf16 = jnp.bfloat16

D_MODEL = 1024
HEAD_DIM = 64
ATTN_HEADS = 8
ATTN_KV_HEADS = 2
ATTN_GROUP = ATTN_HEADS // ATTN_KV_HEADS
GRID_W = 64
ROPE_THETA = 10000.0
DN_HEADS = 8
DN_DK = 64
CHUNK = 64
N_GROUPS = 4
EXPERTS_PER_GROUP = 4
N_EXPERTS = 16
D_FF_EXPERT = 256
N_MOD = 6
EPS = 1e-6

ATTN_Q_W = ATTN_HEADS * HEAD_DIM
ATTN_KV_W = ATTN_KV_HEADS * HEAD_DIM
DN_W = DN_HEADS * DN_DK
N_GATES = 2 * DN_HEADS

OFF_Q = 0
OFF_K = ATTN_Q_W
OFF_V = OFF_K + ATTN_KV_W
OFF_DQKV = OFF_V + ATTN_KV_W
OFF_DGATE = OFF_DQKV + 3 * DN_W
OFF_GATTN = OFF_DGATE + DN_W
OFF_GDN = OFF_GATTN + D_MODEL
W_MAIN = OFF_GDN + D_MODEL

ROUTER_LANES = 128
VMEM_LIMIT = 56 * 1024 * 1024

_TRANS_B = (((1,), (1,)), ((), ()))
_TRANS_A = (((0,), (0,)), ((), ()))


def _cparams(sem):
    return pltpu.CompilerParams(dimension_semantics=sem, vmem_limit_bytes=VMEM_LIMIT)


def _dot(a, b):
    return jnp.dot(a, b, preferred_element_type=f32)


def _silu(x):
    return x * jax.nn.sigmoid(x)


def _softplus(x):
    return jnp.maximum(x, 0.0) + jnp.log1p(jnp.exp(-jnp.abs(x)))


def _split2(x):
    hi = x.astype(bf16)
    lo = (x - hi.astype(f32)).astype(bf16)
    return hi, lo


def _split3(x):
    a = x.astype(bf16)
    r = x - a.astype(f32)
    b = r.astype(bf16)
    c = (r - b.astype(f32)).astype(bf16)
    return a, b, c


LANE_TILE = 128


def _group_sumsq_many(arrs, ones_pair):
    pieces = []
    for x in arrs:
        hi, lo = _split2(x * x)
        for s in range(x.shape[1] // LANE_TILE):
            pieces.append(hi[:, s * LANE_TILE:(s + 1) * LANE_TILE])
            pieces.append(lo[:, s * LANE_TILE:(s + 1) * LANE_TILE])
    res = _dot(jnp.concatenate(pieces, axis=0), ones_pair)
    out, off = [], 0
    for x in arrs:
        r = x.shape[0]
        slabs = []
        for s in range(x.shape[1] // LANE_TILE):
            slabs.append(res[off:off + r] + res[off + r:off + 2 * r])
            off += 2 * r
        out.append(slabs[0] if len(slabs) == 1 else jnp.concatenate(slabs, axis=1))
    return out


def _group_sumsq(x, ones_pair):
    return _group_sumsq_many([x], ones_pair)[0]


def _const_spec(shape):
    nd = len(shape)
    return pl.BlockSpec(shape, lambda *_: (0,) * nd)


def _mod_kernel(c_ref, w_ref, b_ref, o_ref):
    c = c_ref[...]
    o_ref[...] = _dot(_silu(c).astype(bf16), w_ref[...].astype(bf16)) + b_ref[...]


def _mod_call(cond8, w_mod, b_mod):
    tn = 1536
    n = w_mod.shape[1]
    return pl.pallas_call(
        _mod_kernel,
        grid=(n // tn,),
        in_specs=[_const_spec((8, D_MODEL)),
                  pl.BlockSpec((D_MODEL, tn), lambda j: (0, j)),
                  pl.BlockSpec((1, tn), lambda j: (0, j))],
        out_specs=pl.BlockSpec((8, tn), lambda j: (0, j)),
        out_shape=jax.ShapeDtypeStruct((8, n), f32),
        compiler_params=_cparams(("arbitrary",)),
        name="mod",
    )(cond8, w_mod, b_mod)


def _rope(x, cos, sin):
    w = x.shape[-1]
    lane = lax.broadcasted_iota(jnp.int32, x.shape, 1)
    first = (lane % 32) < 16
    swapped = jnp.where(first, pltpu.roll(x, w - 16, 1), pltpu.roll(x, 16, 1))
    return x * cos + swapped * sin


def _proj_kernel(*refs, rope, seq_per_tile):
    (x_ref, mod_ref, nmix_ref, wmain_ref, wba_ref, wbat_ref, qn_ref, kn_ref, ones_ref) = refs[:9]
    pos = 9
    if rope:
        cos_ref, sin_ref = refs[9:11]
        pos = 11
    (q_out, k_out, v_out, dqkv_out, ba_out, bat_out, dgate_out, sga_out, sgd_out) = refs[pos:]

    x = x_ref[...]
    mod = mod_ref[0]
    shift1 = mod[:, 0:D_MODEL]
    scale1 = mod[:, D_MODEL:2 * D_MODEL]
    ms = jnp.mean(x * x, axis=-1, keepdims=True)
    h = x * lax.rsqrt(ms + EPS) * nmix_ref[...]
    h = h * (1.0 + scale1) + shift1
    hb = h.astype(bf16)

    aq = _dot(hb, wmain_ref[:, OFF_Q:OFF_Q + ATTN_Q_W])
    ss = _group_sumsq(aq, ones_ref[...])
    aq = aq * lax.rsqrt(ss * (1.0 / HEAD_DIM) + EPS) * qn_ref[...]
    if rope:
        aq = _rope(aq, cos_ref[...], sin_ref[...])
    q_out[...] = (aq * (HEAD_DIM ** -0.5)).astype(bf16)

    ak = _dot(hb, wmain_ref[:, OFF_K:OFF_K + ATTN_KV_W])
    ss = _group_sumsq(ak, ones_ref[...])
    ak = ak * lax.rsqrt(ss * (1.0 / HEAD_DIM) + EPS) * kn_ref[...]
    av = _dot(hb, wmain_ref[:, OFF_V:OFF_V + ATTN_KV_W])
    if rope:
        ak = _rope(ak, cos_ref[:, 0:ATTN_KV_W], sin_ref[:, 0:ATTN_KV_W])
        for kv in range(ATTN_KV_HEADS):
            k_out[0, kv] = ak[:, kv * HEAD_DIM:(kv + 1) * HEAD_DIM].astype(k_out.dtype)
            v_out[0, kv] = av[:, kv * HEAD_DIM:(kv + 1) * HEAD_DIM].astype(v_out.dtype)
    else:
        tm = x.shape[0]
        seq = tm // seq_per_tile
        for kv in range(ATTN_KV_HEADS):
            k_out[:, 0, kv] = ak[:, kv * HEAD_DIM:(kv + 1) * HEAD_DIM].reshape(seq_per_tile, seq, HEAD_DIM)
            v_out[:, 0, kv] = av[:, kv * HEAD_DIM:(kv + 1) * HEAD_DIM].reshape(seq_per_tile, seq, HEAD_DIM)

    dqkv_out[...] = _dot(hb, wmain_ref[:, OFF_DQKV:OFF_DGATE]).astype(bf16)
    ba_out[...] = _dot(hb, wba_ref[...])
    bat_out[...] = lax.dot_general(wbat_ref[...], hb, _TRANS_B, preferred_element_type=f32)
    dgate_out[...] = _dot(hb, wmain_ref[:, OFF_DGATE:OFF_GATTN]).astype(bf16)
    sga_out[...] = jax.nn.sigmoid(_dot(hb, wmain_ref[:, OFF_GATTN:OFF_GDN])).astype(bf16)
    sgd_out[...] = jax.nn.sigmoid(_dot(hb, wmain_ref[:, OFF_GDN:W_MAIN])).astype(bf16)


def _proj_call(x2d, mod3, mod_row, nmix, wmain, wba, wbat, qn, kn, ones_bd, rope_tabs, n_seq, seq_len):
    t = x2d.shape[0]
    tm = 512
    rope = rope_tabs is not None
    tiles_per_seq = max(seq_len // tm, 1)
    seq_per_tile = max(tm // seq_len, 1)
    in_specs = [pl.BlockSpec((tm, D_MODEL), lambda i: (i, 0)),
                pl.BlockSpec((1, 1, N_MOD * D_MODEL), lambda i: (mod_row(i // tiles_per_seq), 0, 0)),
                _const_spec((1, D_MODEL)),
                _const_spec((D_MODEL, W_MAIN)),
                _const_spec((D_MODEL, 2 * N_GATES)),
                _const_spec((2 * N_GATES, D_MODEL)),
                _const_spec((1, ATTN_Q_W)),
                _const_spec((1, ATTN_KV_W)),
                _const_spec((LANE_TILE, LANE_TILE))]
    args = [x2d, mod3, nmix, wmain, wba, wbat, qn, kn, ones_bd]
    if rope:
        in_specs += [pl.BlockSpec((tm, ATTN_Q_W), lambda i: (i % tiles_per_seq, 0))] * 2
        args += list(rope_tabs)
        kv_shape = jax.ShapeDtypeStruct((n_seq, ATTN_KV_HEADS, seq_len, HEAD_DIM), bf16)
        kv_spec = pl.BlockSpec((1, ATTN_KV_HEADS, tm, HEAD_DIM),
                               lambda i: (i // tiles_per_seq, 0, i % tiles_per_seq, 0))
    else:
        kv_shape = jax.ShapeDtypeStruct((n_seq, 1, ATTN_KV_HEADS, seq_len, HEAD_DIM), f32)
        kv_spec = pl.BlockSpec((seq_per_tile, 1, ATTN_KV_HEADS, seq_len, HEAD_DIM), lambda i: (i, 0, 0, 0, 0))

    def row_spec(w):
        return pl.BlockSpec((tm, w), lambda i: (i, 0))

    out_shape = (jax.ShapeDtypeStruct((t, ATTN_Q_W), bf16), kv_shape, kv_shape,
                 jax.ShapeDtypeStruct((t, 3 * DN_W), bf16),
                 jax.ShapeDtypeStruct((t, 2 * N_GATES), f32),
                 jax.ShapeDtypeStruct((2 * N_GATES, t), f32),
                 jax.ShapeDtypeStruct((t, DN_W), bf16),
                 jax.ShapeDtypeStruct((t, D_MODEL), bf16),
                 jax.ShapeDtypeStruct((t, D_MODEL), bf16))
    out_specs = (row_spec(ATTN_Q_W), kv_spec, kv_spec, row_spec(3 * DN_W), row_spec(2 * N_GATES),
                 pl.BlockSpec((2 * N_GATES, tm), lambda i: (0, i)),
                 row_spec(DN_W), row_spec(D_MODEL), row_spec(D_MODEL))
    return pl.pallas_call(
        functools.partial(_proj_kernel, rope=rope, seq_per_tile=seq_per_tile),
        grid=(t // tm,),
        in_specs=in_specs,
        out_specs=out_specs,
        out_shape=out_shape,
        compiler_params=_cparams(("arbitrary",)),
        name="proj_lat" if rope else "proj_ctx",
    )(*args)


def _attn_kernel(*refs, n_sets):
    q_ref = refs[0]
    kv_refs = refs[1:1 + 2 * n_sets]
    rep_ref = refs[1 + 2 * n_sets]
    o_ref = refs[-1]
    rep = rep_ref[...]
    width = ATTN_GROUP * HEAD_DIM
    blk = lax.broadcasted_iota(jnp.int32, (1, width), 1) // HEAD_DIM

    def head_slab(ref, kv):
        x = ref[0, 0, kv] if len(ref.shape) == 5 else ref[0, kv]
        return _dot(x.astype(bf16), rep)

    for kv in range(ATTN_KV_HEADS):
        q = q_ref[0, :, kv * width:(kv + 1) * width]
        k4 = [head_slab(kv_refs[2 * s], kv) for s in range(n_sets)]
        v4 = [head_slab(kv_refs[2 * s + 1], kv) for s in range(n_sets)]
        acc = jnp.zeros((q.shape[0], width), f32)
        for g in range(ATTN_GROUP):
            sel = blk == g
            scores = [lax.dot_general(q, jnp.where(sel, k, 0.0).astype(bf16), _TRANS_B,
                                      preferred_element_type=f32) for k in k4]
            m = scores[0].max(axis=-1, keepdims=True)
            for s in scores[1:]:
                m = jnp.maximum(m, s.max(axis=-1, keepdims=True))
            probs = [jnp.exp(s - m) for s in scores]
            denom = probs[0].sum(axis=-1, keepdims=True)
            for p in probs[1:]:
                denom = denom + p.sum(axis=-1, keepdims=True)
            og = _dot(probs[0].astype(bf16), jnp.where(sel, v4[0], 0.0).astype(bf16))
            for p, v in zip(probs[1:], v4[1:]):
                og = og + _dot(p.astype(bf16), jnp.where(sel, v, 0.0).astype(bf16))
            acc = acc + og * (1.0 / denom)
        o_ref[0, :, kv * width:(kv + 1) * width] = acc.astype(o_ref.dtype)


def _attn_call(q3, kv_sets, rep, tq, name):
    b, l, _ = q3.shape
    in_specs = [pl.BlockSpec((1, tq, ATTN_Q_W), lambda i, t: (i, t, 0))]
    args = [q3]
    for arr in kv_sets:
        nd = arr.ndim
        blk = (1,) + arr.shape[1:]
        in_specs.append(pl.BlockSpec(blk, lambda i, t, nd=nd: (i,) + (0,) * (nd - 1)))
        args.append(arr)
    in_specs.append(_const_spec(rep.shape))
    args.append(rep)
    return pl.pallas_call(
        functools.partial(_attn_kernel, n_sets=len(kv_sets) // 2),
        grid=(b, l // tq),
        in_specs=in_specs,
        out_specs=pl.BlockSpec((1, tq, ATTN_Q_W), lambda i, t: (i, t, 0)),
        out_shape=jax.ShapeDtypeStruct((b, l, ATTN_Q_W), bf16),
        compiler_params=_cparams(("arbitrary", "arbitrary")),
        name=name,
    )(*args)


def _dn_kernel(*refs, seq_len, has_s0, want_state):
    it = iter(refs)
    dqkv_ref = next(it)
    ba_ref = next(it)
    bar_ref = next(it)
    barp_ref = next(it)
    dgate_ref = next(it)
    s0_ref = next(it) if has_s0 else None
    convw_ref = next(it)
    alog_c = next(it)
    dtb_c = next(it)
    alog_r = next(it)
    dtb_r = next(it)
    alog_p = next(it)
    dtb_p = next(it)
    dnn_ref = next(it)
    ones_ref = next(it)
    expand_ref = next(it)
    o_ref = next(it)
    st_ref = next(it) if want_state else None
    q_s, k_s, v_s, kt_s, gx_s, bx_s, grp_s, gr_s, of_s, ob_s, s_s, cu_s, nw_s, qe_s, el_s = it
    chunks_per_iter = 2
    pre_chunks = 2

    n = seq_len // CHUNK
    c = CHUNK
    pw = 2 * DN_DK
    n_pairs = DN_HEADS // 2
    r64 = lax.broadcasted_iota(jnp.int32, (c, c), 0)
    c64 = lax.broadcasted_iota(jnp.int32, (c, c), 1)
    tril = jnp.where(r64 >= c64, 1.0, 0.0).astype(bf16)
    triu = jnp.where(r64 <= c64, 1.0, 0.0).astype(bf16)
    rp = lax.broadcasted_iota(jnp.int32, (c, pw), 0)
    cp = lax.broadcasted_iota(jnp.int32, (c, pw), 1) % c
    lower_p = rp >= cp
    upper_p = rp <= cp
    eye_p = jnp.where(rp == cp, 1.0, 0.0)
    same8 = (rp // 8) == (cp // 8)
    off16 = ((rp // 16) == (cp // 16)) & ~same8
    off32 = ((rp // 32) == (cp // 32)) & ((rp // 16) != (cp // 16))
    off64 = (rp // 32) != (cp // 32)
    rb2 = lax.broadcasted_iota(jnp.int32, (pw, pw), 0)
    cb2 = lax.broadcasted_iota(jnp.int32, (pw, pw), 1)
    bd_mask = (rb2 // c) == (cb2 // c)
    bd_triu = jnp.where(bd_mask & (rb2 % c <= cb2 % c), 1.0, 0.0).astype(bf16)
    bd_tril = jnp.where(bd_mask & (rb2 % c >= cb2 % c), 1.0, 0.0).astype(bf16)
    lane16 = lax.broadcasted_iota(jnp.int32, (1, N_GATES), 1)
    sub16 = lax.broadcasted_iota(jnp.int32, (N_GATES, 1), 0)
    sub8 = lax.broadcasted_iota(jnp.int32, (2 * n_pairs, 1), 0)
    sub_pw = lax.broadcasted_iota(jnp.int32, (pw, 1), 0)
    row_c = lax.broadcasted_iota(jnp.int32, (c, 1), 0)

    def bdiag(x):
        xb = x.astype(bf16)
        return jnp.where(bd_mask, jnp.concatenate([xb, xb], axis=0), jnp.zeros((), bf16))

    def conv_silu(ci):
        r0 = pl.multiple_of(ci * c, c)
        xb = dqkv_ref[0, pl.ds(r0, c), :].astype(f32)
        p0 = pl.multiple_of(jnp.maximum(r0 - 16, 0), 16)
        n0 = pl.multiple_of(jnp.minimum(r0 + c, seq_len - 16), 16)
        prev = dqkv_ref[0, pl.ds(p0, 16), :].astype(f32)[15:16, :]
        nxt = dqkv_ref[0, pl.ds(n0, 16), :].astype(f32)[0:1, :]
        prev = jnp.where(ci > 0, prev, 0.0)
        nxt = jnp.where(ci < n - 1, nxt, 0.0)
        xm = jnp.where(row_c == 0, prev, pltpu.roll(xb, 1, 0))
        xp = jnp.where(row_c == c - 1, nxt, pltpu.roll(xb, c - 1, 0))
        w = convw_ref[...]
        return _silu(xm * w[0:1, :] + xb * w[1:2, :] + xp * w[2:3, :])

    def gate_col(ci):
        ba = ba_ref[0, pl.ds(pl.multiple_of(ci * c, c), c), :]
        beta = jax.nn.sigmoid(ba[:, 0:N_GATES])
        g = -jnp.exp(alog_c[...]) * _softplus(ba[:, N_GATES:2 * N_GATES] + dtb_c[...])
        return beta, _split3(g)

    def pre_body(it, carry):
        cis = [it * pre_chunks + s for s in range(pre_chunks)]
        rows = [pl.multiple_of(ci * c, c) for ci in cis]
        act = [conv_silu(ci) for ci in cis]
        qk = [a[:, 0:2 * DN_W] for a in act]
        gates = [gate_col(ci) for ci in cis]
        g_r = [_split3(-jnp.exp(alog_r[...]) * _softplus(bar_ref[0, ci][N_GATES:2 * N_GATES, :] + dtb_r[...]))
               for ci in cis]
        g_p = [_split3(-jnp.exp(alog_p[...]) * _softplus(barp_ref[0, ci] + dtb_p[...])) for ci in cis]

        sumsq = _group_sumsq_many(qk, ones_ref[...])
        fwd = [sum(_dot(tril, p) for p in gsp) for _, gsp in gates]
        bwd = [sum(_dot(triu, p) for p in gsp) for _, gsp in gates]
        fwd_r = [sum(_dot(p, triu) for p in parts) for parts in g_r]
        bwd_r = [sum(_dot(p, tril) for p in parts) for parts in g_r]
        fwd_p = [sum(_dot(p, bd_triu) for p in parts) for parts in g_p]
        bwd_p = [sum(_dot(p, bd_tril) for p in parts) for parts in g_p]
        expand = expand_ref[...]
        gcum = [jnp.where(lane16 < DN_HEADS, f, b) for f, b in zip(fwd, bwd)]
        gx = [sum(_dot(p, expand) for p in _split3(gc_)) for gc_ in gcum]
        bx = [sum(_dot(p, expand) for p in _split2(beta)) for beta, _ in gates]

        for s, ci in enumerate(cis):
            r0 = rows[s]
            a = act[s]
            scale = lax.rsqrt(sumsq[s] + EPS)
            q_s[pl.ds(r0, c), :] = a[:, 0:DN_W] * scale[:, 0:DN_W] * (DN_DK ** -0.5)
            kn = a[:, DN_W:2 * DN_W] * scale[:, DN_W:2 * DN_W]
            k_s[pl.ds(r0, c), :] = kn
            v_s[pl.ds(r0, c), :] = a[:, 2 * DN_W:3 * DN_W]
            kt_s[ci] = jnp.transpose(jnp.concatenate([kn, kn], axis=0)).astype(bf16)
            gx_s[pl.ds(r0, c), :] = gx[s]
            bx_s[pl.ds(r0, c), :] = bx[s]
            gr_s[ci] = jnp.where(sub16 < DN_HEADS, fwd_r[s], bwd_r[s])
            grp_s[ci] = jnp.where(sub8 < n_pairs, fwd_p[s], bwd_p[s])
        return carry

    lax.fori_loop(0, n // pre_chunks, pre_body, 0)

    zero_blk = jnp.zeros((DN_DK, DN_DK), f32)
    for d in range(2):
        for p in range(n_pairs):
            if has_s0:
                top = jnp.concatenate([s0_ref[0, 0, d, 2 * p], zero_blk], axis=1)
                bot = jnp.concatenate([zero_blk, s0_ref[0, 0, d, 2 * p + 1]], axis=1)
                s_s[d * n_pairs + p] = jnp.concatenate([top, bot], axis=0)
            else:
                s_s[d * n_pairs + p] = jnp.zeros((pw, pw), f32)

    def each(fn, *lists):
        return [fn(*vals) for vals in zip(*lists)]

    def prep_body(it, carry):
        qs, ks, vs, bx, eg, decay, strict, ktd, kq, where_to = [], [], [], [], [], [], [], [], [], []
        for sub in range(chunks_per_iter):
            ci = it * chunks_per_iter + sub
            r0 = pl.multiple_of(ci * c, c)
            grp = grp_s[ci]
            gr16 = gr_s[ci]
            for p in range(n_pairs):
                sl = slice(p * pw, (p + 1) * pw)
                q_p = q_s[pl.ds(r0, c), sl]
                k_p = k_s[pl.ds(r0, c), sl]
                v_p = v_s[pl.ds(r0, c), sl]
                kt = kt_s[ci, p * pw:(p + 1) * pw, :]
                kq_p = _dot(jnp.concatenate([k_p, q_p], axis=0).astype(bf16),
                            jnp.where(bd_mask, kt, jnp.zeros((), bf16)))
                for d in range(2):
                    pp = d * n_pairs + p
                    j0 = d * DN_HEADS + 2 * p
                    xl = slice(pp * pw, (pp + 1) * pw)
                    g_end = gr16[:, c - 1:c] if d == 0 else gr16[:, 0:1]
                    e_rest_t = jnp.exp(g_end - gr16)
                    incl = lower_p if d == 0 else upper_p
                    gx = gx_s[pl.ds(r0, c), xl]
                    g_last = gx[c - 1:c, :] if d == 0 else gx[0:1, :]
                    el_s[ci, pp:pp + 1, :] = jnp.exp(g_last)
                    qs.append(q_p)
                    ks.append(k_p)
                    vs.append(v_p)
                    kq.append(kq_p)
                    bx.append(bx_s[pl.ds(r0, c), xl])
                    eg.append(jnp.exp(gx))
                    diff = gx - grp[pp:pp + 1, :]
                    decay.append(jnp.where(incl, jnp.exp(jnp.where(incl, diff, 0.0)), 0.0))
                    strict.append((rp > cp) if d == 0 else (rp < cp))
                    fac = jnp.where(sub_pw < c, e_rest_t[j0:j0 + 1, :], e_rest_t[j0 + 1:j0 + 2, :])
                    ktd.append((kt[:, 0:c].astype(f32) * fac).astype(bf16))
                    where_to.append((ci, pp, r0, sl, d))

        lmat = each(lambda m, x, b, dc: jnp.where(m, x[0:c] * b * dc, 0.0), strict, kq, bx, decay)
        amat = each(lambda x, dc: (x[c:2 * c] * dc).astype(bf16), kq, decay)
        d1 = each(lambda l: jnp.where(same8, l, 0.0), lmat)
        d2 = each(lambda a: _dot(a.astype(bf16), bdiag(a)), d1)
        tmat = each(lambda a: eye_p - a, d1)
        tmat = each(lambda t, a: t + _dot(a.astype(bf16), bdiag(t)), tmat, d2)
        d4 = each(lambda a: _dot(a.astype(bf16), bdiag(a)).astype(bf16), d2)
        tmat = each(lambda t, a: t + _dot(a, bdiag(t)), tmat, d4)
        for off_mask in (off16, off32, off64):
            tbd = each(bdiag, tmat)
            tl = each(lambda t, l: _dot(t.astype(bf16), bdiag(jnp.where(off_mask, l, 0.0))).astype(bf16), tmat, lmat)
            tmat = each(lambda t, a, b: t - _dot(a, b), tmat, tl, tbd)
        uw = each(lambda t, v, k, b, e: _dot(t.astype(bf16), jnp.concatenate([bdiag(v * b), bdiag(k * (b * e))], axis=1)),
                  tmat, vs, ks, bx, eg)
        ku_kw = each(lambda kd, y: _dot(kd, y.astype(bf16)), ktd, uw)
        au_aw = each(lambda a, y: _dot(a, jnp.concatenate([bdiag(y[:, 0:pw]), bdiag(y[:, pw:2 * pw])], axis=1)),
                     amat, uw)
        for (ci, pp, r0, sl, d), q_p, e, kk, aa in zip(where_to, qs, eg, ku_kw, au_aw):
            cu_s[ci, pp] = jnp.where(bd_mask, kk[:, 0:pw], 0.0)
            nw_s[ci, pp] = jnp.where(bd_mask, -kk[:, pw:2 * pw], 0.0).astype(bf16)
            qe_s[ci, pp] = (q_p * e - aa[:, pw:2 * pw]).astype(bf16)
            dst = of_s if d == 0 else ob_s
            dst[pl.ds(r0, c), sl] = aa[:, 0:pw]
        return carry

    lax.fori_loop(0, n // chunks_per_iter, prep_body, 0)

    def scan_body(i, carry):
        cis = [i, n - 1 - i]
        pairs = [(d, p) for d in range(2) for p in range(n_pairs)]
        s_old = [s_s[d * n_pairs + p] for d, p in pairs]
        s_b = [s.astype(bf16) for s in s_old]
        o_add = [_dot(qe_s[cis[d], d * n_pairs + p], sb) for (d, p), sb in zip(pairs, s_b)]
        s_upd = [_dot(nw_s[cis[d], d * n_pairs + p], sb) for (d, p), sb in zip(pairs, s_b)]
        for (d, p), s, oa, su in zip(pairs, s_old, o_add, s_upd):
            pp = d * n_pairs + p
            r0 = pl.multiple_of(cis[d] * c, c)
            dst = of_s if d == 0 else ob_s
            dst[pl.ds(r0, c), p * pw:(p + 1) * pw] += oa
            s_s[pp] = s * el_s[cis[d], pp:pp + 1, :] + su + cu_s[cis[d], pp]
        return carry

    lax.fori_loop(0, n, scan_body, 0)

    rb = 128

    def post_body(bi, carry):
        r0 = pl.multiple_of(bi * rb, rb)
        o = of_s[pl.ds(r0, rb), :] + ob_s[pl.ds(r0, rb), :]
        o = o * lax.rsqrt(_group_sumsq(o, ones_ref[...]) * (1.0 / DN_DK) + EPS) * dnn_ref[...]
        gate = dgate_ref[0, pl.ds(r0, rb), :].astype(f32)
        o_ref[0, pl.ds(r0, rb), :] = (o * _silu(gate)).astype(o_ref.dtype)
        return carry

    lax.fori_loop(0, seq_len // rb, post_body, 0)

    if want_state:
        for d in range(2):
            for p in range(n_pairs):
                s = s_s[d * n_pairs + p]
                st_ref[0, 0, d, 2 * p] = s[0:DN_DK, 0:DN_DK]
                st_ref[0, 0, d, 2 * p + 1] = s[DN_DK:pw, DN_DK:pw]


def _dn_call(dqkv3, ba3, bar4, barp4, dgate3, s0, consts, want_state, name):
    b, l, _ = dqkv3.shape
    n = l // CHUNK
    n_pairs = DN_HEADS // 2
    pw = 2 * DN_DK
    state_blk = (1, 1, 2, DN_HEADS, DN_DK, DN_DK)
    state_spec = pl.BlockSpec(state_blk, lambda i: (i, 0, 0, 0, 0, 0))
    in_specs = [pl.BlockSpec((1, l, 3 * DN_W), lambda i: (i, 0, 0)),
                pl.BlockSpec((1, l, 2 * N_GATES), lambda i: (i, 0, 0)),
                pl.BlockSpec((1, n, 2 * N_GATES, CHUNK), lambda i: (i, 0, 0, 0)),
                pl.BlockSpec((1, n, 2 * n_pairs, pw), lambda i: (i, 0, 0, 0)),
                pl.BlockSpec((1, l, DN_W), lambda i: (i, 0, 0))]
    args = [dqkv3, ba3, bar4, barp4, dgate3]
    if s0 is not None:
        in_specs.append(state_spec)
        args.append(s0)
    in_specs += [_const_spec(a.shape) for a in consts]
    args += consts
    out_shape = [jax.ShapeDtypeStruct((b, l, DN_W), bf16)]
    out_specs = [pl.BlockSpec((1, l, DN_W), lambda i: (i, 0, 0))]
    if want_state:
        out_shape.append(jax.ShapeDtypeStruct((b,) + state_blk[1:], f32))
        out_specs.append(state_spec)
    scratch = [pltpu.VMEM((l, DN_W), f32), pltpu.VMEM((l, DN_W), f32), pltpu.VMEM((l, DN_W), f32),
               pltpu.VMEM((n, DN_W, pw), bf16),
               pltpu.VMEM((l, N_GATES * DN_DK), f32), pltpu.VMEM((l, N_GATES * DN_DK), f32),
               pltpu.VMEM((n, 2 * n_pairs, pw), f32), pltpu.VMEM((n, N_GATES, CHUNK), f32),
               pltpu.VMEM((l, DN_W), f32), pltpu.VMEM((l, DN_W), f32),
               pltpu.VMEM((2 * n_pairs, pw, pw), f32),
               pltpu.VMEM((n, 2 * n_pairs, pw, pw), f32),
               pltpu.VMEM((n, 2 * n_pairs, pw, pw), bf16),
               pltpu.VMEM((n, 2 * n_pairs, CHUNK, pw), bf16),
               pltpu.VMEM((n, 2 * n_pairs, pw), f32)]
    res = pl.pallas_call(
        functools.partial(_dn_kernel, seq_len=l, has_s0=s0 is not None, want_state=want_state),
        grid=(b,),
        in_specs=in_specs,
        out_specs=tuple(out_specs),
        out_shape=tuple(out_shape),
        scratch_shapes=scratch,
        compiler_params=_cparams(("arbitrary",)),
        name=name,
    )(*args)
    return res if want_state else (res[0], None)


def _merge_kernel(oa_ref, od_ref, sga_ref, sgd_ref, x_ref, mod_ref, wa_ref, wd_ref, wo_ref, nffn_ref,
                  wr_ref, br_ref, x1_out, h2_out, comb_out):
    mod = mod_ref[0]
    gate1 = mod[:, 2 * D_MODEL:3 * D_MODEL]
    shift2 = mod[:, 3 * D_MODEL:4 * D_MODEL]
    scale2 = mod[:, 4 * D_MODEL:5 * D_MODEL]
    merged = (sga_ref[...].astype(f32) * _dot(oa_ref[...], wa_ref[...])
              + sgd_ref[...].astype(f32) * _dot(od_ref[...], wd_ref[...]))
    m = _dot(merged.astype(bf16), wo_ref[...])
    x1 = x_ref[...] + gate1 * m
    x1_out[...] = x1
    ms = jnp.mean(x1 * x1, axis=-1, keepdims=True)
    h2 = x1 * lax.rsqrt(ms + EPS) * nffn_ref[...]
    h2 = h2 * (1.0 + scale2) + shift2
    h2_out[...] = h2.astype(bf16)

    h_hi, h_lo = _split2(h2)
    w_hi, w_lo = _split2(wr_ref[...])
    logits = _dot(h_hi, w_hi) + _dot(h_hi, w_lo) + _dot(h_lo, w_hi) + br_ref[...]
    lane = lax.broadcasted_iota(jnp.int32, logits.shape, 1)
    big = ROUTER_LANES
    neg = -jnp.inf

    def first_lane(hit):
        return jnp.min(jnp.where(hit, lane, big), axis=-1, keepdims=True)

    is_g = lane < N_GROUPS
    gl = jnp.where(is_g, logits, neg)
    gexp = jnp.where(is_g, jnp.exp(gl - gl.max(axis=-1, keepdims=True)), 0.0)
    gp = gexp / gexp.sum(axis=-1, keepdims=True)
    g_top = gp.max(axis=-1, keepdims=True)
    g_idx = first_lane(is_g & (gp == g_top))
    lo_lane = N_GROUPS + g_idx * EXPERTS_PER_GROUP
    is_e = (lane >= lo_lane) & (lane < lo_lane + EXPERTS_PER_GROUP)
    el = jnp.where(is_e, logits, neg)
    eexp = jnp.where(is_e, jnp.exp(el - el.max(axis=-1, keepdims=True)), 0.0)
    ep = jnp.where(is_e, eexp / eexp.sum(axis=-1, keepdims=True), -1.0)
    p1 = ep.max(axis=-1, keepdims=True)
    i1 = first_lane(ep == p1)
    ep2 = jnp.where(lane == i1, -1.0, ep)
    p2 = ep2.max(axis=-1, keepdims=True)
    i2 = first_lane(ep2 == p2)
    tot = p1 + p2
    comb_out[...] = (jnp.where(lane == i1, p1 / tot * g_top, 0.0)
                     + jnp.where(lane == i2, p2 / tot * g_top, 0.0))


def _merge_call(oa, od, sga, sgd, x2d, mod3, mod_row, wa, wd, wo, nffn, wr, br, seq_len, name):
    t = x2d.shape[0]
    tm = 512
    tiles_per_seq = max(seq_len // tm, 1)

    def row_spec(w):
        return pl.BlockSpec((tm, w), lambda i: (i, 0))

    return pl.pallas_call(
        _merge_kernel,
        grid=(t // tm,),
        in_specs=[row_spec(ATTN_Q_W), row_spec(DN_W), row_spec(D_MODEL), row_spec(D_MODEL), row_spec(D_MODEL),
                  pl.BlockSpec((1, 1, N_MOD * D_MODEL), lambda i: (mod_row(i // tiles_per_seq), 0, 0)),
                  _const_spec(wa.shape), _const_spec(wd.shape), _const_spec(wo.shape),
                  _const_spec((1, D_MODEL)), _const_spec(wr.shape), _const_spec(br.shape)],
        out_specs=(row_spec(D_MODEL), row_spec(D_MODEL), row_spec(ROUTER_LANES)),
        out_shape=(jax.ShapeDtypeStruct((t, D_MODEL), f32),
                   jax.ShapeDtypeStruct((t, D_MODEL), bf16),
                   jax.ShapeDtypeStruct((t, ROUTER_LANES), f32)),
        compiler_params=_cparams(("arbitrary",)),
        name=name,
    )(oa, od, sga, sgd, x2d, mod3, wa, wd, wo, nffn, wr, br)


def _moe_kernel(h2_ref, comb_ref, x1_ref, mod_ref, wgu_ref, wdn_ref, nfin_ref, y_ref, acc_ref):
    e = pl.program_id(1)

    @pl.when(e == 0)
    def _():
        acc_ref[...] = jnp.zeros_like(acc_ref)

    gu = _dot(h2_ref[...], wgu_ref[0])
    hid = _silu(gu[:, 0:D_FF_EXPERT]) * gu[:, D_FF_EXPERT:2 * D_FF_EXPERT]
    comb = comb_ref[...]
    lane = lax.broadcasted_iota(jnp.int32, comb.shape, 1)
    ce = jnp.sum(jnp.where(lane == e + N_GROUPS, comb, 0.0), axis=-1, keepdims=True)
    acc_ref[...] += _dot((hid * ce).astype(bf16), wdn_ref[0])

    @pl.when(e == N_EXPERTS - 1)
    def _():
        gate2 = mod_ref[0][:, 5 * D_MODEL:6 * D_MODEL]
        x2 = x1_ref[...] + gate2 * acc_ref[...]
        ms = jnp.mean(x2 * x2, axis=-1, keepdims=True)
        y_ref[...] = x2 * lax.rsqrt(ms + EPS) * nfin_ref[...]


def _moe_call(h2, comb, x1, mod3, mod_row, wgu, wdn, nfin, seq_len, name):
    t = h2.shape[0]
    tm = 1024
    tiles_per_seq = max(seq_len // tm, 1)
    return pl.pallas_call(
        _moe_kernel,
        grid=(t // tm, N_EXPERTS),
        in_specs=[pl.BlockSpec((tm, D_MODEL), lambda i, e: (i, 0)),
                  pl.BlockSpec((tm, ROUTER_LANES), lambda i, e: (i, 0)),
                  pl.BlockSpec((tm, D_MODEL), lambda i, e: (i, 0)),
                  pl.BlockSpec((1, 1, N_MOD * D_MODEL), lambda i, e: (mod_row(i // tiles_per_seq), 0, 0)),
                  pl.BlockSpec((1, D_MODEL, 2 * D_FF_EXPERT), lambda i, e: (e, 0, 0)),
                  pl.BlockSpec((1, D_FF_EXPERT, D_MODEL), lambda i, e: (e, 0, 0)),
                  pl.BlockSpec((1, D_MODEL), lambda i, e: (0, 0))],
        out_specs=pl.BlockSpec((tm, D_MODEL), lambda i, e: (i, 0)),
        out_shape=jax.ShapeDtypeStruct((t, D_MODEL), f32),
        scratch_shapes=[pltpu.VMEM((tm, D_MODEL), f32)],
        compiler_params=_cparams(("arbitrary", "arbitrary")),
        name=name,
    )(h2, comb, x1, mod3, wgu, wdn, nfin)


def _rope_tables(seq_len):
    t = np.arange(seq_len)
    row = (t // GRID_W).astype(np.float64)
    col = (t % GRID_W).astype(np.float64)
    half = HEAD_DIM // 2
    inv = np.power(ROPE_THETA, -np.arange(0, half, 2, dtype=np.float64) / half)
    d = np.arange(HEAD_DIM)
    freq = inv[d % (half // 2)]
    pos = np.where(d[None, :] < half, row[:, None], col[:, None])
    ang = pos * freq[None, :]
    sign = np.where((d % half) < half // 2, -1.0, 1.0)
    cos = np.tile(np.cos(ang), (1, ATTN_HEADS)).astype(np.float32)
    sin = np.tile(np.sin(ang) * sign[None, :], (1, ATTN_HEADS)).astype(np.float32)
    return jnp.asarray(cos), jnp.asarray(sin)


def _ones_block_diag():
    i = np.arange(LANE_TILE)
    return jnp.asarray((i[:, None] // HEAD_DIM == i[None, :] // HEAD_DIM).astype(np.float32), dtype=bf16)


def _lane_replicator():
    i = np.arange(HEAD_DIM)
    j = np.arange(ATTN_GROUP * HEAD_DIM)
    return jnp.asarray((i[:, None] == j[None, :] % HEAD_DIM).astype(np.float32), dtype=bf16)


def _trunk(x3, mod3, mod_row, wts, prefix):
    n_seq, seq_len, _ = x3.shape
    t = n_seq * seq_len
    x2d = x3.reshape(t, D_MODEL)
    is_lat = prefix is not None
    rope_tabs = _rope_tables(seq_len) if is_lat else None

    q, k, v, dqkv, ba, bat, dgate, sga, sgd = _proj_call(
        x2d, mod3, mod_row, wts["nmix"], wts["wmain"], wts["wba"], wts["wbat"], wts["qn"], wts["kn"],
        wts["ones_bd"], rope_tabs, n_seq, seq_len)

    q3 = q.reshape(n_seq, seq_len, ATTN_Q_W)
    if is_lat:
        pk, pv, s0 = prefix
        o_attn = _attn_call(q3, [pk, pv, k, v], wts["rep"], 256, "attn_lat")
    else:
        s0 = None
        o_attn = _attn_call(q3, [k, v], wts["rep"], seq_len, "attn_ctx")

    n_chunks = seq_len // CHUNK
    bar4 = bat.reshape(2 * N_GATES, n_seq, n_chunks, CHUNK).transpose(1, 2, 0, 3)
    barp4 = bat[N_GATES:].reshape(N_GATES // 2, 2, n_seq, n_chunks, CHUNK).transpose(2, 3, 0, 1, 4).reshape(
        n_seq, n_chunks, N_GATES // 2, 2 * CHUNK)
    dn_consts = [wts[name] for name in ("convw", "alog_c", "dtb_c", "alog_r", "dtb_r", "alog_p", "dtb_p", "dnn",
                                        "ones_bd", "expand")]
    o_dn, state = _dn_call(
        dqkv.reshape(n_seq, seq_len, 3 * DN_W), ba.reshape(n_seq, seq_len, 2 * N_GATES), bar4, barp4,
        dgate.reshape(n_seq, seq_len, DN_W), s0, dn_consts, not is_lat, "dn_lat" if is_lat else "dn_ctx")

    x1, h2, comb = _merge_call(
        o_attn.reshape(t, ATTN_Q_W), o_dn.reshape(t, DN_W), sga, sgd, x2d, mod3, mod_row,
        wts["wa"], wts["wd"], wts["wo"], wts["nffn"], wts["wr"], wts["br"], seq_len,
        "merge_lat" if is_lat else "merge_ctx")

    y = _moe_call(h2, comb, x1, mod3, mod_row, wts["wgu"], wts["wdn"], wts["nfin"], seq_len,
                  "moe_lat" if is_lat else "moe_ctx")
    return y.reshape(n_seq, seq_len, D_MODEL), k, v, state


def kernel(x_prompt, x_sample, c, cache_attn_k, cache_attn_v, state_delta, c_ctx, w_mod, b_mod, norm_mix, norm_ffn, norm_final, w_in, q_norm, k_norm, conv_w, a_log, dt_bias, dn_norm, w_attn_br, w_dn_br, w_out, w_rg, b_rg, w_re, b_re, w_gate_e, w_up_e, w_down_e):
    layer = 0
    n_lat = x_sample.shape[0]
    w_in_l = w_in[layer]
    ba_lo = OFF_DGATE
    wts = {
        "nmix": norm_mix[layer][None, :],
        "wmain": jnp.concatenate([w_in_l[:, :ba_lo], w_in_l[:, ba_lo + 2 * N_GATES:]], axis=1).astype(bf16),
        "wba": w_in_l[:, ba_lo:ba_lo + 2 * N_GATES].astype(bf16),
        "wbat": w_in_l[:, ba_lo:ba_lo + 2 * N_GATES].T.astype(bf16),
        "qn": jnp.tile(q_norm[layer], ATTN_HEADS)[None, :],
        "kn": jnp.tile(k_norm[layer], ATTN_KV_HEADS)[None, :],
        "ones_bd": _ones_block_diag(),
        "rep": _lane_replicator(),
        "convw": conv_w[layer],
        "alog_c": a_log[layer].reshape(1, N_GATES),
        "dtb_c": dt_bias[layer].reshape(1, N_GATES),
        "alog_r": a_log[layer].reshape(N_GATES, 1),
        "dtb_r": dt_bias[layer].reshape(N_GATES, 1),
        "alog_p": jnp.repeat(a_log[layer].reshape(N_GATES // 2, 2), CHUNK, axis=1),
        "dtb_p": jnp.repeat(dt_bias[layer].reshape(N_GATES // 2, 2), CHUNK, axis=1),
        "expand": jnp.asarray(np.repeat(np.eye(N_GATES, dtype=np.float32), DN_DK, axis=1), dtype=bf16),
        "dnn": jnp.tile(dn_norm[layer], DN_HEADS)[None, :],
        "wa": w_attn_br[layer].astype(bf16),
        "wd": w_dn_br[layer].astype(bf16),
        "wo": w_out[layer].astype(bf16),
        "nffn": norm_ffn[layer][None, :],
        "wr": jnp.concatenate([w_rg[layer], w_re[layer],
                               jnp.zeros((D_MODEL, ROUTER_LANES - N_GROUPS - N_EXPERTS), f32)], axis=1),
        "br": jnp.concatenate([b_rg[layer], b_re[layer],
                               jnp.zeros((ROUTER_LANES - N_GROUPS - N_EXPERTS,), f32)])[None, :],
        "wgu": jnp.concatenate([w_gate_e[layer], w_up_e[layer]], axis=-1).astype(bf16),
        "wdn": w_down_e[layer].astype(bf16),
        "nfin": norm_final[None, :],
    }
    cond8 = jnp.concatenate([c_ctx[None, :], c, jnp.zeros((8 - 1 - n_lat, D_MODEL), f32)], axis=0)
    mod3 = _mod_call(cond8, w_mod[layer], b_mod[layer][None, :])[:, None, :]

    y_prompt, new_k, new_v, new_state = _trunk(x_prompt, mod3, lambda s: 0, wts, None)
    y_sample, _, _, _ = _trunk(x_sample, mod3, lambda s: s + 1, wts,
                               (cache_attn_k, cache_attn_v, state_delta))
    return (y_prompt, y_sample, new_k, new_v, new_state)
```

```python
import functools

import numpy as np
import jax
import jax.numpy as jnp
from jax import lax
from jax.experimental import pallas as pl
from jax.experimental.pallas import tpu as pltpu

f32 = jnp.float32
bf16 = jnp.bfloat16

D_MODEL = 1024
HEAD_DIM = 64
ATTN_HEADS = 8
ATTN_KV_HEADS = 2
ATTN_GROUP = ATTN_HEADS // ATTN_KV_HEADS
GRID_W = 64
ROPE_THETA = 10000.0
DN_HEADS = 8
DN_DK = 64
CHUNK = 64
N_GROUPS = 4
EXPERTS_PER_GROUP = 4
N_EXPERTS = 16
D_FF_EXPERT = 256
N_MOD = 6
EPS = 1e-6

ATTN_Q_W = ATTN_HEADS * HEAD_DIM
ATTN_KV_W = ATTN_KV_HEADS * HEAD_DIM
DN_W = DN_HEADS * DN_DK
N_GATES = 2 * DN_HEADS

OFF_Q = 0
OFF_K = ATTN_Q_W
OFF_V = OFF_K + ATTN_KV_W
OFF_DQKV = OFF_V + ATTN_KV_W
OFF_DGATE = OFF_DQKV + 3 * DN_W
OFF_GATTN = OFF_DGATE + DN_W
OFF_GDN = OFF_GATTN + D_MODEL
W_MAIN = OFF_GDN + D_MODEL

ROUTER_LANES = 128
VMEM_LIMIT = 56 * 1024 * 1024

_TRANS_B = (((1,), (1,)), ((), ()))
_TRANS_A = (((0,), (0,)), ((), ()))


def _cparams(sem):
    return pltpu.CompilerParams(dimension_semantics=sem, vmem_limit_bytes=VMEM_LIMIT)


def _dot(a, b):
    return jnp.dot(a, b, preferred_element_type=f32)


def _silu(x):
    return x * jax.nn.sigmoid(x)


def _softplus(x):
    return jnp.maximum(x, 0.0) + jnp.log1p(jnp.exp(-jnp.abs(x)))


def _split2(x):
    hi = x.astype(bf16)
    lo = (x - hi.astype(f32)).astype(bf16)
    return hi, lo


def _split3(x):
    a = x.astype(bf16)
    r = x - a.astype(f32)
    b = r.astype(bf16)
    c = (r - b.astype(f32)).astype(bf16)
    return a, b, c


LANE_TILE = 128


def _group_sumsq_many(arrs, ones_pair):
    pieces = []
    for x in arrs:
        hi, lo = _split2(x * x)
        for s in range(x.shape[1] // LANE_TILE):
            pieces.append(hi[:, s * LANE_TILE:(s + 1) * LANE_TILE])
            pieces.append(lo[:, s * LANE_TILE:(s + 1) * LANE_TILE])
    res = _dot(jnp.concatenate(pieces, axis=0), ones_pair)
    out, off = [], 0
    for x in arrs:
        r = x.shape[0]
        slabs = []
        for s in range(x.shape[1] // LANE_TILE):
            slabs.append(res[off:off + r] + res[off + r:off + 2 * r])
            off += 2 * r
        out.append(slabs[0] if len(slabs) == 1 else jnp.concatenate(slabs, axis=1))
    return out


def _group_sumsq(x, ones_pair):
    return _group_sumsq_many([x], ones_pair)[0]


def _const_spec(shape):
    nd = len(shape)
    return pl.BlockSpec(shape, lambda *_: (0,) * nd)


def _mod_kernel(c_ref, w_ref, b_ref, o_ref):
    c = c_ref[...]
    o_ref[...] = _dot(_silu(c).astype(bf16), w_ref[...].astype(bf16)) + b_ref[...]


def _mod_call(cond8, w_mod, b_mod):
    tn = 1536
    n = w_mod.shape[1]
    return pl.pallas_call(
        _mod_kernel,
        grid=(n // tn,),
        in_specs=[_const_spec((8, D_MODEL)),
                  pl.BlockSpec((D_MODEL, tn), lambda j: (0, j)),
                  pl.BlockSpec((1, tn), lambda j: (0, j))],
        out_specs=pl.BlockSpec((8, tn), lambda j: (0, j)),
        out_shape=jax.ShapeDtypeStruct((8, n), f32),
        compiler_params=_cparams(("arbitrary",)),
        name="mod",
    )(cond8, w_mod, b_mod)


def _rope(x, cos, sin):
    w = x.shape[-1]
    lane = lax.broadcasted_iota(jnp.int32, x.shape, 1)
    first = (lane % 32) < 16
    swapped = jnp.where(first, pltpu.roll(x, w - 16, 1), pltpu.roll(x, 16, 1))
    return x * cos + swapped * sin


def _proj_kernel(*refs, rope, seq_per_tile):
    (x_ref, mod_ref, nmix_ref, wmain_ref, wba_ref, wbat_ref, qn_ref, kn_ref, ones_ref) = refs[:9]
    pos = 9
    if rope:
        cos_ref, sin_ref = refs[9:11]
        pos = 11
    (q_out, k_out, v_out, dqkv_out, ba_out, bat_out, dgate_out, sga_out, sgd_out) = refs[pos:]

    x = x_ref[...]
    mod = mod_ref[0]
    shift1 = mod[:, 0:D_MODEL]
    scale1 = mod[:, D_MODEL:2 * D_MODEL]
    ms = jnp.mean(x * x, axis=-1, keepdims=True)
    h = x * lax.rsqrt(ms + EPS) * nmix_ref[...]
    h = h * (1.0 + scale1) + shift1
    hb = h.astype(bf16)

    aq = _dot(hb, wmain_ref[:, OFF_Q:OFF_Q + ATTN_Q_W])
    ss = _group_sumsq(aq, ones_ref[...])
    aq = aq * lax.rsqrt(ss * (1.0 / HEAD_DIM) + EPS) * qn_ref[...]
    if rope:
        aq = _rope(aq, cos_ref[...], sin_ref[...])
    q_out[...] = (aq * (HEAD_DIM ** -0.5)).astype(bf16)

    ak = _dot(hb, wmain_ref[:, OFF_K:OFF_K + ATTN_KV_W])
    ss = _group_sumsq(ak, ones_ref[...])
    ak = ak * lax.rsqrt(ss * (1.0 / HEAD_DIM) + EPS) * kn_ref[...]
    av = _dot(hb, wmain_ref[:, OFF_V:OFF_V + ATTN_KV_W])
    if rope:
        ak = _rope(ak, cos_ref[:, 0:ATTN_KV_W], sin_ref[:, 0:ATTN_KV_W])
        for kv in range(ATTN_KV_HEADS):
            k_out[0, kv] = ak[:, kv * HEAD_DIM:(kv + 1) * HEAD_DIM].astype(k_out.dtype)
            v_out[0, kv] = av[:, kv * HEAD_DIM:(kv + 1) * HEAD_DIM].astype(v_out.dtype)
    else:
        tm = x.shape[0]
        seq = tm // seq_per_tile
        for kv in range(ATTN_KV_HEADS):
            k_out[:, 0, kv] = ak[:, kv * HEAD_DIM:(kv + 1) * HEAD_DIM].reshape(seq_per_tile, seq, HEAD_DIM)
            v_out[:, 0, kv] = av[:, kv * HEAD_DIM:(kv + 1) * HEAD_DIM].reshape(seq_per_tile, seq, HEAD_DIM)

    dqkv_out[...] = _dot(hb, wmain_ref[:, OFF_DQKV:OFF_DGATE]).astype(bf16)
    ba_out[...] = _dot(hb, wba_ref[...])
    bat_out[...] = lax.dot_general(wbat_ref[...], hb, _TRANS_B, preferred_element_type=f32)
    dgate_out[...] = _dot(hb, wmain_ref[:, OFF_DGATE:OFF_GATTN]).astype(bf16)
    sga_out[...] = jax.nn.sigmoid(_dot(hb, wmain_ref[:, OFF_GATTN:OFF_GDN])).astype(bf16)
    sgd_out[...] = jax.nn.sigmoid(_dot(hb, wmain_ref[:, OFF_GDN:W_MAIN])).astype(bf16)


def _proj_call(x2d, mod3, mod_row, nmix, wmain, wba, wbat, qn, kn, ones_bd, rope_tabs, n_seq, seq_len):
    t = x2d.shape[0]
    tm = 512
    rope = rope_tabs is not None
    tiles_per_seq = max(seq_len // tm, 1)
    seq_per_tile = max(tm // seq_len, 1)
    in_specs = [pl.BlockSpec((tm, D_MODEL), lambda i: (i, 0)),
                pl.BlockSpec((1, 1, N_MOD * D_MODEL), lambda i: (mod_row(i // tiles_per_seq), 0, 0)),
                _const_spec((1, D_MODEL)),
                _const_spec((D_MODEL, W_MAIN)),
                _const_spec((D_MODEL, 2 * N_GATES)),
                _const_spec((2 * N_GATES, D_MODEL)),
                _const_spec((1, ATTN_Q_W)),
                _const_spec((1, ATTN_KV_W)),
                _const_spec((LANE_TILE, LANE_TILE))]
    args = [x2d, mod3, nmix, wmain, wba, wbat, qn, kn, ones_bd]
    if rope:
        in_specs += [pl.BlockSpec((tm, ATTN_Q_W), lambda i: (i % tiles_per_seq, 0))] * 2
        args += list(rope_tabs)
        kv_shape = jax.ShapeDtypeStruct((n_seq, ATTN_KV_HEADS, seq_len, HEAD_DIM), bf16)
        kv_spec = pl.BlockSpec((1, ATTN_KV_HEADS, tm, HEAD_DIM),
                               lambda i: (i // tiles_per_seq, 0, i % tiles_per_seq, 0))
    else:
        kv_shape = jax.ShapeDtypeStruct((n_seq, 1, ATTN_KV_HEADS, seq_len, HEAD_DIM), f32)
        kv_spec = pl.BlockSpec((seq_per_tile, 1, ATTN_KV_HEADS, seq_len, HEAD_DIM), lambda i: (i, 0, 0, 0, 0))

    def row_spec(w):
        return pl.BlockSpec((tm, w), lambda i: (i, 0))

    out_shape = (jax.ShapeDtypeStruct((t, ATTN_Q_W), bf16), kv_shape, kv_shape,
                 jax.ShapeDtypeStruct((t, 3 * DN_W), bf16),
                 jax.ShapeDtypeStruct((t, 2 * N_GATES), f32),
                 jax.ShapeDtypeStruct((2 * N_GATES, t), f32),
                 jax.ShapeDtypeStruct((t, DN_W), bf16),
                 jax.ShapeDtypeStruct((t, D_MODEL), bf16),
                 jax.ShapeDtypeStruct((t, D_MODEL), bf16))
    out_specs = (row_spec(ATTN_Q_W), kv_spec, kv_spec, row_spec(3 * DN_W), row_spec(2 * N_GATES),
                 pl.BlockSpec((2 * N_GATES, tm), lambda i: (0, i)),
                 row_spec(DN_W), row_spec(D_MODEL), row_spec(D_MODEL))
    return pl.pallas_call(
        functools.partial(_proj_kernel, rope=rope, seq_per_tile=seq_per_tile),
        grid=(t // tm,),
        in_specs=in_specs,
        out_specs=out_specs,
        out_shape=out_shape,
        compiler_params=_cparams(("arbitrary",)),
        name="proj_lat" if rope else "proj_ctx",
    )(*args)


def _attn_kernel(*refs, n_sets):
    q_ref = refs[0]
    kv_refs = refs[1:1 + 2 * n_sets]
    rep_ref = refs[1 + 2 * n_sets]
    o_ref = refs[-1]
    rep = rep_ref[...]
    width = ATTN_GROUP * HEAD_DIM
    blk = lax.broadcasted_iota(jnp.int32, (1, width), 1) // HEAD_DIM

    def head_slab(ref, kv):
        x = ref[0, 0, kv] if len(ref.shape) == 5 else ref[0, kv]
        return _dot(x.astype(bf16), rep)

    for kv in range(ATTN_KV_HEADS):
        q = q_ref[0, :, kv * width:(kv + 1) * width]
        k4 = [head_slab(kv_refs[2 * s], kv) for s in range(n_sets)]
        v4 = [head_slab(kv_refs[2 * s + 1], kv) for s in range(n_sets)]
        acc = jnp.zeros((q.shape[0], width), f32)
        for g in range(ATTN_GROUP):
            sel = blk == g
            scores = [lax.dot_general(q, jnp.where(sel, k, 0.0).astype(bf16), _TRANS_B,
                                      preferred_element_type=f32) for k in k4]
            m = scores[0].max(axis=-1, keepdims=True)
            for s in scores[1:]:
                m = jnp.maximum(m, s.max(axis=-1, keepdims=True))
            probs = [jnp.exp(s - m) for s in scores]
            denom = probs[0].sum(axis=-1, keepdims=True)
            for p in probs[1:]:
                denom = denom + p.sum(axis=-1, keepdims=True)
            og = _dot(probs[0].astype(bf16), jnp.where(sel, v4[0], 0.0).astype(bf16))
            for p, v in zip(probs[1:], v4[1:]):
                og = og + _dot(p.astype(bf16), jnp.where(sel, v, 0.0).astype(bf16))
            acc = acc + og * (1.0 / denom)
        o_ref[0, :, kv * width:(kv + 1) * width] = acc.astype(o_ref.dtype)


def _attn_call(q3, kv_sets, rep, tq, name):
    b, l, _ = q3.shape
    in_specs = [pl.BlockSpec((1, tq, ATTN_Q_W), lambda i, t: (i, t, 0))]
    args = [q3]
    for arr in kv_sets:
        nd = arr.ndim
        blk = (1,) + arr.shape[1:]
        in_specs.append(pl.BlockSpec(blk, lambda i, t, nd=nd: (i,) + (0,) * (nd - 1)))
        args.append(arr)
    in_specs.append(_const_spec(rep.shape))
    args.append(rep)
    return pl.pallas_call(
        functools.partial(_attn_kernel, n_sets=len(kv_sets) // 2),
        grid=(b, l // tq),
        in_specs=in_specs,
        out_specs=pl.BlockSpec((1, tq, ATTN_Q_W), lambda i, t: (i, t, 0)),
        out_shape=jax.ShapeDtypeStruct((b, l, ATTN_Q_W), bf16),
        compiler_params=_cparams(("arbitrary", "arbitrary")),
        name=name,
    )(*args)


def _dn_kernel(*refs, seq_len, has_s0, want_state):
    it = iter(refs)
    dqkv_ref = next(it)
    ba_ref = next(it)
    bar_ref = next(it)
    barp_ref = next(it)
    dgate_ref = next(it)
    s0_ref = next(it) if has_s0 else None
    convw_ref = next(it)
    alog_c = next(it)
    dtb_c = next(it)
    alog_r = next(it)
    dtb_r = next(it)
    alog_p = next(it)
    dtb_p = next(it)
    dnn_ref = next(it)
    ones_ref = next(it)
    expand_ref = next(it)
    o_ref = next(it)
    st_ref = next(it) if want_state else None
    q_s, k_s, v_s, kt_s, gx_s, bx_s, grp_s, gr_s, of_s, ob_s, s_s, cu_s, nw_s, qe_s, el_s = it
    chunks_per_iter = 2
    pre_chunks = 2

    n = seq_len // CHUNK
    c = CHUNK
    pw = 2 * DN_DK
    n_pairs = DN_HEADS // 2
    r64 = lax.broadcasted_iota(jnp.int32, (c, c), 0)
    c64 = lax.broadcasted_iota(jnp.int32, (c, c), 1)
    tril = jnp.where(r64 >= c64, 1.0, 0.0).astype(bf16)
    triu = jnp.where(r64 <= c64, 1.0, 0.0).astype(bf16)
    rp = lax.broadcasted_iota(jnp.int32, (c, pw), 0)
    cp = lax.broadcasted_iota(jnp.int32, (c, pw), 1) % c
    lower_p = rp >= cp
    upper_p = rp <= cp
    eye_p = jnp.where(rp == cp, 1.0, 0.0)
    same8 = (rp // 8) == (cp // 8)
    off16 = ((rp // 16) == (cp // 16)) & ~same8
    off32 = ((rp // 32) == (cp // 32)) & ((rp // 16) != (cp // 16))
    off64 = (rp // 32) != (cp // 32)
    rb2 = lax.broadcasted_iota(jnp.int32, (pw, pw), 0)
    cb2 = lax.broadcasted_iota(jnp.int32, (pw, pw), 1)
    bd_mask = (rb2 // c) == (cb2 // c)
    bd_triu = jnp.where(bd_mask & (rb2 % c <= cb2 % c), 1.0, 0.0).astype(bf16)
    bd_tril = jnp.where(bd_mask & (rb2 % c >= cb2 % c), 1.0, 0.0).astype(bf16)
    lane16 = lax.broadcasted_iota(jnp.int32, (1, N_GATES), 1)
    sub16 = lax.broadcasted_iota(jnp.int32, (N_GATES, 1), 0)
    sub8 = lax.broadcasted_iota(jnp.int32, (2 * n_pairs, 1), 0)
    sub_pw = lax.broadcasted_iota(jnp.int32, (pw, 1), 0)
    row_c = lax.broadcasted_iota(jnp.int32, (c, 1), 0)

    def bdiag(x):
        xb = x.astype(bf16)
        return jnp.where(bd_mask, jnp.concatenate([xb, xb], axis=0), jnp.zeros((), bf16))

    def conv_silu(ci):
        r0 = pl.multiple_of(ci * c, c)
        xb = dqkv_ref[0, pl.ds(r0, c), :].astype(f32)
        p0 = pl.multiple_of(jnp.maximum(r0 - 16, 0), 16)
        n0 = pl.multiple_of(jnp.minimum(r0 + c, seq_len - 16), 16)
        prev = dqkv_ref[0, pl.ds(p0, 16), :].astype(f32)[15:16, :]
        nxt = dqkv_ref[0, pl.ds(n0, 16), :].astype(f32)[0:1, :]
        prev = jnp.where(ci > 0, prev, 0.0)
        nxt = jnp.where(ci < n - 1, nxt, 0.0)
        xm = jnp.where(row_c == 0, prev, pltpu.roll(xb, 1, 0))
        xp = jnp.where(row_c == c - 1, nxt, pltpu.roll(xb, c - 1, 0))
        w = convw_ref[...]
        return _silu(xm * w[0:1, :] + xb * w[1:2, :] + xp * w[2:3, :])

    def gate_col(ci):
        ba = ba_ref[0, pl.ds(pl.multiple_of(ci * c, c), c), :]
        beta = jax.nn.sigmoid(ba[:, 0:N_GATES])
        g = -jnp.exp(alog_c[...]) * _softplus(ba[:, N_GATES:2 * N_GATES] + dtb_c[...])
        return beta, _split3(g)

    def pre_body(it, carry):
        cis = [it * pre_chunks + s for s in range(pre_chunks)]
        rows = [pl.multiple_of(ci * c, c) for ci in cis]
        act = [conv_silu(ci) for ci in cis]
        qk = [a[:, 0:2 * DN_W] for a in act]
        gates = [gate_col(ci) for ci in cis]
        g_r = [_split3(-jnp.exp(alog_r[...]) * _softplus(bar_ref[0, ci][N_GATES:2 * N_GATES, :] + dtb_r[...]))
               for ci in cis]
        g_p = [_split3(-jnp.exp(alog_p[...]) * _softplus(barp_ref[0, ci] + dtb_p[...])) for ci in cis]

        sumsq = _group_sumsq_many(qk, ones_ref[...])
        fwd = [sum(_dot(tril, p) for p in gsp) for _, gsp in gates]
        bwd = [sum(_dot(triu, p) for p in gsp) for _, gsp in gates]
        fwd_r = [sum(_dot(p, triu) for p in parts) for parts in g_r]
        bwd_r = [sum(_dot(p, tril) for p in parts) for parts in g_r]
        fwd_p = [sum(_dot(p, bd_triu) for p in parts) for parts in g_p]
        bwd_p = [sum(_dot(p, bd_tril) for p in parts) for parts in g_p]
        expand = expand_ref[...]
        gcum = [jnp.where(lane16 < DN_HEADS, f, b) for f, b in zip(fwd, bwd)]
        gx = [sum(_dot(p, expand) for p in _split3(gc_)) for gc_ in gcum]
        bx = [sum(_dot(p, expand) for p in _split2(beta)) for beta, _ in gates]

        for s, ci in enumerate(cis):
            r0 = rows[s]
            a = act[s]
            scale = lax.rsqrt(sumsq[s] + EPS)
            q_s[pl.ds(r0, c), :] = a[:, 0:DN_W] * scale[:, 0:DN_W] * (DN_DK ** -0.5)
            kn = a[:, DN_W:2 * DN_W] * scale[:, DN_W:2 * DN_W]
            k_s[pl.ds(r0, c), :] = kn
            v_s[pl.ds(r0, c), :] = a[:, 2 * DN_W:3 * DN_W]
            kt_s[ci] = jnp.transpose(jnp.concatenate([kn, kn], axis=0)).astype(bf16)
            gx_s[pl.ds(r0, c), :] = gx[s]
            bx_s[pl.ds(r0, c), :] = bx[s]
            gr_s[ci] = jnp.where(sub16 < DN_HEADS, fwd_r[s], bwd_r[s])
            grp_s[ci] = jnp.where(sub8 < n_pairs, fwd_p[s], bwd_p[s])
        return carry

    lax.fori_loop(0, n // pre_chunks, pre_body, 0)

    zero_blk = jnp.zeros((DN_DK, DN_DK), f32)
    for d in range(2):
        for p in range(n_pairs):
            if has_s0:
                top = jnp.concatenate([s0_ref[0, 0, d, 2 * p], zero_blk], axis=1)
                bot = jnp.concatenate([zero_blk, s0_ref[0, 0, d, 2 * p + 1]], axis=1)
                s_s[d * n_pairs + p] = jnp.concatenate([top, bot], axis=0)
            else:
                s_s[d * n_pairs + p] = jnp.zeros((pw, pw), f32)

    def each(fn, *lists):
        return [fn(*vals) for vals in zip(*lists)]

    def prep_body(it, carry):
        qs, ks, vs, bx, eg, decay, strict, ktd, kq, where_to = [], [], [], [], [], [], [], [], [], []
        for sub in range(chunks_per_iter):
            ci = it * chunks_per_iter + sub
            r0 = pl.multiple_of(ci * c, c)
            grp = grp_s[ci]
            gr16 = gr_s[ci]
            for p in range(n_pairs):
                sl = slice(p * pw, (p + 1) * pw)
                q_p = q_s[pl.ds(r0, c), sl]
                k_p = k_s[pl.ds(r0, c), sl]
                v_p = v_s[pl.ds(r0, c), sl]
                kt = kt_s[ci, p * pw:(p + 1) * pw, :]
                kq_p = _dot(jnp.concatenate([k_p, q_p], axis=0).astype(bf16),
                            jnp.where(bd_mask, kt, jnp.zeros((), bf16)))
                for d in range(2):
                    pp = d * n_pairs + p
                    j0 = d * DN_HEADS + 2 * p
                    xl = slice(pp * pw, (pp + 1) * pw)
                    g_end = gr16[:, c - 1:c] if d == 0 else gr16[:, 0:1]
                    e_rest_t = jnp.exp(g_end - gr16)
                    incl = lower_p if d == 0 else upper_p
                    gx = gx_s[pl.ds(r0, c), xl]
                    g_last = gx[c - 1:c, :] if d == 0 else gx[0:1, :]
                    el_s[ci, pp:pp + 1, :] = jnp.exp(g_last)
                    qs.append(q_p)
                    ks.append(k_p)
                    vs.append(v_p)
                    kq.append(kq_p)
                    bx.append(bx_s[pl.ds(r0, c), xl])
                    eg.append(jnp.exp(gx))
                    diff = gx - grp[pp:pp + 1, :]
                    decay.append(jnp.where(incl, jnp.exp(jnp.where(incl, diff, 0.0)), 0.0))
                    strict.append((rp > cp) if d == 0 else (rp < cp))
                    fac = jnp.where(sub_pw < c, e_rest_t[j0:j0 + 1, :], e_rest_t[j0 + 1:j0 + 2, :])
                    ktd.append((kt[:, 0:c].astype(f32) * fac).astype(bf16))
                    where_to.append((ci, pp, r0, sl, d))

        lmat = each(lambda m, x, b, dc: jnp.where(m, x[0:c] * b * dc, 0.0), strict, kq, bx, decay)
        amat = each(lambda x, dc: (x[c:2 * c] * dc).astype(bf16), kq, decay)
        d1 = each(lambda l: jnp.where(same8, l, 0.0), lmat)
        d2 = each(lambda a: _dot(a.astype(bf16), bdiag(a)), d1)
        tmat = each(lambda a: eye_p - a, d1)
        tmat = each(lambda t, a: t + _dot(a.astype(bf16), bdiag(t)), tmat, d2)
        d4 = each(lambda a: _dot(a.astype(bf16), bdiag(a)).astype(bf16), d2)
        tmat = each(lambda t, a: t + _dot(a, bdiag(t)), tmat, d4)
        for off_mask in (off16, off32, off64):
            tbd = each(bdiag, tmat)
            tl = each(lambda t, l: _dot(t.astype(bf16), bdiag(jnp.where(off_mask, l, 0.0))).astype(bf16), tmat, lmat)
            tmat = each(lambda t, a, b: t - _dot(a, b), tmat, tl, tbd)
        uw = each(lambda t, v, k, b, e: _dot(t.astype(bf16), jnp.concatenate([bdiag(v * b), bdiag(k * (b * e))], axis=1)),
                  tmat, vs, ks, bx, eg)
        ku_kw = each(lambda kd, y: _dot(kd, y.astype(bf16)), ktd, uw)
        au_aw = each(lambda a, y: _dot(a, jnp.concatenate([bdiag(y[:, 0:pw]), bdiag(y[:, pw:2 * pw])], axis=1)),
                     amat, uw)
        for (ci, pp, r0, sl, d), q_p, e, kk, aa in zip(where_to, qs, eg, ku_kw, au_aw):
            cu_s[ci, pp] = jnp.where(bd_mask, kk[:, 0:pw], 0.0)
            nw_s[ci, pp] = jnp.where(bd_mask, -kk[:, pw:2 * pw], 0.0).astype(bf16)
            qe_s[ci, pp] = (q_p * e - aa[:, pw:2 * pw]).astype(bf16)
            dst = of_s if d == 0 else ob_s
            dst[pl.ds(r0, c), sl] = aa[:, 0:pw]
        return carry

    lax.fori_loop(0, n // chunks_per_iter, prep_body, 0)

    def scan_body(i, carry):
        cis = [i, n - 1 - i]
        pairs = [(d, p) for d in range(2) for p in range(n_pairs)]
        s_old = [s_s[d * n_pairs + p] for d, p in pairs]
        s_b = [s.astype(bf16) for s in s_old]
        o_add = [_dot(qe_s[cis[d], d * n_pairs + p], sb) for (d, p), sb in zip(pairs, s_b)]
        s_upd = [_dot(nw_s[cis[d], d * n_pairs + p], sb) for (d, p), sb in zip(pairs, s_b)]
        for (d, p), s, oa, su in zip(pairs, s_old, o_add, s_upd):
            pp = d * n_pairs + p
            r0 = pl.multiple_of(cis[d] * c, c)
            dst = of_s if d == 0 else ob_s
            dst[pl.ds(r0, c), p * pw:(p + 1) * pw] += oa
            s_s[pp] = s * el_s[cis[d], pp:pp + 1, :] + su + cu_s[cis[d], pp]
        return carry

    lax.fori_loop(0, n, scan_body, 0)

    rb = 128

    def post_body(bi, carry):
        r0 = pl.multiple_of(bi * rb, rb)
        o = of_s[pl.ds(r0, rb), :] + ob_s[pl.ds(r0, rb), :]
        o = o * lax.rsqrt(_group_sumsq(o, ones_ref[...]) * (1.0 / DN_DK) + EPS) * dnn_ref[...]
        gate = dgate_ref[0, pl.ds(r0, rb), :].astype(f32)
        o_ref[0, pl.ds(r0, rb), :] = (o * _silu(gate)).astype(o_ref.dtype)
        return carry

    lax.fori_loop(0, seq_len // rb, post_body, 0)

    if want_state:
        for d in range(2):
            for p in range(n_pairs):
                s = s_s[d * n_pairs + p]
                st_ref[0, 0, d, 2 * p] = s[0:DN_DK, 0:DN_DK]
                st_ref[0, 0, d, 2 * p + 1] = s[DN_DK:pw, DN_DK:pw]


def _dn_call(dqkv3, ba3, bar4, barp4, dgate3, s0, consts, want_state, name):
    b, l, _ = dqkv3.shape
    n = l // CHUNK
    n_pairs = DN_HEADS // 2
    pw = 2 * DN_DK
    state_blk = (1, 1, 2, DN_HEADS, DN_DK, DN_DK)
    state_spec = pl.BlockSpec(state_blk, lambda i: (i, 0, 0, 0, 0, 0))
    in_specs = [pl.BlockSpec((1, l, 3 * DN_W), lambda i: (i, 0, 0)),
                pl.BlockSpec((1, l, 2 * N_GATES), lambda i: (i, 0, 0)),
                pl.BlockSpec((1, n, 2 * N_GATES, CHUNK), lambda i: (i, 0, 0, 0)),
                pl.BlockSpec((1, n, 2 * n_pairs, pw), lambda i: (i, 0, 0, 0)),
                pl.BlockSpec((1, l, DN_W), lambda i: (i, 0, 0))]
    args = [dqkv3, ba3, bar4, barp4, dgate3]
    if s0 is not None:
        in_specs.append(state_spec)
        args.append(s0)
    in_specs += [_const_spec(a.shape) for a in consts]
    args += consts
    out_shape = [jax.ShapeDtypeStruct((b, l, DN_W), bf16)]
    out_specs = [pl.BlockSpec((1, l, DN_W), lambda i: (i, 0, 0))]
    if want_state:
        out_shape.append(jax.ShapeDtypeStruct((b,) + state_blk[1:], f32))
        out_specs.append(state_spec)
    scratch = [pltpu.VMEM((l, DN_W), f32), pltpu.VMEM((l, DN_W), f32), pltpu.VMEM((l, DN_W), f32),
               pltpu.VMEM((n, DN_W, pw), bf16),
               pltpu.VMEM((l, N_GATES * DN_DK), f32), pltpu.VMEM((l, N_GATES * DN_DK), f32),
               pltpu.VMEM((n, 2 * n_pairs, pw), f32), pltpu.VMEM((n, N_GATES, CHUNK), f32),
               pltpu.VMEM((l, DN_W), f32), pltpu.VMEM((l, DN_W), f32),
               pltpu.VMEM((2 * n_pairs, pw, pw), f32),
               pltpu.VMEM((n, 2 * n_pairs, pw, pw), f32),
               pltpu.VMEM((n, 2 * n_pairs, pw, pw), bf16),
               pltpu.VMEM((n, 2 * n_pairs, CHUNK, pw), bf16),
               pltpu.VMEM((n, 2 * n_pairs, pw), f32)]
    res = pl.pallas_call(
        functools.partial(_dn_kernel, seq_len=l, has_s0=s0 is not None, want_state=want_state),
        grid=(b,),
        in_specs=in_specs,
        out_specs=tuple(out_specs),
        out_shape=tuple(out_shape),
        scratch_shapes=scratch,
        compiler_params=_cparams(("arbitrary",)),
        name=name,
    )(*args)
    return res if want_state else (res[0], None)


def _merge_kernel(oa_ref, od_ref, sga_ref, sgd_ref, x_ref, mod_ref, wa_ref, wd_ref, wo_ref, nffn_ref,
                  wr_ref, br_ref, x1_out, h2_out, comb_out):
    mod = mod_ref[0]
    gate1 = mod[:, 2 * D_MODEL:3 * D_MODEL]
    shift2 = mod[:, 3 * D_MODEL:4 * D_MODEL]
    scale2 = mod[:, 4 * D_MODEL:5 * D_MODEL]
    merged = (sga_ref[...].astype(f32) * _dot(oa_ref[...], wa_ref[...])
              + sgd_ref[...].astype(f32) * _dot(od_ref[...], wd_ref[...]))
    m = _dot(merged.astype(bf16), wo_ref[...])
    x1 = x_ref[...] + gate1 * m
    x1_out[...] = x1
    ms = jnp.mean(x1 * x1, axis=-1, keepdims=True)
    h2 = x1 * lax.rsqrt(ms + EPS) * nffn_ref[...]
    h2 = h2 * (1.0 + scale2) + shift2
    h2_out[...] = h2.astype(bf16)

    h_hi, h_lo = _split2(h2)
    w_hi, w_lo = _split2(wr_ref[...])
    logits = _dot(h_hi, w_hi) + _dot(h_hi, w_lo) + _dot(h_lo, w_hi) + br_ref[...]
    lane = lax.broadcasted_iota(jnp.int32, logits.shape, 1)
    big = ROUTER_LANES
    neg = -jnp.inf

    def first_lane(hit):
        return jnp.min(jnp.where(hit, lane, big), axis=-1, keepdims=True)

    is_g = lane < N_GROUPS
    gl = jnp.where(is_g, logits, neg)
    gexp = jnp.where(is_g, jnp.exp(gl - gl.max(axis=-1, keepdims=True)), 0.0)
    gp = gexp / gexp.sum(axis=-1, keepdims=True)
    g_top = gp.max(axis=-1, keepdims=True)
    g_idx = first_lane(is_g & (gp == g_top))
    lo_lane = N_GROUPS + g_idx * EXPERTS_PER_GROUP
    is_e = (lane >= lo_lane) & (lane < lo_lane + EXPERTS_PER_GROUP)
    el = jnp.where(is_e, logits, neg)
    eexp = jnp.where(is_e, jnp.exp(el - el.max(axis=-1, keepdims=True)), 0.0)
    ep = jnp.where(is_e, eexp / eexp.sum(axis=-1, keepdims=True), -1.0)
    p1 = ep.max(axis=-1, keepdims=True)
    i1 = first_lane(ep == p1)
    ep2 = jnp.where(lane == i1, -1.0, ep)
    p2 = ep2.max(axis=-1, keepdims=True)
    i2 = first_lane(ep2 == p2)
    tot = p1 + p2
    comb_out[...] = (jnp.where(lane == i1, p1 / tot * g_top, 0.0)
                     + jnp.where(lane == i2, p2 / tot * g_top, 0.0))


def _merge_call(oa, od, sga, sgd, x2d, mod3, mod_row, wa, wd, wo, nffn, wr, br, seq_len, name):
    t = x2d.shape[0]
    tm = 512
    tiles_per_seq = max(seq_len // tm, 1)

    def row_spec(w):
        return pl.BlockSpec((tm, w), lambda i: (i, 0))

    return pl.pallas_call(
        _merge_kernel,
        grid=(t // tm,),
        in_specs=[row_spec(ATTN_Q_W), row_spec(DN_W), row_spec(D_MODEL), row_spec(D_MODEL), row_spec(D_MODEL),
                  pl.BlockSpec((1, 1, N_MOD * D_MODEL), lambda i: (mod_row(i // tiles_per_seq), 0, 0)),
                  _const_spec(wa.shape), _const_spec(wd.shape), _const_spec(wo.shape),
                  _const_spec((1, D_MODEL)), _const_spec(wr.shape), _const_spec(br.shape)],
        out_specs=(row_spec(D_MODEL), row_spec(D_MODEL), row_spec(ROUTER_LANES)),
        out_shape=(jax.ShapeDtypeStruct((t, D_MODEL), f32),
                   jax.ShapeDtypeStruct((t, D_MODEL), bf16),
                   jax.ShapeDtypeStruct((t, ROUTER_LANES), f32)),
        compiler_params=_cparams(("arbitrary",)),
        name=name,
    )(oa, od, sga, sgd, x2d, mod3, wa, wd, wo, nffn, wr, br)


def _moe_kernel(h2_ref, comb_ref, x1_ref, mod_ref, wgu_ref, wdn_ref, nfin_ref, y_ref, acc_ref):
    g = pl.program_id(1)
    gw = EXPERTS_PER_GROUP * D_FF_EXPERT

    @pl.when(g == 0)
    def _():
        acc_ref[...] = jnp.zeros_like(acc_ref)

    gu = _dot(h2_ref[...], wgu_ref[0])
    hid = _silu(gu[:, 0:gw]) * gu[:, gw:2 * gw]
    comb = comb_ref[...]
    lane = lax.broadcasted_iota(jnp.int32, comb.shape, 1)
    first = N_GROUPS + g * EXPERTS_PER_GROUP
    parts = []
    for e in range(EXPERTS_PER_GROUP):
        ce = jnp.sum(jnp.where(lane == first + e, comb, 0.0), axis=-1, keepdims=True)
        parts.append((hid[:, e * D_FF_EXPERT:(e + 1) * D_FF_EXPERT] * ce).astype(bf16))
    acc_ref[...] += _dot(jnp.concatenate(parts, axis=1), wdn_ref[0])

    @pl.when(g == N_GROUPS - 1)
    def _():
        gate2 = mod_ref[0][:, 5 * D_MODEL:6 * D_MODEL]
        x2 = x1_ref[...] + gate2 * acc_ref[...]
        ms = jnp.mean(x2 * x2, axis=-1, keepdims=True)
        y_ref[...] = x2 * lax.rsqrt(ms + EPS) * nfin_ref[...]


def _moe_call(h2, comb, x1, mod3, mod_row, wgu, wdn, nfin, seq_len, name):
    t = h2.shape[0]
    tm = 1024
    tiles_per_seq = max(seq_len // tm, 1)
    return pl.pallas_call(
        _moe_kernel,
        grid=(t // tm, N_GROUPS),
        in_specs=[pl.BlockSpec((tm, D_MODEL), lambda i, e: (i, 0)),
                  pl.BlockSpec((tm, ROUTER_LANES), lambda i, e: (i, 0)),
                  pl.BlockSpec((tm, D_MODEL), lambda i, e: (i, 0)),
                  pl.BlockSpec((1, 1, N_MOD * D_MODEL), lambda i, e: (mod_row(i // tiles_per_seq), 0, 0)),
                  pl.BlockSpec((1, D_MODEL, 2 * EXPERTS_PER_GROUP * D_FF_EXPERT), lambda i, e: (e, 0, 0)),
                  pl.BlockSpec((1, EXPERTS_PER_GROUP * D_FF_EXPERT, D_MODEL), lambda i, e: (e, 0, 0)),
                  pl.BlockSpec((1, D_MODEL), lambda i, e: (0, 0))],
        out_specs=pl.BlockSpec((tm, D_MODEL), lambda i, e: (i, 0)),
        out_shape=jax.ShapeDtypeStruct((t, D_MODEL), f32),
        scratch_shapes=[pltpu.VMEM((tm, D_MODEL), f32)],
        compiler_params=_cparams(("arbitrary", "arbitrary")),
        name=name,
    )(h2, comb, x1, mod3, wgu, wdn, nfin)


def _rope_tables(seq_len):
    t = np.arange(seq_len)
    row = (t // GRID_W).astype(np.float64)
    col = (t % GRID_W).astype(np.float64)
    half = HEAD_DIM // 2
    inv = np.power(ROPE_THETA, -np.arange(0, half, 2, dtype=np.float64) / half)
    d = np.arange(HEAD_DIM)
    freq = inv[d % (half // 2)]
    pos = np.where(d[None, :] < half, row[:, None], col[:, None])
    ang = pos * freq[None, :]
    sign = np.where((d % half) < half // 2, -1.0, 1.0)
    cos = np.tile(np.cos(ang), (1, ATTN_HEADS)).astype(np.float32)
    sin = np.tile(np.sin(ang) * sign[None, :], (1, ATTN_HEADS)).astype(np.float32)
    return jnp.asarray(cos), jnp.asarray(sin)


def _ones_block_diag():
    i = np.arange(LANE_TILE)
    return jnp.asarray((i[:, None] // HEAD_DIM == i[None, :] // HEAD_DIM).astype(np.float32), dtype=bf16)


def _group_major(w):
    d, f = w.shape[1], w.shape[2]
    return w.reshape(N_GROUPS, EXPERTS_PER_GROUP, d, f).transpose(0, 2, 1, 3).reshape(
        N_GROUPS, d, EXPERTS_PER_GROUP * f)


def _lane_replicator():
    i = np.arange(HEAD_DIM)
    j = np.arange(ATTN_GROUP * HEAD_DIM)
    return jnp.asarray((i[:, None] == j[None, :] % HEAD_DIM).astype(np.float32), dtype=bf16)


def _trunk(x3, mod3, mod_row, wts, prefix):
    n_seq, seq_len, _ = x3.shape
    t = n_seq * seq_len
    x2d = x3.reshape(t, D_MODEL)
    is_lat = prefix is not None
    rope_tabs = _rope_tables(seq_len) if is_lat else None

    q, k, v, dqkv, ba, bat, dgate, sga, sgd = _proj_call(
        x2d, mod3, mod_row, wts["nmix"], wts["wmain"], wts["wba"], wts["wbat"], wts["qn"], wts["kn"],
        wts["ones_bd"], rope_tabs, n_seq, seq_len)

    q3 = q.reshape(n_seq, seq_len, ATTN_Q_W)
    if is_lat:
        pk, pv, s0 = prefix
        o_attn = _attn_call(q3, [pk, pv, k, v], wts["rep"], 256, "attn_lat")
    else:
        s0 = None
        o_attn = _attn_call(q3, [k, v], wts["rep"], seq_len, "attn_ctx")

    n_chunks = seq_len // CHUNK
    bar4 = bat.reshape(2 * N_GATES, n_seq, n_chunks, CHUNK).transpose(1, 2, 0, 3)
    barp4 = bat[N_GATES:].reshape(N_GATES // 2, 2, n_seq, n_chunks, CHUNK).transpose(2, 3, 0, 1, 4).reshape(
        n_seq, n_chunks, N_GATES // 2, 2 * CHUNK)
    dn_consts = [wts[name] for name in ("convw", "alog_c", "dtb_c", "alog_r", "dtb_r", "alog_p", "dtb_p", "dnn",
                                        "ones_bd", "expand")]
    o_dn, state = _dn_call(
        dqkv.reshape(n_seq, seq_len, 3 * DN_W), ba.reshape(n_seq, seq_len, 2 * N_GATES), bar4, barp4,
        dgate.reshape(n_seq, seq_len, DN_W), s0, dn_consts, not is_lat, "dn_lat" if is_lat else "dn_ctx")

    x1, h2, comb = _merge_call(
        o_attn.reshape(t, ATTN_Q_W), o_dn.reshape(t, DN_W), sga, sgd, x2d, mod3, mod_row,
        wts["wa"], wts["wd"], wts["wo"], wts["nffn"], wts["wr"], wts["br"], seq_len,
        "merge_lat" if is_lat else "merge_ctx")

    y = _moe_call(h2, comb, x1, mod3, mod_row, wts["wgu"], wts["wdn"], wts["nfin"], seq_len,
                  "moe_lat" if is_lat else "moe_ctx")
    return y.reshape(n_seq, seq_len, D_MODEL), k, v, state


def kernel(x_prompt, x_sample, c, cache_attn_k, cache_attn_v, state_delta, c_ctx, w_mod, b_mod, norm_mix, norm_ffn, norm_final, w_in, q_norm, k_norm, conv_w, a_log, dt_bias, dn_norm, w_attn_br, w_dn_br, w_out, w_rg, b_rg, w_re, b_re, w_gate_e, w_up_e, w_down_e):
    layer = 0
    n_lat = x_sample.shape[0]
    w_in_l = w_in[layer]
    ba_lo = OFF_DGATE
    wts = {
        "nmix": norm_mix[layer][None, :],
        "wmain": jnp.concatenate([w_in_l[:, :ba_lo], w_in_l[:, ba_lo + 2 * N_GATES:]], axis=1).astype(bf16),
        "wba": w_in_l[:, ba_lo:ba_lo + 2 * N_GATES].astype(bf16),
        "wbat": w_in_l[:, ba_lo:ba_lo + 2 * N_GATES].T.astype(bf16),
        "qn": jnp.tile(q_norm[layer], ATTN_HEADS)[None, :],
        "kn": jnp.tile(k_norm[layer], ATTN_KV_HEADS)[None, :],
        "ones_bd": _ones_block_diag(),
        "rep": _lane_replicator(),
        "convw": conv_w[layer],
        "alog_c": a_log[layer].reshape(1, N_GATES),
        "dtb_c": dt_bias[layer].reshape(1, N_GATES),
        "alog_r": a_log[layer].reshape(N_GATES, 1),
        "dtb_r": dt_bias[layer].reshape(N_GATES, 1),
        "alog_p": jnp.repeat(a_log[layer].reshape(N_GATES // 2, 2), CHUNK, axis=1),
        "dtb_p": jnp.repeat(dt_bias[layer].reshape(N_GATES // 2, 2), CHUNK, axis=1),
        "expand": jnp.asarray(np.repeat(np.eye(N_GATES, dtype=np.float32), DN_DK, axis=1), dtype=bf16),
        "dnn": jnp.tile(dn_norm[layer], DN_HEADS)[None, :],
        "wa": w_attn_br[layer].astype(bf16),
        "wd": w_dn_br[layer].astype(bf16),
        "wo": w_out[layer].astype(bf16),
        "nffn": norm_ffn[layer][None, :],
        "wr": jnp.concatenate([w_rg[layer], w_re[layer],
                               jnp.zeros((D_MODEL, ROUTER_LANES - N_GROUPS - N_EXPERTS), f32)], axis=1),
        "br": jnp.concatenate([b_rg[layer], b_re[layer],
                               jnp.zeros((ROUTER_LANES - N_GROUPS - N_EXPERTS,), f32)])[None, :],
        "wgu": jnp.concatenate([_group_major(w_gate_e[layer]), _group_major(w_up_e[layer])], axis=-1).astype(bf16),
        "wdn": w_down_e[layer].reshape(N_GROUPS, EXPERTS_PER_GROUP * D_FF_EXPERT, D_MODEL).astype(bf16),
        "nfin": norm_final[None, :],
    }
    cond8 = jnp.concatenate([c_ctx[None, :], c, jnp.zeros((8 - 1 - n_lat, D_MODEL), f32)], axis=0)
    mod3 = _mod_call(cond8, w_mod[layer], b_mod[layer][None, :])[:, None, :]

    y_prompt, new_k, new_v, new_state = _trunk(x_prompt, mod3, lambda s: 0, wts, None)
    y_sample, _, _, _ = _trunk(x_sample, mod3, lambda s: s + 1, wts,
                               (cache_attn_k, cache_attn_v, state_delta))
    return (y_prompt, y_sample, new_k, new_v, new_state)
```

```python
import functools

import numpy as np
import jax
import jax.numpy as jnp
from jax import lax
from jax.experimental import pallas as pl
from jax.experimental.pallas import tpu as pltpu

f32 = jnp.float32
bf16 = jnp.bfloat16

D_MODEL = 1024
HEAD_DIM = 64
ATTN_HEADS = 8
ATTN_KV_HEADS = 2
ATTN_GROUP = ATTN_HEADS // ATTN_KV_HEADS
GRID_W = 64
ROPE_THETA = 10000.0
DN_HEADS = 8
DN_DK = 64
CHUNK = 64
N_GROUPS = 4
EXPERTS_PER_GROUP = 4
N_EXPERTS = 16
D_FF_EXPERT = 256
N_MOD = 6
EPS = 1e-6

ATTN_Q_W = ATTN_HEADS * HEAD_DIM
ATTN_KV_W = ATTN_KV_HEADS * HEAD_DIM
DN_W = DN_HEADS * DN_DK
N_GATES = 2 * DN_HEADS

OFF_Q = 0
OFF_K = ATTN_Q_W
OFF_V = OFF_K + ATTN_KV_W
OFF_DQKV = OFF_V + ATTN_KV_W
OFF_DGATE = OFF_DQKV + 3 * DN_W
OFF_GATTN = OFF_DGATE + DN_W
OFF_GDN = OFF_GATTN + D_MODEL
W_MAIN = OFF_GDN + D_MODEL

ROUTER_LANES = 128
VMEM_LIMIT = 56 * 1024 * 1024

_TRANS_B = (((1,), (1,)), ((), ()))
_TRANS_A = (((0,), (0,)), ((), ()))


def _cparams(sem):
    return pltpu.CompilerParams(dimension_semantics=sem, vmem_limit_bytes=VMEM_LIMIT)


def _dot(a, b):
    return jnp.dot(a, b, preferred_element_type=f32)


def _silu(x):
    return x * jax.nn.sigmoid(x)


def _softplus(x):
    return jnp.maximum(x, 0.0) + jnp.log1p(jnp.exp(-jnp.abs(x)))


def _split2(x):
    hi = x.astype(bf16)
    lo = (x - hi.astype(f32)).astype(bf16)
    return hi, lo


def _split3(x):
    a = x.astype(bf16)
    r = x - a.astype(f32)
    b = r.astype(bf16)
    c = (r - b.astype(f32)).astype(bf16)
    return a, b, c


LANE_TILE = 128


def _group_sumsq_many(arrs, ones_pair):
    pieces = []
    for x in arrs:
        hi, lo = _split2(x * x)
        for s in range(x.shape[1] // LANE_TILE):
            pieces.append(hi[:, s * LANE_TILE:(s + 1) * LANE_TILE])
            pieces.append(lo[:, s * LANE_TILE:(s + 1) * LANE_TILE])
    res = _dot(jnp.concatenate(pieces, axis=0), ones_pair)
    out, off = [], 0
    for x in arrs:
        r = x.shape[0]
        slabs = []
        for s in range(x.shape[1] // LANE_TILE):
            slabs.append(res[off:off + r] + res[off + r:off + 2 * r])
            off += 2 * r
        out.append(slabs[0] if len(slabs) == 1 else jnp.concatenate(slabs, axis=1))
    return out


def _group_sumsq(x, ones_pair):
    return _group_sumsq_many([x], ones_pair)[0]


def _const_spec(shape):
    nd = len(shape)
    return pl.BlockSpec(shape, lambda *_: (0,) * nd)


def _mod_kernel(c_ref, w_ref, b_ref, o_ref):
    c = c_ref[...]
    o_ref[...] = _dot(_silu(c).astype(bf16), w_ref[...].astype(bf16)) + b_ref[...]


def _mod_call(cond8, w_mod, b_mod):
    tn = 1536
    n = w_mod.shape[1]
    return pl.pallas_call(
        _mod_kernel,
        grid=(n // tn,),
        in_specs=[_const_spec((8, D_MODEL)),
                  pl.BlockSpec((D_MODEL, tn), lambda j: (0, j)),
                  pl.BlockSpec((1, tn), lambda j: (0, j))],
        out_specs=pl.BlockSpec((8, tn), lambda j: (0, j)),
        out_shape=jax.ShapeDtypeStruct((8, n), f32),
        compiler_params=_cparams(("arbitrary",)),
        name="mod",
    )(cond8, w_mod, b_mod)


def _rope(x, cos, sin):
    w = x.shape[-1]
    lane = lax.broadcasted_iota(jnp.int32, x.shape, 1)
    first = (lane % 32) < 16
    swapped = jnp.where(first, pltpu.roll(x, w - 16, 1), pltpu.roll(x, 16, 1))
    return x * cos + swapped * sin


def _proj_kernel(*refs, rope, seq_per_tile):
    (x_ref, mod_ref, nmix_ref, wa_ref, wb_ref, wba_ref, qn_ref, kn_ref, ones_ref) = refs[:9]
    pos = 9
    if rope:
        cos_ref, sin_ref = refs[9:11]
        pos = 11
    outs = list(refs[pos:])
    q_out, kcat_out, vcat_out = outs[:3]
    if not rope:
        k_out, v_out = outs[3:5]
        outs = outs[2:]
    (dqkv_out, ba_out, bat_out, dgate_out, sga_out, sgd_out) = outs[3:]

    x = x_ref[...]
    mod = mod_ref[0]
    shift1 = mod[:, 0:D_MODEL]
    scale1 = mod[:, D_MODEL:2 * D_MODEL]
    ms = jnp.mean(x * x, axis=-1, keepdims=True)
    h = x * lax.rsqrt(ms + EPS) * nmix_ref[...]
    h = h * (1.0 + scale1) + shift1
    hb = h.astype(bf16)

    aq = _dot(hb, wa_ref[:, OFF_Q:OFF_Q + ATTN_Q_W])
    ak = _dot(hb, wa_ref[:, OFF_K:OFF_K + ATTN_KV_W])
    av = _dot(hb, wa_ref[:, OFF_V:OFF_V + ATTN_KV_W])
    ssq, ssk = _group_sumsq_many([aq, ak], ones_ref[...])
    aq = aq * lax.rsqrt(ssq * (1.0 / HEAD_DIM) + EPS) * qn_ref[...]
    ak = ak * lax.rsqrt(ssk * (1.0 / HEAD_DIM) + EPS) * kn_ref[...]
    if not rope:
        tm = x.shape[0]
        seq = tm // seq_per_tile
        akt = jnp.transpose(ak)
        avt = jnp.transpose(av)
        for s in range(seq_per_tile):
            for kv in range(ATTN_KV_HEADS):
                k_out[s, 0, kv] = akt[kv * HEAD_DIM:(kv + 1) * HEAD_DIM, s * seq:(s + 1) * seq]
                v_out[s, 0, kv] = avt[kv * HEAD_DIM:(kv + 1) * HEAD_DIM, s * seq:(s + 1) * seq]
    else:
        aq = _rope(aq, cos_ref[...], sin_ref[...])
        ak = _rope(ak, cos_ref[:, 0:ATTN_KV_W], sin_ref[:, 0:ATTN_KV_W])
    q_out[...] = (aq * (HEAD_DIM ** -0.5)).astype(bf16)
    kcat_out[...] = ak.astype(bf16)
    vcat_out[...] = av.astype(bf16)

    dqkv_out[...] = _dot(hb, wa_ref[:, OFF_DQKV:OFF_DGATE]).astype(bf16)
    ba = _dot(hb, wba_ref[...])
    ba_out[...] = ba[:, 0:2 * N_GATES]
    bat_out[...] = jnp.transpose(ba)[0:2 * N_GATES, :]
    dgate_out[...] = _dot(hb, wb_ref[:, 0:DN_W]).astype(bf16)
    sga_out[...] = jax.nn.sigmoid(_dot(hb, wb_ref[:, DN_W:DN_W + D_MODEL])).astype(bf16)
    sgd_out[...] = jax.nn.sigmoid(_dot(hb, wb_ref[:, DN_W + D_MODEL:DN_W + 2 * D_MODEL])).astype(bf16)


def _proj_call(x2d, mod3, mod_row, nmix, wa, wb, wba, qn, kn, ones_bd, rope_tabs, n_seq, seq_len):
    t = x2d.shape[0]
    tm = 512
    rope = rope_tabs is not None
    tiles_per_seq = max(seq_len // tm, 1)
    seq_per_tile = max(tm // seq_len, 1)
    in_specs = [pl.BlockSpec((tm, D_MODEL), lambda i: (i, 0)),
                pl.BlockSpec((1, 1, N_MOD * D_MODEL), lambda i: (mod_row(i // tiles_per_seq), 0, 0)),
                _const_spec((1, D_MODEL)),
                _const_spec(wa.shape),
                _const_spec(wb.shape),
                _const_spec((D_MODEL, LANE_TILE)),
                _const_spec((1, ATTN_Q_W)),
                _const_spec((1, ATTN_KV_W)),
                _const_spec((LANE_TILE, LANE_TILE))]
    args = [x2d, mod3, nmix, wa, wb, wba, qn, kn, ones_bd]

    def row_spec(w):
        return pl.BlockSpec((tm, w), lambda i: (i, 0))

    out_shape = [jax.ShapeDtypeStruct((t, ATTN_Q_W), bf16),
                 jax.ShapeDtypeStruct((t, ATTN_KV_W), bf16),
                 jax.ShapeDtypeStruct((t, ATTN_KV_W), bf16)]
    out_specs = [row_spec(ATTN_Q_W), row_spec(ATTN_KV_W), row_spec(ATTN_KV_W)]
    if rope:
        in_specs += [pl.BlockSpec((tm, ATTN_Q_W), lambda i: (i % tiles_per_seq, 0))] * 2
        args += list(rope_tabs)
    else:
        kv_shape = jax.ShapeDtypeStruct((n_seq, 1, ATTN_KV_HEADS, HEAD_DIM, seq_len), f32)
        kv_spec = pl.BlockSpec((seq_per_tile, 1, ATTN_KV_HEADS, HEAD_DIM, seq_len), lambda i: (i, 0, 0, 0, 0))
        out_shape += [kv_shape, kv_shape]
        out_specs += [kv_spec, kv_spec]
    out_shape += [jax.ShapeDtypeStruct((t, 3 * DN_W), bf16),
                  jax.ShapeDtypeStruct((t, 2 * N_GATES), f32),
                  jax.ShapeDtypeStruct((2 * N_GATES, t), f32),
                  jax.ShapeDtypeStruct((t, DN_W), bf16),
                  jax.ShapeDtypeStruct((t, D_MODEL), bf16),
                  jax.ShapeDtypeStruct((t, D_MODEL), bf16)]
    out_specs += [row_spec(3 * DN_W), row_spec(2 * N_GATES),
                  pl.BlockSpec((2 * N_GATES, tm), lambda i: (0, i)),
                  row_spec(DN_W), row_spec(D_MODEL), row_spec(D_MODEL)]
    return pl.pallas_call(
        functools.partial(_proj_kernel, rope=rope, seq_per_tile=seq_per_tile),
        grid=(t // tm,),
        in_specs=in_specs,
        out_specs=tuple(out_specs),
        out_shape=tuple(out_shape),
        compiler_params=_cparams(("arbitrary",)),
        name="proj_lat" if rope else "proj_ctx",
    )(*args)


def _attn_kernel(*refs, n_sets):
    q_ref = refs[0]
    kv_refs = refs[1:1 + 2 * n_sets]
    rep_ref, repcat_ref = refs[1 + 2 * n_sets:3 + 2 * n_sets]
    o_ref = refs[-1]
    width = ATTN_GROUP * HEAD_DIM
    blk = lax.broadcasted_iota(jnp.int32, (1, width), 1) // HEAD_DIM

    def head_slab(ref, kv):
        if len(ref.shape) == 5:
            return _dot(ref[0, 0, kv].astype(bf16), rep_ref[...])
        return _dot(ref[0], repcat_ref[kv])

    for kv in range(ATTN_KV_HEADS):
        q = q_ref[0, :, kv * width:(kv + 1) * width]
        k4 = [head_slab(kv_refs[2 * s], kv) for s in range(n_sets)]
        v4 = [head_slab(kv_refs[2 * s + 1], kv) for s in range(n_sets)]
        acc = jnp.zeros((q.shape[0], width), f32)
        for g in range(ATTN_GROUP):
            sel = blk == g
            scores = [lax.dot_general(q, jnp.where(sel, k, 0.0).astype(bf16), _TRANS_B,
                                      preferred_element_type=f32) for k in k4]
            m = scores[0].max(axis=-1, keepdims=True)
            for s in scores[1:]:
                m = jnp.maximum(m, s.max(axis=-1, keepdims=True))
            probs = [jnp.exp(s - m) for s in scores]
            denom = probs[0].sum(axis=-1, keepdims=True)
            for p in probs[1:]:
                denom = denom + p.sum(axis=-1, keepdims=True)
            og = _dot(probs[0].astype(bf16), jnp.where(sel, v4[0], 0.0).astype(bf16))
            for p, v in zip(probs[1:], v4[1:]):
                og = og + _dot(p.astype(bf16), jnp.where(sel, v, 0.0).astype(bf16))
            acc = acc + og * (1.0 / denom)
        o_ref[0, :, kv * width:(kv + 1) * width] = acc.astype(o_ref.dtype)


def _attn_call(q3, kv_sets, rep, repcat, tq, name):
    b, l, _ = q3.shape
    in_specs = [pl.BlockSpec((1, tq, ATTN_Q_W), lambda i, t: (i, t, 0))]
    args = [q3]
    for arr in kv_sets:
        nd = arr.ndim
        blk = (1,) + arr.shape[1:]
        in_specs.append(pl.BlockSpec(blk, lambda i, t, nd=nd: (i,) + (0,) * (nd - 1)))
        args.append(arr)
    in_specs += [_const_spec(rep.shape), _const_spec(repcat.shape)]
    args += [rep, repcat]
    return pl.pallas_call(
        functools.partial(_attn_kernel, n_sets=len(kv_sets) // 2),
        grid=(b, l // tq),
        in_specs=in_specs,
        out_specs=pl.BlockSpec((1, tq, ATTN_Q_W), lambda i, t: (i, t, 0)),
        out_shape=jax.ShapeDtypeStruct((b, l, ATTN_Q_W), bf16),
        compiler_params=_cparams(("arbitrary", "arbitrary")),
        name=name,
    )(*args)


def _dn_kernel(*refs, seq_len, has_s0, want_state):
    it = iter(refs)
    dqkv_ref = next(it)
    ba_ref = next(it)
    bar_ref = next(it)
    barp_ref = next(it)
    dgate_ref = next(it)
    s0_ref = next(it) if has_s0 else None
    convw_ref = next(it)
    alog_c = next(it)
    dtb_c = next(it)
    alog_r = next(it)
    dtb_r = next(it)
    alog_p = next(it)
    dtb_p = next(it)
    dnn_ref = next(it)
    ones_ref = next(it)
    expand_ref = next(it)
    o_ref = next(it)
    st_ref = next(it) if want_state else None
    q_s, k_s, v_s, kt_s, gx_s, bx_s, grp_s, gr_s, of_s, ob_s, s_s, cu_s, nw_s, qe_s, el_s = it
    chunks_per_iter = 2
    pre_chunks = 2

    n = seq_len // CHUNK
    c = CHUNK
    pw = 2 * DN_DK
    n_pairs = DN_HEADS // 2
    r64 = lax.broadcasted_iota(jnp.int32, (c, c), 0)
    c64 = lax.broadcasted_iota(jnp.int32, (c, c), 1)
    tril = jnp.where(r64 >= c64, 1.0, 0.0).astype(bf16)
    triu = jnp.where(r64 <= c64, 1.0, 0.0).astype(bf16)
    rp = lax.broadcasted_iota(jnp.int32, (c, pw), 0)
    cp = lax.broadcasted_iota(jnp.int32, (c, pw), 1) % c
    lower_p = rp >= cp
    upper_p = rp <= cp
    eye_p = jnp.where(rp == cp, 1.0, 0.0)
    same8 = (rp // 8) == (cp // 8)
    off16 = ((rp // 16) == (cp // 16)) & ~same8
    off32 = ((rp // 32) == (cp // 32)) & ((rp // 16) != (cp // 16))
    off64 = (rp // 32) != (cp // 32)
    rb2 = lax.broadcasted_iota(jnp.int32, (pw, pw), 0)
    cb2 = lax.broadcasted_iota(jnp.int32, (pw, pw), 1)
    bd_mask = (rb2 // c) == (cb2 // c)
    bd_triu = jnp.where(bd_mask & (rb2 % c <= cb2 % c), 1.0, 0.0).astype(bf16)
    bd_tril = jnp.where(bd_mask & (rb2 % c >= cb2 % c), 1.0, 0.0).astype(bf16)
    lane16 = lax.broadcasted_iota(jnp.int32, (1, N_GATES), 1)
    sub16 = lax.broadcasted_iota(jnp.int32, (N_GATES, 1), 0)
    sub8 = lax.broadcasted_iota(jnp.int32, (2 * n_pairs, 1), 0)
    sub_pw = lax.broadcasted_iota(jnp.int32, (pw, 1), 0)
    row_c = lax.broadcasted_iota(jnp.int32, (c, 1), 0)

    def bdiag(x):
        xb = x.astype(bf16)
        return jnp.where(bd_mask, jnp.concatenate([xb, xb], axis=0), jnp.zeros((), bf16))

    def conv_silu(ci):
        r0 = pl.multiple_of(ci * c, c)
        xb = dqkv_ref[0, pl.ds(r0, c), :].astype(f32)
        p0 = pl.multiple_of(jnp.maximum(r0 - 16, 0), 16)
        n0 = pl.multiple_of(jnp.minimum(r0 + c, seq_len - 16), 16)
        prev = dqkv_ref[0, pl.ds(p0, 16), :].astype(f32)[15:16, :]
        nxt = dqkv_ref[0, pl.ds(n0, 16), :].astype(f32)[0:1, :]
        prev = jnp.where(ci > 0, prev, 0.0)
        nxt = jnp.where(ci < n - 1, nxt, 0.0)
        xm = jnp.where(row_c == 0, prev, pltpu.roll(xb, 1, 0))
        xp = jnp.where(row_c == c - 1, nxt, pltpu.roll(xb, c - 1, 0))
        w = convw_ref[...]
        return _silu(xm * w[0:1, :] + xb * w[1:2, :] + xp * w[2:3, :])

    def gate_col(ci):
        ba = ba_ref[0, pl.ds(pl.multiple_of(ci * c, c), c), :]
        beta = jax.nn.sigmoid(ba[:, 0:N_GATES])
        g = -jnp.exp(alog_c[...]) * _softplus(ba[:, N_GATES:2 * N_GATES] + dtb_c[...])
        return beta, _split3(g)

    def pre_body(it, carry):
        cis = [it * pre_chunks + s for s in range(pre_chunks)]
        rows = [pl.multiple_of(ci * c, c) for ci in cis]
        act = [conv_silu(ci) for ci in cis]
        qk = [a[:, 0:2 * DN_W] for a in act]
        gates = [gate_col(ci) for ci in cis]
        g_r = [_split3(-jnp.exp(alog_r[...]) * _softplus(bar_ref[0, ci][N_GATES:2 * N_GATES, :] + dtb_r[...]))
               for ci in cis]
        g_p = [_split3(-jnp.exp(alog_p[...]) * _softplus(barp_ref[0, ci] + dtb_p[...])) for ci in cis]

        sumsq = _group_sumsq_many(qk, ones_ref[...])
        fwd = [sum(_dot(tril, p) for p in gsp) for _, gsp in gates]
        bwd = [sum(_dot(triu, p) for p in gsp) for _, gsp in gates]
        fwd_r = [sum(_dot(p, triu) for p in parts) for parts in g_r]
        bwd_r = [sum(_dot(p, tril) for p in parts) for parts in g_r]
        fwd_p = [sum(_dot(p, bd_triu) for p in parts) for parts in g_p]
        bwd_p = [sum(_dot(p, bd_tril) for p in parts) for parts in g_p]
        expand = expand_ref[...]
        gcum = [jnp.where(lane16 < DN_HEADS, f, b) for f, b in zip(fwd, bwd)]
        gx = [sum(_dot(p, expand) for p in _split3(gc_)) for gc_ in gcum]
        bx = [sum(_dot(p, expand) for p in _split2(beta)) for beta, _ in gates]

        for s, ci in enumerate(cis):
            r0 = rows[s]
            a = act[s]
            scale = lax.rsqrt(sumsq[s] + EPS)
            q_s[pl.ds(r0, c), :] = a[:, 0:DN_W] * scale[:, 0:DN_W] * (DN_DK ** -0.5)
            kn = a[:, DN_W:2 * DN_W] * scale[:, DN_W:2 * DN_W]
            k_s[pl.ds(r0, c), :] = kn
            v_s[pl.ds(r0, c), :] = a[:, 2 * DN_W:3 * DN_W]
            kt_s[ci] = jnp.transpose(jnp.concatenate([kn, kn], axis=0)).astype(bf16)
            gx_s[pl.ds(r0, c), :] = gx[s]
            bx_s[pl.ds(r0, c), :] = bx[s]
            gr_s[ci] = jnp.where(sub16 < DN_HEADS, fwd_r[s], bwd_r[s])
            grp_s[ci] = jnp.where(sub8 < n_pairs, fwd_p[s], bwd_p[s])
        return carry

    lax.fori_loop(0, n // pre_chunks, pre_body, 0)

    zero_blk = jnp.zeros((DN_DK, DN_DK), f32)
    for d in range(2):
        for p in range(n_pairs):
            if has_s0:
                top = jnp.concatenate([s0_ref[0, 0, d, 2 * p], zero_blk], axis=1)
                bot = jnp.concatenate([zero_blk, s0_ref[0, 0, d, 2 * p + 1]], axis=1)
                s_s[d * n_pairs + p] = jnp.concatenate([top, bot], axis=0)
            else:
                s_s[d * n_pairs + p] = jnp.zeros((pw, pw), f32)

    def each(fn, *lists):
        return [fn(*vals) for vals in zip(*lists)]

    def prep_body(it, carry):
        qs, ks, vs, bx, eg, decay, strict, ktd, kq, where_to = [], [], [], [], [], [], [], [], [], []
        for sub in range(chunks_per_iter):
            ci = it * chunks_per_iter + sub
            r0 = pl.multiple_of(ci * c, c)
            grp = grp_s[ci]
            gr16 = gr_s[ci]
            for p in range(n_pairs):
                sl = slice(p * pw, (p + 1) * pw)
                q_p = q_s[pl.ds(r0, c), sl]
                k_p = k_s[pl.ds(r0, c), sl]
                v_p = v_s[pl.ds(r0, c), sl]
                kt = kt_s[ci, p * pw:(p + 1) * pw, :]
                kq_p = _dot(jnp.concatenate([k_p, q_p], axis=0).astype(bf16),
                            jnp.where(bd_mask, kt, jnp.zeros((), bf16)))
                for d in range(2):
                    pp = d * n_pairs + p
                    j0 = d * DN_HEADS + 2 * p
                    xl = slice(pp * pw, (pp + 1) * pw)
                    g_end = gr16[:, c - 1:c] if d == 0 else gr16[:, 0:1]
                    e_rest_t = jnp.exp(g_end - gr16)
                    incl = lower_p if d == 0 else upper_p
                    gx = gx_s[pl.ds(r0, c), xl]
                    g_last = gx[c - 1:c, :] if d == 0 else gx[0:1, :]
                    el_s[ci, pp:pp + 1, :] = jnp.exp(g_last)
                    qs.append(q_p)
                    ks.append(k_p)
                    vs.append(v_p)
                    kq.append(kq_p)
                    bx.append(bx_s[pl.ds(r0, c), xl])
                    eg.append(jnp.exp(gx))
                    diff = gx - grp[pp:pp + 1, :]
                    decay.append(jnp.where(incl, jnp.exp(jnp.where(incl, diff, 0.0)), 0.0))
                    strict.append((rp > cp) if d == 0 else (rp < cp))
                    fac = jnp.where(sub_pw < c, e_rest_t[j0:j0 + 1, :], e_rest_t[j0 + 1:j0 + 2, :])
                    ktd.append((kt[:, 0:c].astype(f32) * fac).astype(bf16))
                    where_to.append((ci, pp, r0, sl, d))

        lmat = each(lambda m, x, b, dc: jnp.where(m, x[0:c] * b * dc, 0.0), strict, kq, bx, decay)
        amat = each(lambda x, dc: (x[c:2 * c] * dc).astype(bf16), kq, decay)
        d1 = each(lambda l: jnp.where(same8, l, 0.0), lmat)
        d2 = each(lambda a: _dot(a.astype(bf16), bdiag(a)), d1)
        tmat = each(lambda a: eye_p - a, d1)
        tmat = each(lambda t, a: t + _dot(a.astype(bf16), bdiag(t)), tmat, d2)
        d4 = each(lambda a: _dot(a.astype(bf16), bdiag(a)).astype(bf16), d2)
        tmat = each(lambda t, a: t + _dot(a, bdiag(t)), tmat, d4)
        for off_mask in (off16, off32, off64):
            tbd = each(bdiag, tmat)
            tl = each(lambda t, l: _dot(t.astype(bf16), bdiag(jnp.where(off_mask, l, 0.0))).astype(bf16), tmat, lmat)
            tmat = each(lambda t, a, b: t - _dot(a, b), tmat, tl, tbd)
        uw = each(lambda t, v, k, b, e: _dot(t.astype(bf16), jnp.concatenate([bdiag(v * b), bdiag(k * (b * e))], axis=1)),
                  tmat, vs, ks, bx, eg)
        ku_kw = each(lambda kd, y: _dot(kd, y.astype(bf16)), ktd, uw)
        au_aw = each(lambda a, y: _dot(a, jnp.concatenate([bdiag(y[:, 0:pw]), bdiag(y[:, pw:2 * pw])], axis=1)),
                     amat, uw)
        for (ci, pp, r0, sl, d), q_p, e, kk, aa in zip(where_to, qs, eg, ku_kw, au_aw):
            cu_s[ci, pp] = jnp.where(bd_mask, kk[:, 0:pw], 0.0)
            nw_s[ci, pp] = jnp.where(bd_mask, -kk[:, pw:2 * pw], 0.0).astype(bf16)
            qe_s[ci, pp] = (q_p * e - aa[:, pw:2 * pw]).astype(bf16)
            dst = of_s if d == 0 else ob_s
            dst[pl.ds(r0, c), sl] = aa[:, 0:pw]
        return carry

    lax.fori_loop(0, n // chunks_per_iter, prep_body, 0)

    def scan_body(i, carry):
        cis = [i, n - 1 - i]
        pairs = [(d, p) for d in range(2) for p in range(n_pairs)]
        s_old = [s_s[d * n_pairs + p] for d, p in pairs]
        s_b = [s.astype(bf16) for s in s_old]
        o_add = [_dot(qe_s[cis[d], d * n_pairs + p], sb) for (d, p), sb in zip(pairs, s_b)]
        s_upd = [_dot(nw_s[cis[d], d * n_pairs + p], sb) for (d, p), sb in zip(pairs, s_b)]
        for (d, p), s, oa, su in zip(pairs, s_old, o_add, s_upd):
            pp = d * n_pairs + p
            r0 = pl.multiple_of(cis[d] * c, c)
            dst = of_s if d == 0 else ob_s
            dst[pl.ds(r0, c), p * pw:(p + 1) * pw] += oa
            s_s[pp] = s * el_s[cis[d], pp:pp + 1, :] + su + cu_s[cis[d], pp]
        return carry

    lax.fori_loop(0, n, scan_body, 0)

    rb = 128

    def post_body(bi, carry):
        r0 = pl.multiple_of(bi * rb, rb)
        o = of_s[pl.ds(r0, rb), :] + ob_s[pl.ds(r0, rb), :]
        o = o * lax.rsqrt(_group_sumsq(o, ones_ref[...]) * (1.0 / DN_DK) + EPS) * dnn_ref[...]
        gate = dgate_ref[0, pl.ds(r0, rb), :].astype(f32)
        o_ref[0, pl.ds(r0, rb), :] = (o * _silu(gate)).astype(o_ref.dtype)
        return carry

    lax.fori_loop(0, seq_len // rb, post_body, 0)

    if want_state:
        for d in range(2):
            for p in range(n_pairs):
                s = s_s[d * n_pairs + p]
                st_ref[0, 0, d, 2 * p] = s[0:DN_DK, 0:DN_DK]
                st_ref[0, 0, d, 2 * p + 1] = s[DN_DK:pw, DN_DK:pw]


def _dn_call(dqkv3, ba3, bar4, barp4, dgate3, s0, consts, want_state, name):
    b, l, _ = dqkv3.shape
    n = l // CHUNK
    n_pairs = DN_HEADS // 2
    pw = 2 * DN_DK
    state_blk = (1, 1, 2, DN_HEADS, DN_DK, DN_DK)
    state_spec = pl.BlockSpec(state_blk, lambda i: (i, 0, 0, 0, 0, 0))
    in_specs = [pl.BlockSpec((1, l, 3 * DN_W), lambda i: (i, 0, 0)),
                pl.BlockSpec((1, l, 2 * N_GATES), lambda i: (i, 0, 0)),
                pl.BlockSpec((1, n, 2 * N_GATES, CHUNK), lambda i: (i, 0, 0, 0)),
                pl.BlockSpec((1, n, 2 * n_pairs, pw), lambda i: (i, 0, 0, 0)),
                pl.BlockSpec((1, l, DN_W), lambda i: (i, 0, 0))]
    args = [dqkv3, ba3, bar4, barp4, dgate3]
    if s0 is not None:
        in_specs.append(state_spec)
        args.append(s0)
    in_specs += [_const_spec(a.shape) for a in consts]
    args += consts
    out_shape = [jax.ShapeDtypeStruct((b, l, DN_W), bf16)]
    out_specs = [pl.BlockSpec((1, l, DN_W), lambda i: (i, 0, 0))]
    if want_state:
        out_shape.append(jax.ShapeDtypeStruct((b,) + state_blk[1:], f32))
        out_specs.append(state_spec)
    scratch = [pltpu.VMEM((l, DN_W), f32), pltpu.VMEM((l, DN_W), f32), pltpu.VMEM((l, DN_W), f32),
               pltpu.VMEM((n, DN_W, pw), bf16),
               pltpu.VMEM((l, N_GATES * DN_DK), f32), pltpu.VMEM((l, N_GATES * DN_DK), f32),
               pltpu.VMEM((n, 2 * n_pairs, pw), f32), pltpu.VMEM((n, N_GATES, CHUNK), f32),
               pltpu.VMEM((l, DN_W), f32), pltpu.VMEM((l, DN_W), f32),
               pltpu.VMEM((2 * n_pairs, pw, pw), f32),
               pltpu.VMEM((n, 2 * n_pairs, pw, pw), f32),
               pltpu.VMEM((n, 2 * n_pairs, pw, pw), bf16),
               pltpu.VMEM((n, 2 * n_pairs, CHUNK, pw), bf16),
               pltpu.VMEM((n, 2 * n_pairs, pw), f32)]
    res = pl.pallas_call(
        functools.partial(_dn_kernel, seq_len=l, has_s0=s0 is not None, want_state=want_state),
        grid=(b,),
        in_specs=in_specs,
        out_specs=tuple(out_specs),
        out_shape=tuple(out_shape),
        scratch_shapes=scratch,
        compiler_params=_cparams(("arbitrary",)),
        name=name,
    )(*args)
    return res if want_state else (res[0], None)


def _merge_kernel(oa_ref, od_ref, sga_ref, sgd_ref, x_ref, mod_ref, wa_ref, wd_ref, wo_ref, nffn_ref,
                  wr_ref, br_ref, x1_out, h2_out, comb_out):
    mod = mod_ref[0]
    gate1 = mod[:, 2 * D_MODEL:3 * D_MODEL]
    shift2 = mod[:, 3 * D_MODEL:4 * D_MODEL]
    scale2 = mod[:, 4 * D_MODEL:5 * D_MODEL]
    merged = (sga_ref[...].astype(f32) * _dot(oa_ref[...], wa_ref[...])
              + sgd_ref[...].astype(f32) * _dot(od_ref[...], wd_ref[...]))
    m = _dot(merged.astype(bf16), wo_ref[...])
    x1 = x_ref[...] + gate1 * m
    x1_out[...] = x1
    ms = jnp.mean(x1 * x1, axis=-1, keepdims=True)
    h2 = x1 * lax.rsqrt(ms + EPS) * nffn_ref[...]
    h2 = h2 * (1.0 + scale2) + shift2
    h2_out[...] = h2.astype(bf16)

    h_hi, h_lo = _split2(h2)
    w_hi, w_lo = _split2(wr_ref[...])
    logits = _dot(h_hi, w_hi) + _dot(h_hi, w_lo) + _dot(h_lo, w_hi) + br_ref[...]
    lane = lax.broadcasted_iota(jnp.int32, logits.shape, 1)
    big = ROUTER_LANES
    neg = -jnp.inf

    def first_lane(hit):
        return jnp.min(jnp.where(hit, lane, big), axis=-1, keepdims=True)

    is_g = lane < N_GROUPS
    gl = jnp.where(is_g, logits, neg)
    gexp = jnp.where(is_g, jnp.exp(gl - gl.max(axis=-1, keepdims=True)), 0.0)
    gp = gexp / gexp.sum(axis=-1, keepdims=True)
    g_top = gp.max(axis=-1, keepdims=True)
    g_idx = first_lane(is_g & (gp == g_top))
    lo_lane = N_GROUPS + g_idx * EXPERTS_PER_GROUP
    is_e = (lane >= lo_lane) & (lane < lo_lane + EXPERTS_PER_GROUP)
    el = jnp.where(is_e, logits, neg)
    eexp = jnp.where(is_e, jnp.exp(el - el.max(axis=-1, keepdims=True)), 0.0)
    ep = jnp.where(is_e, eexp / eexp.sum(axis=-1, keepdims=True), -1.0)
    p1 = ep.max(axis=-1, keepdims=True)
    i1 = first_lane(ep == p1)
    ep2 = jnp.where(lane == i1, -1.0, ep)
    p2 = ep2.max(axis=-1, keepdims=True)
    i2 = first_lane(ep2 == p2)
    tot = p1 + p2
    comb_out[...] = (jnp.where(lane == i1, p1 / tot * g_top, 0.0)
                     + jnp.where(lane == i2, p2 / tot * g_top, 0.0))


def _merge_call(oa, od, sga, sgd, x2d, mod3, mod_row, wa, wd, wo, nffn, wr, br, seq_len, name):
    t = x2d.shape[0]
    tm = 512
    tiles_per_seq = max(seq_len // tm, 1)

    def row_spec(w):
        return pl.BlockSpec((tm, w), lambda i: (i, 0))

    return pl.pallas_call(
        _merge_kernel,
        grid=(t // tm,),
        in_specs=[row_spec(ATTN_Q_W), row_spec(DN_W), row_spec(D_MODEL), row_spec(D_MODEL), row_spec(D_MODEL),
                  pl.BlockSpec((1, 1, N_MOD * D_MODEL), lambda i: (mod_row(i // tiles_per_seq), 0, 0)),
                  _const_spec(wa.shape), _const_spec(wd.shape), _const_spec(wo.shape),
                  _const_spec((1, D_MODEL)), _const_spec(wr.shape), _const_spec(br.shape)],
        out_specs=(row_spec(D_MODEL), row_spec(D_MODEL), row_spec(ROUTER_LANES)),
        out_shape=(jax.ShapeDtypeStruct((t, D_MODEL), f32),
                   jax.ShapeDtypeStruct((t, D_MODEL), bf16),
                   jax.ShapeDtypeStruct((t, ROUTER_LANES), f32)),
        compiler_params=_cparams(("arbitrary",)),
        name=name,
    )(oa, od, sga, sgd, x2d, mod3, wa, wd, wo, nffn, wr, br)


def _moe_kernel(h2_ref, comb_ref, x1_ref, mod_ref, wg_ref, wu_ref, wdn_ref, nfin_ref, y_ref, acc_ref):
    g = pl.program_id(1)

    @pl.when(g == 0)
    def _():
        acc_ref[...] = jnp.zeros_like(acc_ref)

    h2 = h2_ref[...]
    comb = comb_ref[...]
    lane = lax.broadcasted_iota(jnp.int32, comb.shape, 1)
    first = N_GROUPS + g * EXPERTS_PER_GROUP
    parts = []
    for e in range(EXPERTS_PER_GROUP):
        hid = _silu(_dot(h2, wg_ref[e])) * _dot(h2, wu_ref[e])
        ce = jnp.sum(jnp.where(lane == first + e, comb, 0.0), axis=-1, keepdims=True)
        parts.append((hid * ce).astype(bf16))
    acc_ref[...] += _dot(jnp.concatenate(parts, axis=1), wdn_ref[0])

    @pl.when(g == N_GROUPS - 1)
    def _():
        gate2 = mod_ref[0][:, 5 * D_MODEL:6 * D_MODEL]
        x2 = x1_ref[...] + gate2 * acc_ref[...]
        ms = jnp.mean(x2 * x2, axis=-1, keepdims=True)
        y_ref[...] = x2 * lax.rsqrt(ms + EPS) * nfin_ref[...]


def _moe_call(h2, comb, x1, mod3, mod_row, wg, wu, wdn, nfin, seq_len, name):
    t = h2.shape[0]
    tm = 1024
    tiles_per_seq = max(seq_len // tm, 1)
    return pl.pallas_call(
        _moe_kernel,
        grid=(t // tm, N_GROUPS),
        in_specs=[pl.BlockSpec((tm, D_MODEL), lambda i, e: (i, 0)),
                  pl.BlockSpec((tm, ROUTER_LANES), lambda i, e: (i, 0)),
                  pl.BlockSpec((tm, D_MODEL), lambda i, e: (i, 0)),
                  pl.BlockSpec((1, 1, N_MOD * D_MODEL), lambda i, e: (mod_row(i // tiles_per_seq), 0, 0)),
                  pl.BlockSpec((EXPERTS_PER_GROUP, D_MODEL, D_FF_EXPERT), lambda i, e: (e, 0, 0)),
                  pl.BlockSpec((EXPERTS_PER_GROUP, D_MODEL, D_FF_EXPERT), lambda i, e: (e, 0, 0)),
                  pl.BlockSpec((1, EXPERTS_PER_GROUP * D_FF_EXPERT, D_MODEL), lambda i, e: (e, 0, 0)),
                  pl.BlockSpec((1, D_MODEL), lambda i, e: (0, 0))],
        out_specs=pl.BlockSpec((tm, D_MODEL), lambda i, e: (i, 0)),
        out_shape=jax.ShapeDtypeStruct((t, D_MODEL), f32),
        scratch_shapes=[pltpu.VMEM((tm, D_MODEL), f32)],
        compiler_params=_cparams(("arbitrary", "arbitrary")),
        name=name,
    )(h2, comb, x1, mod3, wg, wu, wdn, nfin)


def _rope_tables(seq_len):
    t = np.arange(seq_len)
    row = (t // GRID_W).astype(np.float64)
    col = (t % GRID_W).astype(np.float64)
    half = HEAD_DIM // 2
    inv = np.power(ROPE_THETA, -np.arange(0, half, 2, dtype=np.float64) / half)
    d = np.arange(HEAD_DIM)
    freq = inv[d % (half // 2)]
    pos = np.where(d[None, :] < half, row[:, None], col[:, None])
    ang = pos * freq[None, :]
    sign = np.where((d % half) < half // 2, -1.0, 1.0)
    cos = np.tile(np.cos(ang), (1, ATTN_HEADS)).astype(np.float32)
    sin = np.tile(np.sin(ang) * sign[None, :], (1, ATTN_HEADS)).astype(np.float32)
    return jnp.asarray(cos), jnp.asarray(sin)


def _ones_block_diag():
    i = np.arange(LANE_TILE)
    return jnp.asarray((i[:, None] // HEAD_DIM == i[None, :] // HEAD_DIM).astype(np.float32), dtype=bf16)


def _lane_replicator():
    i = np.arange(HEAD_DIM)
    j = np.arange(ATTN_GROUP * HEAD_DIM)
    return jnp.asarray((i[:, None] == j[None, :] % HEAD_DIM).astype(np.float32), dtype=bf16)


def _lane_replicator_cat():
    i = np.arange(ATTN_KV_W)
    j = np.arange(ATTN_GROUP * HEAD_DIM)
    mats = [(i[:, None] == kv * HEAD_DIM + j[None, :] % HEAD_DIM).astype(np.float32) for kv in range(ATTN_KV_HEADS)]
    return jnp.asarray(np.stack(mats), dtype=bf16)


def _trunk(x3, mod3, mod_row, wts, prefix):
    n_seq, seq_len, _ = x3.shape
    t = n_seq * seq_len
    x2d = x3.reshape(t, D_MODEL)
    is_lat = prefix is not None
    rope_tabs = _rope_tables(seq_len) if is_lat else None

    res = _proj_call(
        x2d, mod3, mod_row, wts["nmix"], wts["w_qkv"], wts["w_gates"], wts["wba"], wts["qn"],
        wts["kn"], wts["ones_bd"], rope_tabs, n_seq, seq_len)
    q, kcat, vcat = res[:3]
    k, v = (None, None) if is_lat else (jnp.swapaxes(res[3], 3, 4), jnp.swapaxes(res[4], 3, 4))
    dqkv, ba, bat, dgate, sga, sgd = res[-6:]

    q3 = q.reshape(n_seq, seq_len, ATTN_Q_W)
    kcat3 = kcat.reshape(n_seq, seq_len, ATTN_KV_W)
    vcat3 = vcat.reshape(n_seq, seq_len, ATTN_KV_W)
    if is_lat:
        pk, pv, s0 = prefix
        o_attn = _attn_call(q3, [pk, pv, kcat3, vcat3], wts["rep"], wts["repcat"], 256, "attn_lat")
    else:
        s0 = None
        o_attn = _attn_call(q3, [kcat3, vcat3], wts["rep"], wts["repcat"], seq_len, "attn_ctx")

    n_chunks = seq_len // CHUNK
    bar4 = bat.reshape(2 * N_GATES, n_seq, n_chunks, CHUNK).transpose(1, 2, 0, 3)
    barp4 = bat[N_GATES:].reshape(N_GATES // 2, 2, n_seq, n_chunks, CHUNK).transpose(2, 3, 0, 1, 4).reshape(
        n_seq, n_chunks, N_GATES // 2, 2 * CHUNK)
    dn_consts = [wts[name] for name in ("convw", "alog_c", "dtb_c", "alog_r", "dtb_r", "alog_p", "dtb_p", "dnn",
                                        "ones_bd", "expand")]
    o_dn, state = _dn_call(
        dqkv.reshape(n_seq, seq_len, 3 * DN_W), ba.reshape(n_seq, seq_len, 2 * N_GATES), bar4, barp4,
        dgate.reshape(n_seq, seq_len, DN_W), s0, dn_consts, not is_lat, "dn_lat" if is_lat else "dn_ctx")

    x1, h2, comb = _merge_call(
        o_attn.reshape(t, ATTN_Q_W), o_dn.reshape(t, DN_W), sga, sgd, x2d, mod3, mod_row,
        wts["wa"], wts["wd"], wts["wo"], wts["nffn"], wts["wr"], wts["br"], seq_len,
        "merge_lat" if is_lat else "merge_ctx")

    y = _moe_call(h2, comb, x1, mod3, mod_row, wts["wg"], wts["wu"], wts["wdn"], wts["nfin"], seq_len,
                  "moe_lat" if is_lat else "moe_ctx")
    return y.reshape(n_seq, seq_len, D_MODEL), k, v, state


def kernel(x_prompt, x_sample, c, cache_attn_k, cache_attn_v, state_delta, c_ctx, w_mod, b_mod, norm_mix, norm_ffn, norm_final, w_in, q_norm, k_norm, conv_w, a_log, dt_bias, dn_norm, w_attn_br, w_dn_br, w_out, w_rg, b_rg, w_re, b_re, w_gate_e, w_up_e, w_down_e):
    layer = 0
    n_lat = x_sample.shape[0]
    w_in_l = w_in[layer]
    ba_lo = OFF_DGATE
    wts = {
        "nmix": norm_mix[layer][None, :],
        "w_qkv": w_in_l[:, :ba_lo].astype(bf16),
        "w_gates": w_in_l[:, ba_lo + 2 * N_GATES:].astype(bf16),
        "wba": jnp.pad(w_in_l[:, ba_lo:ba_lo + 2 * N_GATES], ((0, 0), (0, LANE_TILE - 2 * N_GATES))).astype(bf16),
        "qn": jnp.tile(q_norm[layer], ATTN_HEADS)[None, :],
        "kn": jnp.tile(k_norm[layer], ATTN_KV_HEADS)[None, :],
        "ones_bd": _ones_block_diag(),
        "rep": _lane_replicator(),
        "repcat": _lane_replicator_cat(),
        "convw": conv_w[layer],
        "alog_c": a_log[layer].reshape(1, N_GATES),
        "dtb_c": dt_bias[layer].reshape(1, N_GATES),
        "alog_r": a_log[layer].reshape(N_GATES, 1),
        "dtb_r": dt_bias[layer].reshape(N_GATES, 1),
        "alog_p": jnp.repeat(a_log[layer].reshape(N_GATES // 2, 2), CHUNK, axis=1),
        "dtb_p": jnp.repeat(dt_bias[layer].reshape(N_GATES // 2, 2), CHUNK, axis=1),
        "expand": jnp.asarray(np.repeat(np.eye(N_GATES, dtype=np.float32), DN_DK, axis=1), dtype=bf16),
        "dnn": jnp.tile(dn_norm[layer], DN_HEADS)[None, :],
        "wa": w_attn_br[layer].astype(bf16),
        "wd": w_dn_br[layer].astype(bf16),
        "wo": w_out[layer].astype(bf16),
        "nffn": norm_ffn[layer][None, :],
        "wr": jnp.concatenate([w_rg[layer], w_re[layer],
                               jnp.zeros((D_MODEL, ROUTER_LANES - N_GROUPS - N_EXPERTS), f32)], axis=1),
        "br": jnp.concatenate([b_rg[layer], b_re[layer],
                               jnp.zeros((ROUTER_LANES - N_GROUPS - N_EXPERTS,), f32)])[None, :],
        "wg": w_gate_e[layer].astype(bf16),
        "wu": w_up_e[layer].astype(bf16),
        "wdn": w_down_e[layer].reshape(N_GROUPS, EXPERTS_PER_GROUP * D_FF_EXPERT, D_MODEL).astype(bf16),
        "nfin": norm_final[None, :],
    }
    cond8 = jnp.concatenate([c_ctx[None, :], c, jnp.zeros((8 - 1 - n_lat, D_MODEL), f32)], axis=0)
    mod3 = _mod_call(cond8, w_mod[layer], b_mod[layer][None, :])[:, None, :]

    y_prompt, new_k, new_v, new_state = _trunk(x_prompt, mod3, lambda s: 0, wts, None)
    y_sample, _, _, _ = _trunk(x_sample, mod3, lambda s: s + 1, wts,
                               (cache_attn_k, cache_attn_v, state_delta))
    return (y_prompt, y_sample, new_k, new_v, new_state)
```

```python
import functools

import numpy as np
import jax
import jax.numpy as jnp
from jax import lax
from jax.experimental import pallas as pl
from jax.experimental.pallas import tpu as pltpu

f32 = jnp.float32
bf16 = jnp.bfloat16

D_MODEL = 1024
HEAD_DIM = 64
ATTN_HEADS = 8
ATTN_KV_HEADS = 2
ATTN_GROUP = ATTN_HEADS // ATTN_KV_HEADS
GRID_W = 64
ROPE_THETA = 10000.0
DN_HEADS = 8
DN_DK = 64
CHUNK = 64
N_GROUPS = 4
EXPERTS_PER_GROUP = 4
N_EXPERTS = 16
D_FF_EXPERT = 256
N_MOD = 6
EPS = 1e-6

ATTN_Q_W = ATTN_HEADS * HEAD_DIM
ATTN_KV_W = ATTN_KV_HEADS * HEAD_DIM
DN_W = DN_HEADS * DN_DK
N_GATES = 2 * DN_HEADS

OFF_Q = 0
OFF_K = ATTN_Q_W
OFF_V = OFF_K + ATTN_KV_W
OFF_DQKV = OFF_V + ATTN_KV_W
OFF_DGATE = OFF_DQKV + 3 * DN_W
OFF_GATTN = OFF_DGATE + DN_W
OFF_GDN = OFF_GATTN + D_MODEL
W_MAIN = OFF_GDN + D_MODEL

ROUTER_LANES = 128
VMEM_LIMIT = 56 * 1024 * 1024

_TRANS_B = (((1,), (1,)), ((), ()))
_TRANS_A = (((0,), (0,)), ((), ()))


def _cparams(sem):
    return pltpu.CompilerParams(dimension_semantics=sem, vmem_limit_bytes=VMEM_LIMIT)


def _dot(a, b):
    return jnp.dot(a, b, preferred_element_type=f32)


def _silu(x):
    return x * jax.nn.sigmoid(x)


def _softplus(x):
    return jnp.maximum(x, 0.0) + jnp.log1p(jnp.exp(-jnp.abs(x)))


def _split2(x):
    hi = x.astype(bf16)
    lo = (x - hi.astype(f32)).astype(bf16)
    return hi, lo


def _split3(x):
    a = x.astype(bf16)
    r = x - a.astype(f32)
    b = r.astype(bf16)
    c = (r - b.astype(f32)).astype(bf16)
    return a, b, c


LANE_TILE = 128
DN_CHUNKS_PER_ITER = 2


def _group_sumsq_many(arrs, ones_pair):
    pieces = []
    for x in arrs:
        hi, lo = _split2(x * x)
        for s in range(x.shape[1] // LANE_TILE):
            pieces.append(hi[:, s * LANE_TILE:(s + 1) * LANE_TILE])
            pieces.append(lo[:, s * LANE_TILE:(s + 1) * LANE_TILE])
    res = _dot(jnp.concatenate(pieces, axis=0), ones_pair)
    out, off = [], 0
    for x in arrs:
        r = x.shape[0]
        slabs = []
        for s in range(x.shape[1] // LANE_TILE):
            slabs.append(res[off:off + r] + res[off + r:off + 2 * r])
            off += 2 * r
        out.append(slabs[0] if len(slabs) == 1 else jnp.concatenate(slabs, axis=1))
    return out


def _group_sumsq(x, ones_pair):
    return _group_sumsq_many([x], ones_pair)[0]


def _const_spec(shape):
    nd = len(shape)
    return pl.BlockSpec(shape, lambda *_: (0,) * nd, pipeline_mode=pl.Buffered(1))


def _mod_kernel(c_ref, w_ref, b_ref, o_ref):
    c = c_ref[...]
    o_ref[...] = _dot(_silu(c).astype(bf16), w_ref[...].astype(bf16)) + b_ref[...]


def _mod_call(cond8, w_mod, b_mod):
    tn = 1536
    n = w_mod.shape[1]
    return pl.pallas_call(
        _mod_kernel,
        grid=(n // tn,),
        in_specs=[_const_spec((8, D_MODEL)),
                  pl.BlockSpec((D_MODEL, tn), lambda j: (0, j)),
                  pl.BlockSpec((1, tn), lambda j: (0, j))],
        out_specs=pl.BlockSpec((8, tn), lambda j: (0, j)),
        out_shape=jax.ShapeDtypeStruct((8, n), f32),
        compiler_params=_cparams(("arbitrary",)),
        name="mod",
    )(cond8, w_mod, b_mod)


def _rope(x, cos, sin):
    w = x.shape[-1]
    lane = lax.broadcasted_iota(jnp.int32, x.shape, 1)
    first = (lane % 32) < 16
    swapped = jnp.where(first, pltpu.roll(x, w - 16, 1), pltpu.roll(x, 16, 1))
    return x * cos + swapped * sin


def _proj_kernel(*refs, rope, seq_per_tile):
    (x_ref, mod_ref, nmix_ref, wa_ref, wb_ref, wba_ref, qn_ref, kn_ref, ones_ref) = refs[:9]
    pos = 9
    if rope:
        cos_ref, sin_ref = refs[9:11]
        pos = 11
    outs = list(refs[pos:])
    q_out, kcat_out, vcat_out = outs[:3]
    if not rope:
        k_out, v_out = outs[3:5]
        outs = outs[2:]
    (dqkv_out, ba_out, bat_out, dgate_out, sga_out, sgd_out) = outs[3:]

    x = x_ref[...]
    mod = mod_ref[0]
    shift1 = mod[:, 0:D_MODEL]
    scale1 = mod[:, D_MODEL:2 * D_MODEL]
    ms = jnp.mean(x * x, axis=-1, keepdims=True)
    h = x * lax.rsqrt(ms + EPS) * nmix_ref[...]
    h = h * (1.0 + scale1) + shift1
    hb = h.astype(bf16)

    aq = _dot(hb, wa_ref[:, OFF_Q:OFF_Q + ATTN_Q_W])
    ak = _dot(hb, wa_ref[:, OFF_K:OFF_K + ATTN_KV_W])
    av = _dot(hb, wa_ref[:, OFF_V:OFF_V + ATTN_KV_W])
    ssq, ssk = _group_sumsq_many([aq, ak], ones_ref[...])
    aq = aq * lax.rsqrt(ssq * (1.0 / HEAD_DIM) + EPS) * qn_ref[...]
    ak = ak * lax.rsqrt(ssk * (1.0 / HEAD_DIM) + EPS) * kn_ref[...]
    if not rope:
        tm = x.shape[0]
        seq = tm // seq_per_tile
        akt = jnp.transpose(ak)
        avt = jnp.transpose(av)
        for s in range(seq_per_tile):
            for kv in range(ATTN_KV_HEADS):
                k_out[s, 0, kv] = akt[kv * HEAD_DIM:(kv + 1) * HEAD_DIM, s * seq:(s + 1) * seq]
                v_out[s, 0, kv] = avt[kv * HEAD_DIM:(kv + 1) * HEAD_DIM, s * seq:(s + 1) * seq]
    else:
        aq = _rope(aq, cos_ref[...], sin_ref[...])
        ak = _rope(ak, cos_ref[:, 0:ATTN_KV_W], sin_ref[:, 0:ATTN_KV_W])
    q_out[...] = (aq * (HEAD_DIM ** -0.5)).astype(bf16)
    kcat_out[...] = ak.astype(bf16)
    vcat_out[...] = av.astype(bf16)

    dqkv_out[...] = _dot(hb, wa_ref[:, OFF_DQKV:OFF_DGATE]).astype(bf16)
    ba = _dot(hb, wba_ref[...])
    ba_out[...] = ba[:, 0:2 * N_GATES]
    bat_out[...] = jnp.transpose(ba)[0:2 * N_GATES, :]
    dgate_out[...] = _dot(hb, wb_ref[:, 0:DN_W]).astype(bf16)
    sga_out[...] = jax.nn.sigmoid(_dot(hb, wb_ref[:, DN_W:DN_W + D_MODEL])).astype(bf16)
    sgd_out[...] = jax.nn.sigmoid(_dot(hb, wb_ref[:, DN_W + D_MODEL:DN_W + 2 * D_MODEL])).astype(bf16)


def _proj_call(x2d, mod3, mod_row, nmix, wa, wb, wba, qn, kn, ones_bd, rope_tabs, n_seq, seq_len):
    t = x2d.shape[0]
    tm = 512
    rope = rope_tabs is not None
    tiles_per_seq = max(seq_len // tm, 1)
    seq_per_tile = max(tm // seq_len, 1)
    in_specs = [pl.BlockSpec((tm, D_MODEL), lambda i: (i, 0)),
                pl.BlockSpec((1, 1, N_MOD * D_MODEL), lambda i: (mod_row(i // tiles_per_seq), 0, 0)),
                _const_spec((1, D_MODEL)),
                _const_spec(wa.shape),
                _const_spec(wb.shape),
                _const_spec((D_MODEL, LANE_TILE)),
                _const_spec((1, ATTN_Q_W)),
                _const_spec((1, ATTN_KV_W)),
                _const_spec((LANE_TILE, LANE_TILE))]
    args = [x2d, mod3, nmix, wa, wb, wba, qn, kn, ones_bd]

    def row_spec(w):
        return pl.BlockSpec((tm, w), lambda i: (i, 0))

    out_shape = [jax.ShapeDtypeStruct((t, ATTN_Q_W), bf16),
                 jax.ShapeDtypeStruct((t, ATTN_KV_W), bf16),
                 jax.ShapeDtypeStruct((t, ATTN_KV_W), bf16)]
    out_specs = [row_spec(ATTN_Q_W), row_spec(ATTN_KV_W), row_spec(ATTN_KV_W)]
    if rope:
        in_specs += [pl.BlockSpec((tm, ATTN_Q_W), lambda i: (i % tiles_per_seq, 0))] * 2
        args += list(rope_tabs)
    else:
        kv_shape = jax.ShapeDtypeStruct((n_seq, 1, ATTN_KV_HEADS, HEAD_DIM, seq_len), f32)
        kv_spec = pl.BlockSpec((seq_per_tile, 1, ATTN_KV_HEADS, HEAD_DIM, seq_len), lambda i: (i, 0, 0, 0, 0))
        out_shape += [kv_shape, kv_shape]
        out_specs += [kv_spec, kv_spec]
    out_shape += [jax.ShapeDtypeStruct((t, 3 * DN_W), bf16),
                  jax.ShapeDtypeStruct((t, 2 * N_GATES), f32),
                  jax.ShapeDtypeStruct((2 * N_GATES, t), f32),
                  jax.ShapeDtypeStruct((t, DN_W), bf16),
                  jax.ShapeDtypeStruct((t, D_MODEL), bf16),
                  jax.ShapeDtypeStruct((t, D_MODEL), bf16)]
    out_specs += [row_spec(3 * DN_W), row_spec(2 * N_GATES),
                  pl.BlockSpec((2 * N_GATES, tm), lambda i: (0, i)),
                  row_spec(DN_W), row_spec(D_MODEL), row_spec(D_MODEL)]
    return pl.pallas_call(
        functools.partial(_proj_kernel, rope=rope, seq_per_tile=seq_per_tile),
        grid=(t // tm,),
        in_specs=in_specs,
        out_specs=tuple(out_specs),
        out_shape=tuple(out_shape),
        compiler_params=_cparams(("arbitrary",)),
        name="proj_lat" if rope else "proj_ctx",
    )(*args)


def _attn_kernel(*refs, n_sets):
    q_ref = refs[0]
    kv_refs = refs[1:1 + 2 * n_sets]
    rep_ref, repcat_ref = refs[1 + 2 * n_sets:3 + 2 * n_sets]
    o_ref = refs[-1]
    width = ATTN_GROUP * HEAD_DIM
    blk = lax.broadcasted_iota(jnp.int32, (1, width), 1) // HEAD_DIM

    def head_slab(ref, kv):
        if len(ref.shape) == 5:
            return _dot(ref[0, 0, kv].astype(bf16), rep_ref[...])
        return _dot(ref[0], repcat_ref[kv])

    for kv in range(ATTN_KV_HEADS):
        q = q_ref[0, :, kv * width:(kv + 1) * width]
        k4 = [head_slab(kv_refs[2 * s], kv) for s in range(n_sets)]
        v4 = [head_slab(kv_refs[2 * s + 1], kv) for s in range(n_sets)]
        acc = jnp.zeros((q.shape[0], width), f32)
        for g in range(ATTN_GROUP):
            sel = blk == g
            scores = [lax.dot_general(q, jnp.where(sel, k, 0.0).astype(bf16), _TRANS_B,
                                      preferred_element_type=f32) for k in k4]
            m = scores[0].max(axis=-1, keepdims=True)
            for s in scores[1:]:
                m = jnp.maximum(m, s.max(axis=-1, keepdims=True))
            probs = [jnp.exp(s - m) for s in scores]
            denom = probs[0].sum(axis=-1, keepdims=True)
            for p in probs[1:]:
                denom = denom + p.sum(axis=-1, keepdims=True)
            og = _dot(probs[0].astype(bf16), jnp.where(sel, v4[0], 0.0).astype(bf16))
            for p, v in zip(probs[1:], v4[1:]):
                og = og + _dot(p.astype(bf16), jnp.where(sel, v, 0.0).astype(bf16))
            acc = acc + og * (1.0 / denom)
        o_ref[0, :, kv * width:(kv + 1) * width] = acc.astype(o_ref.dtype)


def _attn_call(q3, kv_sets, rep, repcat, tq, name):
    b, l, _ = q3.shape
    in_specs = [pl.BlockSpec((1, tq, ATTN_Q_W), lambda i, t: (i, t, 0))]
    args = [q3]
    for arr in kv_sets:
        nd = arr.ndim
        blk = (1,) + arr.shape[1:]
        in_specs.append(pl.BlockSpec(blk, lambda i, t, nd=nd: (i,) + (0,) * (nd - 1)))
        args.append(arr)
    in_specs += [_const_spec(rep.shape), _const_spec(repcat.shape)]
    args += [rep, repcat]
    return pl.pallas_call(
        functools.partial(_attn_kernel, n_sets=len(kv_sets) // 2),
        grid=(b, l // tq),
        in_specs=in_specs,
        out_specs=pl.BlockSpec((1, tq, ATTN_Q_W), lambda i, t: (i, t, 0)),
        out_shape=jax.ShapeDtypeStruct((b, l, ATTN_Q_W), bf16),
        compiler_params=_cparams(("arbitrary", "arbitrary")),
        name=name,
    )(*args)


def _dn_kernel(*refs, seq_len, has_s0, want_state):
    it = iter(refs)
    dqkv_ref = next(it)
    ba_ref = next(it)
    bar_ref = next(it)
    barp_ref = next(it)
    dgate_ref = next(it)
    s0_ref = next(it) if has_s0 else None
    convw_ref = next(it)
    alog_c = next(it)
    dtb_c = next(it)
    alog_r = next(it)
    dtb_r = next(it)
    alog_p = next(it)
    dtb_p = next(it)
    dnn_ref = next(it)
    ones_ref = next(it)
    expand_ref = next(it)
    o_ref = next(it)
    st_ref = next(it) if want_state else None
    q_s, k_s, v_s, kt_s, gx_s, bx_s, grp_s, gr_s, of_s, ob_s, s_s, cu_s, nw_s, qe_s, el_s = it
    cpi = DN_CHUNKS_PER_ITER

    n = seq_len // CHUNK
    c = CHUNK
    pw = 2 * DN_DK
    n_pairs = DN_HEADS // 2
    r64 = lax.broadcasted_iota(jnp.int32, (c, c), 0)
    c64 = lax.broadcasted_iota(jnp.int32, (c, c), 1)
    tril = jnp.where(r64 >= c64, 1.0, 0.0).astype(bf16)
    triu = jnp.where(r64 <= c64, 1.0, 0.0).astype(bf16)
    rp = lax.broadcasted_iota(jnp.int32, (c, pw), 0)
    cp = lax.broadcasted_iota(jnp.int32, (c, pw), 1) % c
    lower_p = rp >= cp
    upper_p = rp <= cp
    eye_p = jnp.where(rp == cp, 1.0, 0.0)
    same8 = (rp // 8) == (cp // 8)
    off16 = ((rp // 16) == (cp // 16)) & ~same8
    off32 = ((rp // 32) == (cp // 32)) & ((rp // 16) != (cp // 16))
    off64 = (rp // 32) != (cp // 32)
    rb2 = lax.broadcasted_iota(jnp.int32, (pw, pw), 0)
    cb2 = lax.broadcasted_iota(jnp.int32, (pw, pw), 1)
    bd_mask = (rb2 // c) == (cb2 // c)
    bd_triu = jnp.where(bd_mask & (rb2 % c <= cb2 % c), 1.0, 0.0).astype(bf16)
    bd_tril = jnp.where(bd_mask & (rb2 % c >= cb2 % c), 1.0, 0.0).astype(bf16)
    lane16 = lax.broadcasted_iota(jnp.int32, (1, N_GATES), 1)
    sub16 = lax.broadcasted_iota(jnp.int32, (N_GATES, 1), 0)
    sub8 = lax.broadcasted_iota(jnp.int32, (2 * n_pairs, 1), 0)
    sub_pw = lax.broadcasted_iota(jnp.int32, (pw, 1), 0)
    row_c = lax.broadcasted_iota(jnp.int32, (c, 1), 0)

    def bdiag(x):
        xb = x.astype(bf16)
        return jnp.where(bd_mask, jnp.concatenate([xb, xb], axis=0), jnp.zeros((), bf16))

    def conv_silu(ci):
        r0 = pl.multiple_of(ci * c, c)
        xb = dqkv_ref[0, pl.ds(r0, c), :].astype(f32)
        p0 = pl.multiple_of(jnp.maximum(r0 - 16, 0), 16)
        n0 = pl.multiple_of(jnp.minimum(r0 + c, seq_len - 16), 16)
        prev = dqkv_ref[0, pl.ds(p0, 16), :].astype(f32)[15:16, :]
        nxt = dqkv_ref[0, pl.ds(n0, 16), :].astype(f32)[0:1, :]
        prev = jnp.where(ci > 0, prev, 0.0)
        nxt = jnp.where(ci < n - 1, nxt, 0.0)
        xm = jnp.where(row_c == 0, prev, pltpu.roll(xb, 1, 0))
        xp = jnp.where(row_c == c - 1, nxt, pltpu.roll(xb, c - 1, 0))
        w = convw_ref[...]
        return _silu(xm * w[0:1, :] + xb * w[1:2, :] + xp * w[2:3, :])

    def gate_col(ci):
        ba = ba_ref[0, pl.ds(pl.multiple_of(ci * c, c), c), :]
        beta = jax.nn.sigmoid(ba[:, 0:N_GATES])
        g = -jnp.exp(alog_c[...]) * _softplus(ba[:, N_GATES:2 * N_GATES] + dtb_c[...])
        return beta, _split3(g)

    def pre_phase_a(it):
        cis = [it * cpi + s for s in range(cpi)]
        act = [conv_silu(ci) for ci in cis]
        gates = [gate_col(ci) for ci in cis]
        g_r = [_split3(-jnp.exp(alog_r[...]) * _softplus(bar_ref[0, ci][N_GATES:2 * N_GATES, :] + dtb_r[...]))
               for ci in cis]
        g_p = [_split3(-jnp.exp(alog_p[...]) * _softplus(barp_ref[0, ci] + dtb_p[...])) for ci in cis]
        sumsq = _group_sumsq_many([a[:, 0:2 * DN_W] for a in act], ones_ref[...])
        fwd = [sum(_dot(tril, p) for p in gsp) for _, gsp in gates]
        bwd = [sum(_dot(triu, p) for p in gsp) for _, gsp in gates]
        gr = [jnp.where(sub16 < DN_HEADS, sum(_dot(p, triu) for p in parts), sum(_dot(p, tril) for p in parts))
              for parts in g_r]
        grp = [jnp.where(sub8 < n_pairs, sum(_dot(p, bd_triu) for p in parts), sum(_dot(p, bd_tril) for p in parts))
               for parts in g_p]
        gcum = [jnp.where(lane16 < DN_HEADS, f, b) for f, b in zip(fwd, bwd)]
        return act, sumsq, [beta for beta, _ in gates], gcum, gr, grp

    def pre_phase_b(pa):
        _, _, betas, gcum, _, _ = pa
        expand = expand_ref[...]
        gx = [sum(_dot(p, expand) for p in _split3(gc_)) for gc_ in gcum]
        bx = [sum(_dot(p, expand) for p in _split2(beta)) for beta in betas]
        return gx, bx

    def pre_phase_c(pa, pb, slot):
        act, sumsq, _, _, gr, grp = pa
        gx, bx = pb
        for s in range(cpi):
            rs = slice(s * c, (s + 1) * c)
            a = act[s]
            scale = lax.rsqrt(sumsq[s] + EPS)
            q_s[slot, rs, :] = a[:, 0:DN_W] * scale[:, 0:DN_W] * (DN_DK ** -0.5)
            kn = a[:, DN_W:2 * DN_W] * scale[:, DN_W:2 * DN_W]
            k_s[slot, rs, :] = kn
            v_s[slot, rs, :] = a[:, 2 * DN_W:3 * DN_W]
            kt_s[slot, s] = jnp.transpose(jnp.concatenate([kn, kn], axis=0)).astype(bf16)
            gx_s[slot, rs, :] = gx[s]
            bx_s[slot, rs, :] = bx[s]
            gr_s[slot, s] = gr[s]
            grp_s[slot, s] = grp[s]

    zero_blk = jnp.zeros((DN_DK, DN_DK), f32)
    for d in range(2):
        for p in range(n_pairs):
            if has_s0:
                top = jnp.concatenate([s0_ref[0, 0, d, 2 * p], zero_blk], axis=1)
                bot = jnp.concatenate([zero_blk, s0_ref[0, 0, d, 2 * p + 1]], axis=1)
                s_s[d * n_pairs + p] = jnp.concatenate([top, bot], axis=0)
            else:
                s_s[d * n_pairs + p] = jnp.zeros((pw, pw), f32)

    def each(fn, *lists):
        return [fn(*vals) for vals in zip(*lists)]

    def fused_half(it, slot, it_next, slot_next):
        pa = pre_phase_a(it_next) if it_next is not None else None
        qs, ks, vs, bx, eg, decay, strict, ktd, kq, where_to = [], [], [], [], [], [], [], [], [], []
        for sub in range(cpi):
            ci = it * cpi + sub
            r0 = pl.multiple_of(ci * c, c)
            rs = slice(sub * c, (sub + 1) * c)
            grp = grp_s[slot, sub]
            gr16 = gr_s[slot, sub]
            for p in range(n_pairs):
                sl = slice(p * pw, (p + 1) * pw)
                q_p = q_s[slot, rs, sl]
                k_p = k_s[slot, rs, sl]
                v_p = v_s[slot, rs, sl]
                kt = kt_s[slot, sub, p * pw:(p + 1) * pw, :]
                kq_p = _dot(jnp.concatenate([k_p, q_p], axis=0).astype(bf16),
                            jnp.where(bd_mask, kt, jnp.zeros((), bf16)))
                for d in range(2):
                    pp = d * n_pairs + p
                    j0 = d * DN_HEADS + 2 * p
                    xl = slice(pp * pw, (pp + 1) * pw)
                    g_end = gr16[:, c - 1:c] if d == 0 else gr16[:, 0:1]
                    e_rest_t = jnp.exp(g_end - gr16)
                    incl = lower_p if d == 0 else upper_p
                    gx = gx_s[slot, rs, xl]
                    g_last = gx[c - 1:c, :] if d == 0 else gx[0:1, :]
                    el_s[ci, pp:pp + 1, :] = jnp.exp(g_last)
                    qs.append(q_p)
                    ks.append(k_p)
                    vs.append(v_p)
                    kq.append(kq_p)
                    bx.append(bx_s[slot, rs, xl])
                    eg.append(jnp.exp(gx))
                    diff = gx - grp[pp:pp + 1, :]
                    decay.append(jnp.where(incl, jnp.exp(jnp.where(incl, diff, 0.0)), 0.0))
                    strict.append((rp > cp) if d == 0 else (rp < cp))
                    fac = jnp.where(sub_pw < c, e_rest_t[j0:j0 + 1, :], e_rest_t[j0 + 1:j0 + 2, :])
                    ktd.append((kt[:, 0:c].astype(f32) * fac).astype(bf16))
                    where_to.append((ci, pp, r0, sl, d))

        pb = pre_phase_b(pa) if pa is not None else None
        lmat = each(lambda m, x, b, dc: jnp.where(m, x[0:c] * b * dc, 0.0), strict, kq, bx, decay)
        amat = each(lambda x, dc: (x[c:2 * c] * dc).astype(bf16), kq, decay)
        d1 = each(lambda l: jnp.where(same8, l, 0.0), lmat)
        d2 = each(lambda a: _dot(a.astype(bf16), bdiag(a)), d1)
        tmat = each(lambda a: eye_p - a, d1)
        tmat = each(lambda t, a: t + _dot(a.astype(bf16), bdiag(t)), tmat, d2)
        d4 = each(lambda a: _dot(a.astype(bf16), bdiag(a)).astype(bf16), d2)
        tmat = each(lambda t, a: t + _dot(a, bdiag(t)), tmat, d4)
        for off_mask in (off16, off32, off64):
            tbd = each(bdiag, tmat)
            tl = each(lambda t, l: _dot(t.astype(bf16), bdiag(jnp.where(off_mask, l, 0.0))).astype(bf16), tmat, lmat)
            tmat = each(lambda t, a, b: t - _dot(a, b), tmat, tl, tbd)
        uw = each(lambda t, v, k, b, e: _dot(t.astype(bf16), jnp.concatenate([bdiag(v * b), bdiag(k * (b * e))], axis=1)),
                  tmat, vs, ks, bx, eg)
        ku_kw = each(lambda kd, y: _dot(kd, y.astype(bf16)), ktd, uw)
        au_aw = each(lambda a, y: _dot(a, jnp.concatenate([bdiag(y[:, 0:pw]), bdiag(y[:, pw:2 * pw])], axis=1)),
                     amat, uw)
        if pa is not None:
            pre_phase_c(pa, pb, slot_next)
        for (ci, pp, r0, sl, d), q_p, e, kk, aa in zip(where_to, qs, eg, ku_kw, au_aw):
            cu_s[ci, pp] = jnp.where(bd_mask, kk[:, 0:pw], 0.0)
            nw_s[ci, pp] = jnp.where(bd_mask, -kk[:, pw:2 * pw], 0.0).astype(bf16)
            qe_s[ci, pp] = (q_p * e - aa[:, pw:2 * pw]).astype(bf16)
            dst = of_s if d == 0 else ob_s
            dst[pl.ds(r0, c), sl] = aa[:, 0:pw]

    n_it = n // cpi
    first = pre_phase_a(0)
    pre_phase_c(first, pre_phase_b(first), 0)

    def pipe_body(j, carry):
        fused_half(2 * j, 0, 2 * j + 1, 1)
        fused_half(2 * j + 1, 1, 2 * j + 2, 0)
        return carry

    if n_it // 2 - 1 > 0:
        lax.fori_loop(0, n_it // 2 - 1, pipe_body, 0)
    fused_half(n_it - 2, 0, n_it - 1, 1)
    fused_half(n_it - 1, 1, None, None)

    def scan_body(i, carry):
        cis = [i, n - 1 - i]
        pairs = [(d, p) for d in range(2) for p in range(n_pairs)]
        s_old = [s_s[d * n_pairs + p] for d, p in pairs]
        s_b = [s.astype(bf16) for s in s_old]
        o_add = [_dot(qe_s[cis[d], d * n_pairs + p], sb) for (d, p), sb in zip(pairs, s_b)]
        s_upd = [_dot(nw_s[cis[d], d * n_pairs + p], sb) for (d, p), sb in zip(pairs, s_b)]
        for (d, p), s, oa, su in zip(pairs, s_old, o_add, s_upd):
            pp = d * n_pairs + p
            r0 = pl.multiple_of(cis[d] * c, c)
            dst = of_s if d == 0 else ob_s
            dst[pl.ds(r0, c), p * pw:(p + 1) * pw] += oa
            s_s[pp] = s * el_s[cis[d], pp:pp + 1, :] + su + cu_s[cis[d], pp]
        return carry

    lax.fori_loop(0, n, scan_body, 0)

    rb = 128

    def post_body(bi, carry):
        r0 = pl.multiple_of(bi * rb, rb)
        o = of_s[pl.ds(r0, rb), :] + ob_s[pl.ds(r0, rb), :]
        o = o * lax.rsqrt(_group_sumsq(o, ones_ref[...]) * (1.0 / DN_DK) + EPS) * dnn_ref[...]
        gate = dgate_ref[0, pl.ds(r0, rb), :].astype(f32)
        o_ref[0, pl.ds(r0, rb), :] = (o * _silu(gate)).astype(o_ref.dtype)
        return carry

    lax.fori_loop(0, seq_len // rb, post_body, 0)

    if want_state:
        for d in range(2):
            for p in range(n_pairs):
                s = s_s[d * n_pairs + p]
                st_ref[0, 0, d, 2 * p] = s[0:DN_DK, 0:DN_DK]
                st_ref[0, 0, d, 2 * p + 1] = s[DN_DK:pw, DN_DK:pw]


def _dn_call(dqkv3, ba3, bar4, barp4, dgate3, s0, consts, want_state, name):
    b, l, _ = dqkv3.shape
    n = l // CHUNK
    n_pairs = DN_HEADS // 2
    pw = 2 * DN_DK
    state_blk = (1, 1, 2, DN_HEADS, DN_DK, DN_DK)
    state_spec = pl.BlockSpec(state_blk, lambda i: (i, 0, 0, 0, 0, 0))
    in_specs = [pl.BlockSpec((1, l, 3 * DN_W), lambda i: (i, 0, 0)),
                pl.BlockSpec((1, l, 2 * N_GATES), lambda i: (i, 0, 0)),
                pl.BlockSpec((1, n, 2 * N_GATES, CHUNK), lambda i: (i, 0, 0, 0)),
                pl.BlockSpec((1, n, 2 * n_pairs, pw), lambda i: (i, 0, 0, 0)),
                pl.BlockSpec((1, l, DN_W), lambda i: (i, 0, 0))]
    args = [dqkv3, ba3, bar4, barp4, dgate3]
    if s0 is not None:
        in_specs.append(state_spec)
        args.append(s0)
    in_specs += [_const_spec(a.shape) for a in consts]
    args += consts
    out_shape = [jax.ShapeDtypeStruct((b, l, DN_W), bf16)]
    out_specs = [pl.BlockSpec((1, l, DN_W), lambda i: (i, 0, 0))]
    if want_state:
        out_shape.append(jax.ShapeDtypeStruct((b,) + state_blk[1:], f32))
        out_specs.append(state_spec)
    cpi = DN_CHUNKS_PER_ITER
    rows = cpi * CHUNK
    scratch = [pltpu.VMEM((2, rows, DN_W), f32), pltpu.VMEM((2, rows, DN_W), f32),
               pltpu.VMEM((2, rows, DN_W), f32),
               pltpu.VMEM((2, cpi, DN_W, pw), bf16),
               pltpu.VMEM((2, rows, N_GATES * DN_DK), f32), pltpu.VMEM((2, rows, N_GATES * DN_DK), f32),
               pltpu.VMEM((2, cpi, 2 * n_pairs, pw), f32), pltpu.VMEM((2, cpi, N_GATES, CHUNK), f32),
               pltpu.VMEM((l, DN_W), f32), pltpu.VMEM((l, DN_W), f32),
               pltpu.VMEM((2 * n_pairs, pw, pw), f32),
               pltpu.VMEM((n, 2 * n_pairs, pw, pw), f32),
               pltpu.VMEM((n, 2 * n_pairs, pw, pw), bf16),
               pltpu.VMEM((n, 2 * n_pairs, CHUNK, pw), bf16),
               pltpu.VMEM((n, 2 * n_pairs, pw), f32)]
    res = pl.pallas_call(
        functools.partial(_dn_kernel, seq_len=l, has_s0=s0 is not None, want_state=want_state),
        grid=(b,),
        in_specs=in_specs,
        out_specs=tuple(out_specs),
        out_shape=tuple(out_shape),
        scratch_shapes=scratch,
        compiler_params=_cparams(("arbitrary",)),
        name=name,
    )(*args)
    return res if want_state else (res[0], None)


def _merge_kernel(oa_ref, od_ref, sga_ref, sgd_ref, x_ref, mod_ref, wa_ref, wd_ref, wo_ref, nffn_ref,
                  wr_ref, br_ref, x1_out, h2_out, comb_out):
    mod = mod_ref[0]
    gate1 = mod[:, 2 * D_MODEL:3 * D_MODEL]
    shift2 = mod[:, 3 * D_MODEL:4 * D_MODEL]
    scale2 = mod[:, 4 * D_MODEL:5 * D_MODEL]
    merged = (sga_ref[...].astype(f32) * _dot(oa_ref[...], wa_ref[...])
              + sgd_ref[...].astype(f32) * _dot(od_ref[...], wd_ref[...]))
    m = _dot(merged.astype(bf16), wo_ref[...])
    x1 = x_ref[...] + gate1 * m
    x1_out[...] = x1
    ms = jnp.mean(x1 * x1, axis=-1, keepdims=True)
    h2 = x1 * lax.rsqrt(ms + EPS) * nffn_ref[...]
    h2 = h2 * (1.0 + scale2) + shift2
    h2_out[...] = h2.astype(bf16)

    h_hi, h_lo = _split2(h2)
    w_hi, w_lo = _split2(wr_ref[...])
    logits = _dot(h_hi, w_hi) + _dot(h_hi, w_lo) + _dot(h_lo, w_hi) + br_ref[...]
    lane = lax.broadcasted_iota(jnp.int32, logits.shape, 1)
    big = ROUTER_LANES
    neg = -jnp.inf

    def first_lane(hit):
        return jnp.min(jnp.where(hit, lane, big), axis=-1, keepdims=True)

    is_g = lane < N_GROUPS
    gl = jnp.where(is_g, logits, neg)
    gexp = jnp.where(is_g, jnp.exp(gl - gl.max(axis=-1, keepdims=True)), 0.0)
    gp = gexp / gexp.sum(axis=-1, keepdims=True)
    g_top = gp.max(axis=-1, keepdims=True)
    g_idx = first_lane(is_g & (gp == g_top))
    lo_lane = N_GROUPS + g_idx * EXPERTS_PER_GROUP
    is_e = (lane >= lo_lane) & (lane < lo_lane + EXPERTS_PER_GROUP)
    el = jnp.where(is_e, logits, neg)
    eexp = jnp.where(is_e, jnp.exp(el - el.max(axis=-1, keepdims=True)), 0.0)
    ep = jnp.where(is_e, eexp / eexp.sum(axis=-1, keepdims=True), -1.0)
    p1 = ep.max(axis=-1, keepdims=True)
    i1 = first_lane(ep == p1)
    ep2 = jnp.where(lane == i1, -1.0, ep)
    p2 = ep2.max(axis=-1, keepdims=True)
    i2 = first_lane(ep2 == p2)
    tot = p1 + p2
    comb_out[...] = (jnp.where(lane == i1, p1 / tot * g_top, 0.0)
                     + jnp.where(lane == i2, p2 / tot * g_top, 0.0))


def _merge_call(oa, od, sga, sgd, x2d, mod3, mod_row, wa, wd, wo, nffn, wr, br, seq_len, name):
    t = x2d.shape[0]
    tm = 512
    tiles_per_seq = max(seq_len // tm, 1)

    def row_spec(w):
        return pl.BlockSpec((tm, w), lambda i: (i, 0))

    return pl.pallas_call(
        _merge_kernel,
        grid=(t // tm,),
        in_specs=[row_spec(ATTN_Q_W), row_spec(DN_W), row_spec(D_MODEL), row_spec(D_MODEL), row_spec(D_MODEL),
                  pl.BlockSpec((1, 1, N_MOD * D_MODEL), lambda i: (mod_row(i // tiles_per_seq), 0, 0)),
                  _const_spec(wa.shape), _const_spec(wd.shape), _const_spec(wo.shape),
                  _const_spec((1, D_MODEL)), _const_spec(wr.shape), _const_spec(br.shape)],
        out_specs=(row_spec(D_MODEL), row_spec(D_MODEL), row_spec(ROUTER_LANES)),
        out_shape=(jax.ShapeDtypeStruct((t, D_MODEL), f32),
                   jax.ShapeDtypeStruct((t, D_MODEL), bf16),
                   jax.ShapeDtypeStruct((t, ROUTER_LANES), f32)),
        compiler_params=_cparams(("arbitrary",)),
        name=name,
    )(oa, od, sga, sgd, x2d, mod3, wa, wd, wo, nffn, wr, br)


def _moe_kernel(h2_ref, comb_ref, x1_ref, mod_ref, wg_ref, wu_ref, wdn_ref, nfin_ref, y_ref, acc_ref):
    g = pl.program_id(1)

    @pl.when(g == 0)
    def _():
        acc_ref[...] = jnp.zeros_like(acc_ref)

    h2 = h2_ref[...]
    comb = comb_ref[...]
    lane = lax.broadcasted_iota(jnp.int32, comb.shape, 1)
    first = N_GROUPS + g * EXPERTS_PER_GROUP
    parts = []
    for e in range(EXPERTS_PER_GROUP):
        hid = _silu(_dot(h2, wg_ref[e])) * _dot(h2, wu_ref[e])
        ce = jnp.sum(jnp.where(lane == first + e, comb, 0.0), axis=-1, keepdims=True)
        parts.append((hid * ce).astype(bf16))
    acc_ref[...] += _dot(jnp.concatenate(parts, axis=1), wdn_ref[0])

    @pl.when(g == N_GROUPS - 1)
    def _():
        gate2 = mod_ref[0][:, 5 * D_MODEL:6 * D_MODEL]
        x2 = x1_ref[...] + gate2 * acc_ref[...]
        ms = jnp.mean(x2 * x2, axis=-1, keepdims=True)
        y_ref[...] = x2 * lax.rsqrt(ms + EPS) * nfin_ref[...]


def _moe_call(h2, comb, x1, mod3, mod_row, wg, wu, wdn, nfin, seq_len, name):
    t = h2.shape[0]
    tm = 1024
    tiles_per_seq = max(seq_len // tm, 1)
    return pl.pallas_call(
        _moe_kernel,
        grid=(t // tm, N_GROUPS),
        in_specs=[pl.BlockSpec((tm, D_MODEL), lambda i, e: (i, 0)),
                  pl.BlockSpec((tm, ROUTER_LANES), lambda i, e: (i, 0)),
                  pl.BlockSpec((tm, D_MODEL), lambda i, e: (i, 0)),
                  pl.BlockSpec((1, 1, N_MOD * D_MODEL), lambda i, e: (mod_row(i // tiles_per_seq), 0, 0)),
                  pl.BlockSpec((EXPERTS_PER_GROUP, D_MODEL, D_FF_EXPERT), lambda i, e: (e, 0, 0)),
                  pl.BlockSpec((EXPERTS_PER_GROUP, D_MODEL, D_FF_EXPERT), lambda i, e: (e, 0, 0)),
                  pl.BlockSpec((1, EXPERTS_PER_GROUP * D_FF_EXPERT, D_MODEL), lambda i, e: (e, 0, 0)),
                  pl.BlockSpec((1, D_MODEL), lambda i, e: (0, 0))],
        out_specs=pl.BlockSpec((tm, D_MODEL), lambda i, e: (i, 0)),
        out_shape=jax.ShapeDtypeStruct((t, D_MODEL), f32),
        scratch_shapes=[pltpu.VMEM((tm, D_MODEL), f32)],
        compiler_params=_cparams(("arbitrary", "arbitrary")),
        name=name,
    )(h2, comb, x1, mod3, wg, wu, wdn, nfin)


def _rope_tables(seq_len):
    t = np.arange(seq_len)
    row = (t // GRID_W).astype(np.float64)
    col = (t % GRID_W).astype(np.float64)
    half = HEAD_DIM // 2
    inv = np.power(ROPE_THETA, -np.arange(0, half, 2, dtype=np.float64) / half)
    d = np.arange(HEAD_DIM)
    freq = inv[d % (half // 2)]
    pos = np.where(d[None, :] < half, row[:, None], col[:, None])
    ang = pos * freq[None, :]
    sign = np.where((d % half) < half // 2, -1.0, 1.0)
    cos = np.tile(np.cos(ang), (1, ATTN_HEADS)).astype(np.float32)
    sin = np.tile(np.sin(ang) * sign[None, :], (1, ATTN_HEADS)).astype(np.float32)
    return jnp.asarray(cos), jnp.asarray(sin)


def _ones_block_diag():
    i = np.arange(LANE_TILE)
    return jnp.asarray((i[:, None] // HEAD_DIM == i[None, :] // HEAD_DIM).astype(np.float32), dtype=bf16)


def _lane_replicator():
    i = np.arange(HEAD_DIM)
    j = np.arange(ATTN_GROUP * HEAD_DIM)
    return jnp.asarray((i[:, None] == j[None, :] % HEAD_DIM).astype(np.float32), dtype=bf16)


def _lane_replicator_cat():
    i = np.arange(ATTN_KV_W)
    j = np.arange(ATTN_GROUP * HEAD_DIM)
    mats = [(i[:, None] == kv * HEAD_DIM + j[None, :] % HEAD_DIM).astype(np.float32) for kv in range(ATTN_KV_HEADS)]
    return jnp.asarray(np.stack(mats), dtype=bf16)


def _trunk(x3, mod3, mod_row, wts, prefix):
    n_seq, seq_len, _ = x3.shape
    t = n_seq * seq_len
    x2d = x3.reshape(t, D_MODEL)
    is_lat = prefix is not None
    rope_tabs = _rope_tables(seq_len) if is_lat else None

    res = _proj_call(
        x2d, mod3, mod_row, wts["nmix"], wts["w_qkv"], wts["w_gates"], wts["wba"], wts["qn"],
        wts["kn"], wts["ones_bd"], rope_tabs, n_seq, seq_len)
    q, kcat, vcat = res[:3]
    k, v = (None, None) if is_lat else (jnp.swapaxes(res[3], 3, 4), jnp.swapaxes(res[4], 3, 4))
    dqkv, ba, bat, dgate, sga, sgd = res[-6:]

    q3 = q.reshape(n_seq, seq_len, ATTN_Q_W)
    kcat3 = kcat.reshape(n_seq, seq_len, ATTN_KV_W)
    vcat3 = vcat.reshape(n_seq, seq_len, ATTN_KV_W)
    if is_lat:
        pk, pv, s0 = prefix
        o_attn = _attn_call(q3, [pk, pv, kcat3, vcat3], wts["rep"], wts["repcat"], 256, "attn_lat")
    else:
        s0 = None
        o_attn = _attn_call(q3, [kcat3, vcat3], wts["rep"], wts["repcat"], seq_len, "attn_ctx")

    n_chunks = seq_len // CHUNK
    bar4 = bat.reshape(2 * N_GATES, n_seq, n_chunks, CHUNK).transpose(1, 2, 0, 3)
    barp4 = bat[N_GATES:].reshape(N_GATES // 2, 2, n_seq, n_chunks, CHUNK).transpose(2, 3, 0, 1, 4).reshape(
        n_seq, n_chunks, N_GATES // 2, 2 * CHUNK)
    dn_consts = [wts[name] for name in ("convw", "alog_c", "dtb_c", "alog_r", "dtb_r", "alog_p", "dtb_p", "dnn",
                                        "ones_bd", "expand")]
    o_dn, state = _dn_call(
        dqkv.reshape(n_seq, seq_len, 3 * DN_W), ba.reshape(n_seq, seq_len, 2 * N_GATES), bar4, barp4,
        dgate.reshape(n_seq, seq_len, DN_W), s0, dn_consts, not is_lat, "dn_lat" if is_lat else "dn_ctx")

    x1, h2, comb = _merge_call(
        o_attn.reshape(t, ATTN_Q_W), o_dn.reshape(t, DN_W), sga, sgd, x2d, mod3, mod_row,
        wts["wa"], wts["wd"], wts["wo"], wts["nffn"], wts["wr"], wts["br"], seq_len,
        "merge_lat" if is_lat else "merge_ctx")

    y = _moe_call(h2, comb, x1, mod3, mod_row, wts["wg"], wts["wu"], wts["wdn"], wts["nfin"], seq_len,
                  "moe_lat" if is_lat else "moe_ctx")
    return y.reshape(n_seq, seq_len, D_MODEL), k, v, state


def kernel(x_prompt, x_sample, c, cache_attn_k, cache_attn_v, state_delta, c_ctx, w_mod, b_mod, norm_mix, norm_ffn, norm_final, w_in, q_norm, k_norm, conv_w, a_log, dt_bias, dn_norm, w_attn_br, w_dn_br, w_out, w_rg, b_rg, w_re, b_re, w_gate_e, w_up_e, w_down_e):
    layer = 0
    n_lat = x_sample.shape[0]
    w_in_l = w_in[layer]
    ba_lo = OFF_DGATE
    wts = {
        "nmix": norm_mix[layer][None, :],
        "w_qkv": w_in_l[:, :ba_lo].astype(bf16),
        "w_gates": w_in_l[:, ba_lo + 2 * N_GATES:].astype(bf16),
        "wba": jnp.pad(w_in_l[:, ba_lo:ba_lo + 2 * N_GATES], ((0, 0), (0, LANE_TILE - 2 * N_GATES))).astype(bf16),
        "qn": jnp.tile(q_norm[layer], ATTN_HEADS)[None, :],
        "kn": jnp.tile(k_norm[layer], ATTN_KV_HEADS)[None, :],
        "ones_bd": _ones_block_diag(),
        "rep": _lane_replicator(),
        "repcat": _lane_replicator_cat(),
        "convw": conv_w[layer],
        "alog_c": a_log[layer].reshape(1, N_GATES),
        "dtb_c": dt_bias[layer].reshape(1, N_GATES),
        "alog_r": a_log[layer].reshape(N_GATES, 1),
        "dtb_r": dt_bias[layer].reshape(N_GATES, 1),
        "alog_p": jnp.repeat(a_log[layer].reshape(N_GATES // 2, 2), CHUNK, axis=1),
        "dtb_p": jnp.repeat(dt_bias[layer].reshape(N_GATES // 2, 2), CHUNK, axis=1),
        "expand": jnp.asarray(np.repeat(np.eye(N_GATES, dtype=np.float32), DN_DK, axis=1), dtype=bf16),
        "dnn": jnp.tile(dn_norm[layer], DN_HEADS)[None, :],
        "wa": w_attn_br[layer].astype(bf16),
        "wd": w_dn_br[layer].astype(bf16),
        "wo": w_out[layer].astype(bf16),
        "nffn": norm_ffn[layer][None, :],
        "wr": jnp.concatenate([w_rg[layer], w_re[layer],
                               jnp.zeros((D_MODEL, ROUTER_LANES - N_GROUPS - N_EXPERTS), f32)], axis=1),
        "br": jnp.concatenate([b_rg[layer], b_re[layer],
                               jnp.zeros((ROUTER_LANES - N_GROUPS - N_EXPERTS,), f32)])[None, :],
        "wg": w_gate_e[layer].astype(bf16),
        "wu": w_up_e[layer].astype(bf16),
        "wdn": w_down_e[layer].reshape(N_GROUPS, EXPERTS_PER_GROUP * D_FF_EXPERT, D_MODEL).astype(bf16),
        "nfin": norm_final[None, :],
    }
    cond8 = jnp.concatenate([c_ctx[None, :], c, jnp.zeros((8 - 1 - n_lat, D_MODEL), f32)], axis=0)
    mod3 = _mod_call(cond8, w_mod[layer], b_mod[layer][None, :])[:, None, :]

    y_prompt, new_k, new_v, new_state = _trunk(x_prompt, mod3, lambda s: 0, wts, None)
    y_sample, _, _, _ = _trunk(x_sample, mod3, lambda s: s + 1, wts,
                               (cache_attn_k, cache_attn_v, state_delta))
    return (y_prompt, y_sample, new_k, new_v, new_state)
```

```python
import functools

import numpy as np
import jax
import jax.numpy as jnp
from jax import lax
from jax.experimental import pallas as pl
from jax.experimental.pallas import tpu as pltpu

f32 = jnp.float32
bf16 = jnp.bfloat16

D_MODEL = 1024
HEAD_DIM = 64
ATTN_HEADS = 8
ATTN_KV_HEADS = 2
ATTN_GROUP = ATTN_HEADS // ATTN_KV_HEADS
GRID_W = 64
ROPE_THETA = 10000.0
DN_HEADS = 8
DN_DK = 64
CHUNK = 64
N_GROUPS = 4
EXPERTS_PER_GROUP = 4
N_EXPERTS = 16
D_FF_EXPERT = 256
N_MOD = 6
EPS = 1e-6

ATTN_Q_W = ATTN_HEADS * HEAD_DIM
ATTN_KV_W = ATTN_KV_HEADS * HEAD_DIM
DN_W = DN_HEADS * DN_DK
N_GATES = 2 * DN_HEADS

OFF_Q = 0
OFF_K = ATTN_Q_W
OFF_V = OFF_K + ATTN_KV_W
OFF_DQKV = OFF_V + ATTN_KV_W
OFF_DGATE = OFF_DQKV + 3 * DN_W
OFF_GATTN = OFF_DGATE + DN_W
OFF_GDN = OFF_GATTN + D_MODEL
W_MAIN = OFF_GDN + D_MODEL

ROUTER_LANES = 128
VMEM_LIMIT = 56 * 1024 * 1024

_TRANS_B = (((1,), (1,)), ((), ()))
_TRANS_A = (((0,), (0,)), ((), ()))


def _cparams(sem):
    return pltpu.CompilerParams(dimension_semantics=sem, vmem_limit_bytes=VMEM_LIMIT)


def _dot(a, b):
    return jnp.dot(a, b, preferred_element_type=f32)


def _silu(x):
    return x * jax.nn.sigmoid(x)


def _softplus(x):
    return jnp.maximum(x, 0.0) + jnp.log1p(jnp.exp(-jnp.abs(x)))


def _split2(x):
    hi = x.astype(bf16)
    lo = (x - hi.astype(f32)).astype(bf16)
    return hi, lo


def _split3(x):
    a = x.astype(bf16)
    r = x - a.astype(f32)
    b = r.astype(bf16)
    c = (r - b.astype(f32)).astype(bf16)
    return a, b, c


LANE_TILE = 128
DN_CHUNKS_PER_ITER = 2


def _group_sumsq_many(arrs, ones_pair):
    pieces = []
    for x in arrs:
        hi, lo = _split2(x * x)
        for s in range(x.shape[1] // LANE_TILE):
            pieces.append(hi[:, s * LANE_TILE:(s + 1) * LANE_TILE])
            pieces.append(lo[:, s * LANE_TILE:(s + 1) * LANE_TILE])
    res = _dot(jnp.concatenate(pieces, axis=0), ones_pair)
    out, off = [], 0
    for x in arrs:
        r = x.shape[0]
        slabs = []
        for s in range(x.shape[1] // LANE_TILE):
            slabs.append(res[off:off + r] + res[off + r:off + 2 * r])
            off += 2 * r
        out.append(slabs[0] if len(slabs) == 1 else jnp.concatenate(slabs, axis=1))
    return out


def _group_sumsq(x, ones_pair):
    return _group_sumsq_many([x], ones_pair)[0]


def _const_spec(shape):
    nd = len(shape)
    return pl.BlockSpec(shape, lambda *_: (0,) * nd, pipeline_mode=pl.Buffered(1))


def _mod_kernel(c_ref, w_ref, b_ref, o_ref):
    c = c_ref[...]
    o_ref[...] = _dot(_silu(c).astype(bf16), w_ref[...].astype(bf16)) + b_ref[...]


def _mod_call(cond8, w_mod, b_mod):
    tn = 1536
    n = w_mod.shape[1]
    return pl.pallas_call(
        _mod_kernel,
        grid=(n // tn,),
        in_specs=[_const_spec((8, D_MODEL)),
                  pl.BlockSpec((D_MODEL, tn), lambda j: (0, j)),
                  pl.BlockSpec((1, tn), lambda j: (0, j))],
        out_specs=pl.BlockSpec((8, tn), lambda j: (0, j)),
        out_shape=jax.ShapeDtypeStruct((8, n), f32),
        compiler_params=_cparams(("arbitrary",)),
        name="mod",
    )(cond8, w_mod, b_mod)


def _rope(x, cos, sin):
    w = x.shape[-1]
    lane = lax.broadcasted_iota(jnp.int32, x.shape, 1)
    first = (lane % 32) < 16
    swapped = jnp.where(first, pltpu.roll(x, w - 16, 1), pltpu.roll(x, 16, 1))
    return x * cos + swapped * sin


def _proj_kernel(*refs, rope, seq_per_tile):
    (x_ref, mod_ref, nmix_ref, wa_ref, wb_ref, wba_ref, qn_ref, kn_ref, ones_ref) = refs[:9]
    pos = 9
    if rope:
        cos_ref, sin_ref = refs[9:11]
        pos = 11
    outs = list(refs[pos:])
    q_out, kcat_out, vcat_out = outs[:3]
    if not rope:
        k_out, v_out = outs[3:5]
        outs = outs[2:]
    (dqkv_out, ba_out, bat_out, dgate_out, sga_out, sgd_out) = outs[3:]

    x = x_ref[...]
    mod = mod_ref[0]
    shift1 = mod[:, 0:D_MODEL]
    scale1 = mod[:, D_MODEL:2 * D_MODEL]
    ms = jnp.mean(x * x, axis=-1, keepdims=True)
    h = x * lax.rsqrt(ms + EPS) * nmix_ref[...]
    h = h * (1.0 + scale1) + shift1
    hb = h.astype(bf16)

    aq = _dot(hb, wa_ref[:, OFF_Q:OFF_Q + ATTN_Q_W])
    ak = _dot(hb, wa_ref[:, OFF_K:OFF_K + ATTN_KV_W])
    av = _dot(hb, wa_ref[:, OFF_V:OFF_V + ATTN_KV_W])
    ssq, ssk = _group_sumsq_many([aq, ak], ones_ref[...])
    aq = aq * lax.rsqrt(ssq * (1.0 / HEAD_DIM) + EPS) * qn_ref[...]
    ak = ak * lax.rsqrt(ssk * (1.0 / HEAD_DIM) + EPS) * kn_ref[...]
    if not rope:
        tm = x.shape[0]
        seq = tm // seq_per_tile
        akt = jnp.transpose(ak)
        avt = jnp.transpose(av)
        for s in range(seq_per_tile):
            for kv in range(ATTN_KV_HEADS):
                k_out[s, 0, kv] = akt[kv * HEAD_DIM:(kv + 1) * HEAD_DIM, s * seq:(s + 1) * seq]
                v_out[s, 0, kv] = avt[kv * HEAD_DIM:(kv + 1) * HEAD_DIM, s * seq:(s + 1) * seq]
    else:
        aq = _rope(aq, cos_ref[...], sin_ref[...])
        ak = _rope(ak, cos_ref[:, 0:ATTN_KV_W], sin_ref[:, 0:ATTN_KV_W])
    q_out[...] = (aq * (HEAD_DIM ** -0.5)).astype(bf16)
    kcat_out[...] = ak.astype(bf16)
    vcat_out[...] = av.astype(bf16)

    dqkv_out[...] = _dot(hb, wa_ref[:, OFF_DQKV:OFF_DGATE]).astype(bf16)
    ba = _dot(hb, wba_ref[...])
    ba_out[...] = ba
    bat_out[...] = jnp.transpose(ba)[0:2 * N_GATES, :]
    dgate_out[...] = _dot(hb, wb_ref[:, 0:DN_W]).astype(bf16)
    sga_out[...] = jax.nn.sigmoid(_dot(hb, wb_ref[:, DN_W:DN_W + D_MODEL])).astype(bf16)
    sgd_out[...] = jax.nn.sigmoid(_dot(hb, wb_ref[:, DN_W + D_MODEL:DN_W + 2 * D_MODEL])).astype(bf16)


def _proj_call(x2d, mod3, mod_row, nmix, wa, wb, wba, qn, kn, ones_bd, rope_tabs, n_seq, seq_len):
    t = x2d.shape[0]
    tm = 512
    rope = rope_tabs is not None
    tiles_per_seq = max(seq_len // tm, 1)
    seq_per_tile = max(tm // seq_len, 1)
    in_specs = [pl.BlockSpec((tm, D_MODEL), lambda i: (i, 0)),
                pl.BlockSpec((1, 1, N_MOD * D_MODEL), lambda i: (mod_row(i // tiles_per_seq), 0, 0)),
                _const_spec((1, D_MODEL)),
                _const_spec(wa.shape),
                _const_spec(wb.shape),
                _const_spec((D_MODEL, LANE_TILE)),
                _const_spec((1, ATTN_Q_W)),
                _const_spec((1, ATTN_KV_W)),
                _const_spec((LANE_TILE, LANE_TILE))]
    args = [x2d, mod3, nmix, wa, wb, wba, qn, kn, ones_bd]

    def row_spec(w):
        return pl.BlockSpec((tm, w), lambda i: (i, 0))

    out_shape = [jax.ShapeDtypeStruct((t, ATTN_Q_W), bf16),
                 jax.ShapeDtypeStruct((t, ATTN_KV_W), bf16),
                 jax.ShapeDtypeStruct((t, ATTN_KV_W), bf16)]
    out_specs = [row_spec(ATTN_Q_W), row_spec(ATTN_KV_W), row_spec(ATTN_KV_W)]
    if rope:
        in_specs += [pl.BlockSpec((tm, ATTN_Q_W), lambda i: (i % tiles_per_seq, 0))] * 2
        args += list(rope_tabs)
    else:
        kv_shape = jax.ShapeDtypeStruct((n_seq, 1, ATTN_KV_HEADS, HEAD_DIM, seq_len), f32)
        kv_spec = pl.BlockSpec((seq_per_tile, 1, ATTN_KV_HEADS, HEAD_DIM, seq_len), lambda i: (i, 0, 0, 0, 0))
        out_shape += [kv_shape, kv_shape]
        out_specs += [kv_spec, kv_spec]
    out_shape += [jax.ShapeDtypeStruct((t, 3 * DN_W), bf16),
                  jax.ShapeDtypeStruct((t, LANE_TILE), f32),
                  jax.ShapeDtypeStruct((2 * N_GATES, t), f32),
                  jax.ShapeDtypeStruct((t, DN_W), bf16),
                  jax.ShapeDtypeStruct((t, D_MODEL), bf16),
                  jax.ShapeDtypeStruct((t, D_MODEL), bf16)]
    out_specs += [row_spec(3 * DN_W), row_spec(LANE_TILE),
                  pl.BlockSpec((2 * N_GATES, tm), lambda i: (0, i)),
                  row_spec(DN_W), row_spec(D_MODEL), row_spec(D_MODEL)]
    return pl.pallas_call(
        functools.partial(_proj_kernel, rope=rope, seq_per_tile=seq_per_tile),
        grid=(t // tm,),
        in_specs=in_specs,
        out_specs=tuple(out_specs),
        out_shape=tuple(out_shape),
        compiler_params=_cparams(("arbitrary",)),
        name="proj_lat" if rope else "proj_ctx",
    )(*args)


def _attn_kernel(*refs, n_sets):
    q_ref = refs[0]
    kv_refs = refs[1:1 + 2 * n_sets]
    rep_ref, repcat_ref = refs[1 + 2 * n_sets:3 + 2 * n_sets]
    o_ref = refs[-1]
    width = ATTN_GROUP * HEAD_DIM
    blk = lax.broadcasted_iota(jnp.int32, (1, width), 1) // HEAD_DIM

    def head_slab(ref, kv):
        if len(ref.shape) == 5:
            return _dot(ref[0, 0, kv].astype(bf16), rep_ref[...])
        return _dot(ref[0], repcat_ref[kv])

    for kv in range(ATTN_KV_HEADS):
        q = q_ref[0, :, kv * width:(kv + 1) * width]
        k4 = [head_slab(kv_refs[2 * s], kv) for s in range(n_sets)]
        v4 = [head_slab(kv_refs[2 * s + 1], kv) for s in range(n_sets)]
        acc = jnp.zeros((q.shape[0], width), f32)
        for g in range(ATTN_GROUP):
            sel = blk == g
            scores = [lax.dot_general(q, jnp.where(sel, k, 0.0).astype(bf16), _TRANS_B,
                                      preferred_element_type=f32) for k in k4]
            m = scores[0].max(axis=-1, keepdims=True)
            for s in scores[1:]:
                m = jnp.maximum(m, s.max(axis=-1, keepdims=True))
            probs = [jnp.exp(s - m) for s in scores]
            denom = probs[0].sum(axis=-1, keepdims=True)
            for p in probs[1:]:
                denom = denom + p.sum(axis=-1, keepdims=True)
            og = _dot(probs[0].astype(bf16), jnp.where(sel, v4[0], 0.0).astype(bf16))
            for p, v in zip(probs[1:], v4[1:]):
                og = og + _dot(p.astype(bf16), jnp.where(sel, v, 0.0).astype(bf16))
            acc = acc + og * (1.0 / denom)
        o_ref[0, :, kv * width:(kv + 1) * width] = acc.astype(o_ref.dtype)


def _attn_call(q3, kv_sets, rep, repcat, tq, name):
    b, l, _ = q3.shape
    in_specs = [pl.BlockSpec((1, tq, ATTN_Q_W), lambda i, t: (i, t, 0))]
    args = [q3]
    for arr in kv_sets:
        nd = arr.ndim
        blk = (1,) + arr.shape[1:]
        in_specs.append(pl.BlockSpec(blk, lambda i, t, nd=nd: (i,) + (0,) * (nd - 1)))
        args.append(arr)
    in_specs += [_const_spec(rep.shape), _const_spec(repcat.shape)]
    args += [rep, repcat]
    return pl.pallas_call(
        functools.partial(_attn_kernel, n_sets=len(kv_sets) // 2),
        grid=(b, l // tq),
        in_specs=in_specs,
        out_specs=pl.BlockSpec((1, tq, ATTN_Q_W), lambda i, t: (i, t, 0)),
        out_shape=jax.ShapeDtypeStruct((b, l, ATTN_Q_W), bf16),
        compiler_params=_cparams(("arbitrary", "arbitrary")),
        name=name,
    )(*args)


def _dn_kernel(*refs, seq_len, has_s0, want_state):
    it = iter(refs)
    dqkv_ref = next(it)
    ba_ref = next(it)
    bar_ref = next(it)
    barp_ref = next(it)
    dgate_ref = next(it)
    s0_ref = next(it) if has_s0 else None
    convw_ref = next(it)
    alog_r = next(it)
    dtb_r = next(it)
    alog_p = next(it)
    dtb_p = next(it)
    dnn_ref = next(it)
    ones_ref = next(it)
    expand_ref = next(it)
    o_ref = next(it)
    st_ref = next(it) if want_state else None
    q_s, k_s, v_s, kt_s, gx_s, bx_s, grp_s, gr_s, of_s, ob_s, s_s, cu_s, nw_s, qe_s, el_s = it
    cpi = DN_CHUNKS_PER_ITER

    n = seq_len // CHUNK
    c = CHUNK
    pw = 2 * DN_DK
    n_pairs = DN_HEADS // 2
    r64 = lax.broadcasted_iota(jnp.int32, (c, c), 0)
    c64 = lax.broadcasted_iota(jnp.int32, (c, c), 1)
    rp = lax.broadcasted_iota(jnp.int32, (c, pw), 0)
    cp_raw = lax.broadcasted_iota(jnp.int32, (c, pw), 1)
    cp = cp_raw % c
    triu_pad = jnp.where((rp <= cp_raw) & (cp_raw < c), 1.0, 0.0).astype(bf16)
    tril_pad = jnp.where((rp >= cp_raw) & (cp_raw < c), 1.0, 0.0).astype(bf16)
    lane_t = lax.broadcasted_iota(jnp.int32, (1, LANE_TILE), 1)
    lower_p = rp >= cp
    upper_p = rp <= cp
    eye_p = jnp.where(rp == cp, 1.0, 0.0)
    same8 = (rp // 8) == (cp // 8)
    off16 = ((rp // 16) == (cp // 16)) & ~same8
    off32 = ((rp // 32) == (cp // 32)) & ((rp // 16) != (cp // 16))
    off64 = (rp // 32) != (cp // 32)
    rb2 = lax.broadcasted_iota(jnp.int32, (pw, pw), 0)
    cb2 = lax.broadcasted_iota(jnp.int32, (pw, pw), 1)
    bd_mask = (rb2 // c) == (cb2 // c)
    bd_triu = jnp.where(bd_mask & (rb2 % c <= cb2 % c), 1.0, 0.0).astype(bf16)
    bd_tril = jnp.where(bd_mask & (rb2 % c >= cb2 % c), 1.0, 0.0).astype(bf16)
    lane16 = lax.broadcasted_iota(jnp.int32, (1, N_GATES), 1)
    sub16 = lax.broadcasted_iota(jnp.int32, (N_GATES, 1), 0)
    sub8 = lax.broadcasted_iota(jnp.int32, (2 * n_pairs, 1), 0)
    sub_pw = lax.broadcasted_iota(jnp.int32, (pw, 1), 0)
    row_c = lax.broadcasted_iota(jnp.int32, (c, 1), 0)

    zero_b = jnp.zeros((), bf16)
    rq = rb2 % c
    cq = cb2 % c
    bd_same8 = bd_mask & ((rq // 8) == (cq // 8))
    bd_off16 = bd_mask & ((rq // 16) == (cq // 16)) & ((rq // 8) != (cq // 8))
    bd_off32 = bd_mask & ((rq // 32) == (cq // 32)) & ((rq // 16) != (cq // 16))
    bd_off64 = bd_mask & ((rq // 32) != (cq // 32))

    def bdiag(x):
        xb = x.astype(bf16)
        return jnp.where(bd_mask, jnp.concatenate([xb, xb], axis=0), zero_b)

    def conv_silu(ci):
        r0 = pl.multiple_of(ci * c, c)
        xb = dqkv_ref[0, pl.ds(r0, c), :].astype(f32)
        p0 = pl.multiple_of(jnp.maximum(r0 - 16, 0), 16)
        n0 = pl.multiple_of(jnp.minimum(r0 + c, seq_len - 16), 16)
        prev = dqkv_ref[0, pl.ds(p0, 16), :].astype(f32)[15:16, :]
        nxt = dqkv_ref[0, pl.ds(n0, 16), :].astype(f32)[0:1, :]
        prev = jnp.where(ci > 0, prev, 0.0)
        nxt = jnp.where(ci < n - 1, nxt, 0.0)
        xm = jnp.where(row_c == 0, prev, pltpu.roll(xb, 1, 0))
        xp = jnp.where(row_c == c - 1, nxt, pltpu.roll(xb, c - 1, 0))
        w = convw_ref[...]
        return _silu(xm * w[0:1, :] + xb * w[1:2, :] + xp * w[2:3, :])

    def beta_col(ci):
        ba = ba_ref[0, pl.ds(pl.multiple_of(ci * c, c), c), :]
        return jnp.where(lane_t < N_GATES, jax.nn.sigmoid(ba), 0.0)

    def stacked_sums(parts_per_chunk, tri_a, tri_b):
        rows = parts_per_chunk[0][0].shape[0]
        stack = jnp.concatenate([p for parts in parts_per_chunk for p in parts], axis=0)
        res_a = _dot(stack, tri_a)
        res_b = _dot(stack, tri_b)
        out = []
        for s, parts in enumerate(parts_per_chunk):
            base = s * len(parts) * rows
            blocks = [slice(base + k * rows, base + (k + 1) * rows) for k in range(len(parts))]
            out.append((sum(res_a[b] for b in blocks), sum(res_b[b] for b in blocks)))
        return out

    def pre_phase_a(it):
        cis = [it * cpi + s for s in range(cpi)]
        act = [conv_silu(ci) for ci in cis]
        betas = [beta_col(ci) for ci in cis]
        g_r = [_split3(-jnp.exp(alog_r[...]) * _softplus(bar_ref[0, ci][N_GATES:2 * N_GATES, :] + dtb_r[...]))
               for ci in cis]
        g_p = [_split3(-jnp.exp(alog_p[...]) * _softplus(barp_ref[0, ci] + dtb_p[...])) for ci in cis]
        sumsq = _group_sumsq_many([a[:, 0:2 * DN_W] for a in act], ones_ref[...])
        gr = [jnp.where(sub16 < DN_HEADS, f, b) for f, b in stacked_sums(g_r, triu_pad, tril_pad)]
        grp = [jnp.where(sub8 < n_pairs, f, b) for f, b in stacked_sums(g_p, bd_triu, bd_tril)]
        return act, sumsq, betas, gr, grp

    def pre_phase_b(pa):
        _, _, betas, gr, _ = pa
        zpad = jnp.zeros((LANE_TILE - N_GATES, LANE_TILE), f32)
        gcol = [jnp.transpose(jnp.concatenate([g16, zpad], axis=0))[0:c, :] for g16 in gr]
        parts = [p for g in gcol for p in _split3(g)] + [p for b in betas for p in _split2(b)]
        res = _dot(jnp.concatenate(parts, axis=0), expand_ref[...])
        gx = [sum(res[(3 * s + k) * c:(3 * s + k + 1) * c] for k in range(3)) for s in range(cpi)]
        off = 3 * cpi * c
        bx = [sum(res[off + (2 * s + k) * c:off + (2 * s + k + 1) * c] for k in range(2)) for s in range(cpi)]
        return gx, bx

    def pre_phase_c(pa, pb, slot):
        act, sumsq, _, gr, grp = pa
        gx, bx = pb
        for s in range(cpi):
            rs = slice(s * c, (s + 1) * c)
            a = act[s]
            scale = lax.rsqrt(sumsq[s] + EPS)
            q_s[slot, rs, :] = a[:, 0:DN_W] * scale[:, 0:DN_W] * (DN_DK ** -0.5)
            kn = a[:, DN_W:2 * DN_W] * scale[:, DN_W:2 * DN_W]
            k_s[slot, rs, :] = kn
            v_s[slot, rs, :] = a[:, 2 * DN_W:3 * DN_W]
            kt_s[slot, s] = jnp.transpose(jnp.concatenate([kn, kn], axis=0)).astype(bf16)
            gx_s[slot, rs, :] = gx[s]
            bx_s[slot, rs, :] = bx[s]
            gr_s[slot, s] = gr[s][:, 0:c]
            grp_s[slot, s] = grp[s]

    zero_blk = jnp.zeros((DN_DK, DN_DK), f32)
    for d in range(2):
        for p in range(n_pairs):
            if has_s0:
                top = jnp.concatenate([s0_ref[0, 0, d, 2 * p], zero_blk], axis=1)
                bot = jnp.concatenate([zero_blk, s0_ref[0, 0, d, 2 * p + 1]], axis=1)
                s_s[d * n_pairs + p] = jnp.concatenate([top, bot], axis=0)
            else:
                s_s[d * n_pairs + p] = jnp.zeros((pw, pw), f32)

    def each(fn, *lists):
        return [fn(*vals) for vals in zip(*lists)]

    def fused_half(it, slot, it_next, slot_next):
        pa = pre_phase_a(it_next) if it_next is not None else None
        qs, ks, vs, bx, eg, decay, strict, ktd, kq, where_to = [], [], [], [], [], [], [], [], [], []
        for sub in range(cpi):
            ci = it * cpi + sub
            r0 = pl.multiple_of(ci * c, c)
            rs = slice(sub * c, (sub + 1) * c)
            grp = grp_s[slot, sub]
            gr16 = gr_s[slot, sub]
            for p in range(n_pairs):
                sl = slice(p * pw, (p + 1) * pw)
                q_p = q_s[slot, rs, sl]
                k_p = k_s[slot, rs, sl]
                v_p = v_s[slot, rs, sl]
                kt = kt_s[slot, sub, p * pw:(p + 1) * pw, :]
                kq_p = _dot(jnp.concatenate([k_p, q_p], axis=0).astype(bf16),
                            jnp.where(bd_mask, kt, jnp.zeros((), bf16)))
                for d in range(2):
                    pp = d * n_pairs + p
                    j0 = d * DN_HEADS + 2 * p
                    xl = slice(pp * pw, (pp + 1) * pw)
                    g_end = gr16[:, c - 1:c] if d == 0 else gr16[:, 0:1]
                    e_rest_t = jnp.exp(g_end - gr16)
                    incl = lower_p if d == 0 else upper_p
                    gx = gx_s[slot, rs, xl]
                    g_last = gx[c - 1:c, :] if d == 0 else gx[0:1, :]
                    el_s[ci, pp:pp + 1, :] = jnp.exp(g_last)
                    qs.append(q_p)
                    ks.append(k_p)
                    vs.append(v_p)
                    kq.append(kq_p)
                    bx.append(bx_s[slot, rs, xl])
                    eg.append(jnp.exp(gx))
                    diff = gx - grp[pp:pp + 1, :]
                    decay.append(jnp.where(incl, jnp.exp(jnp.where(incl, diff, 0.0)), 0.0))
                    strict.append((rp > cp) if d == 0 else (rp < cp))
                    fac = jnp.where(sub_pw < c, e_rest_t[j0:j0 + 1, :], e_rest_t[j0 + 1:j0 + 2, :])
                    ktd.append((kt[:, 0:c].astype(f32) * fac).astype(bf16))
                    where_to.append((ci, pp, r0, sl, d))

        pb = pre_phase_b(pa) if pa is not None else None
        lmat = each(lambda m, x, b, dc: jnp.where(m, x[0:c] * b * dc, 0.0), strict, kq, bx, decay)
        amat = each(lambda x, dc: (x[c:2 * c] * dc).astype(bf16), kq, decay)
        l2 = each(lambda l: jnp.concatenate([l.astype(bf16)] * 2, axis=0), lmat)
        d1 = each(lambda l: jnp.where(same8, l, 0.0), lmat)
        d2 = each(lambda a, ll: _dot(a.astype(bf16), jnp.where(bd_same8, ll, zero_b)).astype(bf16), d1, l2)
        tmat = each(lambda a: eye_p - a, d1)
        tmat = each(lambda t, a: t + _dot(a, bdiag(t)), tmat, d2)
        d4 = each(lambda a: _dot(a, bdiag(a)).astype(bf16), d2)
        tmat = each(lambda t, a: t + _dot(a, bdiag(t)), tmat, d4)
        for bd_off in (bd_off16, bd_off32, bd_off64):
            tb = each(lambda t: t.astype(bf16), tmat)
            tl = each(lambda t, ll: _dot(t, jnp.where(bd_off, ll, zero_b)).astype(bf16), tb, l2)
            tmat = each(lambda t, a, b: t - _dot(a, bdiag(b)), tmat, tl, tb)
        uw = each(lambda t, v, k, b, e: _dot(t.astype(bf16), jnp.concatenate([bdiag(v * b), bdiag(k * (b * e))], axis=1)),
                  tmat, vs, ks, bx, eg)
        ku_kw = each(lambda kd, y: _dot(kd, y.astype(bf16)), ktd, uw)
        au_aw = each(lambda a, y: _dot(a, jnp.concatenate([bdiag(y[:, 0:pw]), bdiag(y[:, pw:2 * pw])], axis=1)),
                     amat, uw)
        if pa is not None:
            pre_phase_c(pa, pb, slot_next)
        for (ci, pp, r0, sl, d), q_p, e, kk, aa in zip(where_to, qs, eg, ku_kw, au_aw):
            cu_s[ci, pp] = jnp.where(bd_mask, kk[:, 0:pw], 0.0)
            nw_s[ci, pp] = jnp.where(bd_mask, -kk[:, pw:2 * pw], 0.0).astype(bf16)
            qe_s[ci, pp] = (q_p * e - aa[:, pw:2 * pw]).astype(bf16)
            dst = of_s if d == 0 else ob_s
            dst[pl.ds(r0, c), sl] = aa[:, 0:pw]

    n_it = n // cpi
    first = pre_phase_a(0)
    pre_phase_c(first, pre_phase_b(first), 0)

    def pipe_body(j, carry):
        fused_half(2 * j, 0, 2 * j + 1, 1)
        fused_half(2 * j + 1, 1, 2 * j + 2, 0)
        return carry

    if n_it // 2 - 1 > 0:
        lax.fori_loop(0, n_it // 2 - 1, pipe_body, 0)
    fused_half(n_it - 2, 0, n_it - 1, 1)
    fused_half(n_it - 1, 1, None, None)

    def scan_body(i, carry):
        cis = [i, n - 1 - i]
        pairs = [(d, p) for d in range(2) for p in range(n_pairs)]
        s_old = [s_s[d * n_pairs + p] for d, p in pairs]
        s_b = [s.astype(bf16) for s in s_old]
        o_add = [_dot(qe_s[cis[d], d * n_pairs + p], sb) for (d, p), sb in zip(pairs, s_b)]
        s_upd = [_dot(nw_s[cis[d], d * n_pairs + p], sb) for (d, p), sb in zip(pairs, s_b)]
        for (d, p), s, oa, su in zip(pairs, s_old, o_add, s_upd):
            pp = d * n_pairs + p
            r0 = pl.multiple_of(cis[d] * c, c)
            dst = of_s if d == 0 else ob_s
            dst[pl.ds(r0, c), p * pw:(p + 1) * pw] += oa
            s_s[pp] = s * el_s[cis[d], pp:pp + 1, :] + su + cu_s[cis[d], pp]
        return carry

    lax.fori_loop(0, n, scan_body, 0)

    rb = 128

    def post_body(bi, carry):
        r0 = pl.multiple_of(bi * rb, rb)
        o = of_s[pl.ds(r0, rb), :] + ob_s[pl.ds(r0, rb), :]
        o = o * lax.rsqrt(_group_sumsq(o, ones_ref[...]) * (1.0 / DN_DK) + EPS) * dnn_ref[...]
        gate = dgate_ref[0, pl.ds(r0, rb), :].astype(f32)
        o_ref[0, pl.ds(r0, rb), :] = (o * _silu(gate)).astype(o_ref.dtype)
        return carry

    lax.fori_loop(0, seq_len // rb, post_body, 0)

    if want_state:
        for d in range(2):
            for p in range(n_pairs):
                s = s_s[d * n_pairs + p]
                st_ref[0, 0, d, 2 * p] = s[0:DN_DK, 0:DN_DK]
                st_ref[0, 0, d, 2 * p + 1] = s[DN_DK:pw, DN_DK:pw]


def _dn_call(dqkv3, ba3, bar4, barp4, dgate3, s0, consts, want_state, name):
    b, l, _ = dqkv3.shape
    n = l // CHUNK
    n_pairs = DN_HEADS // 2
    pw = 2 * DN_DK
    state_blk = (1, 1, 2, DN_HEADS, DN_DK, DN_DK)
    state_spec = pl.BlockSpec(state_blk, lambda i: (i, 0, 0, 0, 0, 0))
    in_specs = [pl.BlockSpec((1, l, 3 * DN_W), lambda i: (i, 0, 0)),
                pl.BlockSpec((1, l, LANE_TILE), lambda i: (i, 0, 0)),
                pl.BlockSpec((1, n, 2 * N_GATES, CHUNK), lambda i: (i, 0, 0, 0)),
                pl.BlockSpec((1, n, 2 * n_pairs, pw), lambda i: (i, 0, 0, 0)),
                pl.BlockSpec((1, l, DN_W), lambda i: (i, 0, 0))]
    args = [dqkv3, ba3, bar4, barp4, dgate3]
    if s0 is not None:
        in_specs.append(state_spec)
        args.append(s0)
    in_specs += [_const_spec(a.shape) for a in consts]
    args += consts
    out_shape = [jax.ShapeDtypeStruct((b, l, DN_W), bf16)]
    out_specs = [pl.BlockSpec((1, l, DN_W), lambda i: (i, 0, 0))]
    if want_state:
        out_shape.append(jax.ShapeDtypeStruct((b,) + state_blk[1:], f32))
        out_specs.append(state_spec)
    cpi = DN_CHUNKS_PER_ITER
    rows = cpi * CHUNK
    scratch = [pltpu.VMEM((2, rows, DN_W), f32), pltpu.VMEM((2, rows, DN_W), f32),
               pltpu.VMEM((2, rows, DN_W), f32),
               pltpu.VMEM((2, cpi, DN_W, pw), bf16),
               pltpu.VMEM((2, rows, N_GATES * DN_DK), f32), pltpu.VMEM((2, rows, N_GATES * DN_DK), f32),
               pltpu.VMEM((2, cpi, 2 * n_pairs, pw), f32), pltpu.VMEM((2, cpi, N_GATES, CHUNK), f32),
               pltpu.VMEM((l, DN_W), f32), pltpu.VMEM((l, DN_W), f32),
               pltpu.VMEM((2 * n_pairs, pw, pw), f32),
               pltpu.VMEM((n, 2 * n_pairs, pw, pw), f32),
               pltpu.VMEM((n, 2 * n_pairs, pw, pw), bf16),
               pltpu.VMEM((n, 2 * n_pairs, CHUNK, pw), bf16),
               pltpu.VMEM((n, 2 * n_pairs, pw), f32)]
    res = pl.pallas_call(
        functools.partial(_dn_kernel, seq_len=l, has_s0=s0 is not None, want_state=want_state),
        grid=(b,),
        in_specs=in_specs,
        out_specs=tuple(out_specs),
        out_shape=tuple(out_shape),
        scratch_shapes=scratch,
        compiler_params=_cparams(("arbitrary",)),
        name=name,
    )(*args)
    return res if want_state else (res[0], None)


def _merge_kernel(oa_ref, od_ref, sga_ref, sgd_ref, x_ref, mod_ref, wa_ref, wd_ref, wo_ref, nffn_ref,
                  wr_ref, br_ref, x1_out, h2_out, comb_out):
    mod = mod_ref[0]
    gate1 = mod[:, 2 * D_MODEL:3 * D_MODEL]
    shift2 = mod[:, 3 * D_MODEL:4 * D_MODEL]
    scale2 = mod[:, 4 * D_MODEL:5 * D_MODEL]
    merged = (sga_ref[...].astype(f32) * _dot(oa_ref[...], wa_ref[...])
              + sgd_ref[...].astype(f32) * _dot(od_ref[...], wd_ref[...]))
    m = _dot(merged.astype(bf16), wo_ref[...])
    x1 = x_ref[...] + gate1 * m
    x1_out[...] = x1
    ms = jnp.mean(x1 * x1, axis=-1, keepdims=True)
    h2 = x1 * lax.rsqrt(ms + EPS) * nffn_ref[...]
    h2 = h2 * (1.0 + scale2) + shift2
    h2_out[...] = h2.astype(bf16)

    h_hi, h_lo = _split2(h2)
    w_hi, w_lo = _split2(wr_ref[...])
    logits = _dot(h_hi, w_hi) + _dot(h_hi, w_lo) + _dot(h_lo, w_hi) + br_ref[...]
    lane = lax.broadcasted_iota(jnp.int32, logits.shape, 1)
    big = ROUTER_LANES
    neg = -jnp.inf

    def first_lane(hit):
        return jnp.min(jnp.where(hit, lane, big), axis=-1, keepdims=True)

    is_g = lane < N_GROUPS
    gl = jnp.where(is_g, logits, neg)
    gexp = jnp.where(is_g, jnp.exp(gl - gl.max(axis=-1, keepdims=True)), 0.0)
    gp = gexp / gexp.sum(axis=-1, keepdims=True)
    g_top = gp.max(axis=-1, keepdims=True)
    g_idx = first_lane(is_g & (gp == g_top))
    lo_lane = N_GROUPS + g_idx * EXPERTS_PER_GROUP
    is_e = (lane >= lo_lane) & (lane < lo_lane + EXPERTS_PER_GROUP)
    el = jnp.where(is_e, logits, neg)
    eexp = jnp.where(is_e, jnp.exp(el - el.max(axis=-1, keepdims=True)), 0.0)
    ep = jnp.where(is_e, eexp / eexp.sum(axis=-1, keepdims=True), -1.0)
    p1 = ep.max(axis=-1, keepdims=True)
    i1 = first_lane(ep == p1)
    ep2 = jnp.where(lane == i1, -1.0, ep)
    p2 = ep2.max(axis=-1, keepdims=True)
    i2 = first_lane(ep2 == p2)
    tot = p1 + p2
    comb_out[...] = (jnp.where(lane == i1, p1 / tot * g_top, 0.0)
                     + jnp.where(lane == i2, p2 / tot * g_top, 0.0))


def _merge_call(oa, od, sga, sgd, x2d, mod3, mod_row, wa, wd, wo, nffn, wr, br, seq_len, name):
    t = x2d.shape[0]
    tm = 512
    tiles_per_seq = max(seq_len // tm, 1)

    def row_spec(w):
        return pl.BlockSpec((tm, w), lambda i: (i, 0))

    return pl.pallas_call(
        _merge_kernel,
        grid=(t // tm,),
        in_specs=[row_spec(ATTN_Q_W), row_spec(DN_W), row_spec(D_MODEL), row_spec(D_MODEL), row_spec(D_MODEL),
                  pl.BlockSpec((1, 1, N_MOD * D_MODEL), lambda i: (mod_row(i // tiles_per_seq), 0, 0)),
                  _const_spec(wa.shape), _const_spec(wd.shape), _const_spec(wo.shape),
                  _const_spec((1, D_MODEL)), _const_spec(wr.shape), _const_spec(br.shape)],
        out_specs=(row_spec(D_MODEL), row_spec(D_MODEL), row_spec(ROUTER_LANES)),
        out_shape=(jax.ShapeDtypeStruct((t, D_MODEL), f32),
                   jax.ShapeDtypeStruct((t, D_MODEL), bf16),
                   jax.ShapeDtypeStruct((t, ROUTER_LANES), f32)),
        compiler_params=_cparams(("arbitrary",)),
        name=name,
    )(oa, od, sga, sgd, x2d, mod3, wa, wd, wo, nffn, wr, br)


def _moe_kernel(h2_ref, comb_ref, x1_ref, mod_ref, wg_ref, wu_ref, wdn_ref, nfin_ref, y_ref, acc_ref):
    g = pl.program_id(1)

    @pl.when(g == 0)
    def _():
        acc_ref[...] = jnp.zeros_like(acc_ref)

    h2 = h2_ref[...]
    comb = comb_ref[...]
    lane = lax.broadcasted_iota(jnp.int32, comb.shape, 1)
    first = N_GROUPS + g * EXPERTS_PER_GROUP
    parts = []
    for e in range(EXPERTS_PER_GROUP):
        hid = _silu(_dot(h2, wg_ref[e])) * _dot(h2, wu_ref[e])
        ce = jnp.sum(jnp.where(lane == first + e, comb, 0.0), axis=-1, keepdims=True)
        parts.append((hid * ce).astype(bf16))
    acc_ref[...] += _dot(jnp.concatenate(parts, axis=1), wdn_ref[0])

    @pl.when(g == N_GROUPS - 1)
    def _():
        gate2 = mod_ref[0][:, 5 * D_MODEL:6 * D_MODEL]
        x2 = x1_ref[...] + gate2 * acc_ref[...]
        ms = jnp.mean(x2 * x2, axis=-1, keepdims=True)
        y_ref[...] = x2 * lax.rsqrt(ms + EPS) * nfin_ref[...]


def _moe_call(h2, comb, x1, mod3, mod_row, wg, wu, wdn, nfin, seq_len, name):
    t = h2.shape[0]
    tm = 1024
    tiles_per_seq = max(seq_len // tm, 1)
    return pl.pallas_call(
        _moe_kernel,
        grid=(t // tm, N_GROUPS),
        in_specs=[pl.BlockSpec((tm, D_MODEL), lambda i, e: (i, 0)),
                  pl.BlockSpec((tm, ROUTER_LANES), lambda i, e: (i, 0)),
                  pl.BlockSpec((tm, D_MODEL), lambda i, e: (i, 0)),
                  pl.BlockSpec((1, 1, N_MOD * D_MODEL), lambda i, e: (mod_row(i // tiles_per_seq), 0, 0)),
                  pl.BlockSpec((EXPERTS_PER_GROUP, D_MODEL, D_FF_EXPERT), lambda i, e: (e, 0, 0)),
                  pl.BlockSpec((EXPERTS_PER_GROUP, D_MODEL, D_FF_EXPERT), lambda i, e: (e, 0, 0)),
                  pl.BlockSpec((1, EXPERTS_PER_GROUP * D_FF_EXPERT, D_MODEL), lambda i, e: (e, 0, 0)),
                  pl.BlockSpec((1, D_MODEL), lambda i, e: (0, 0))],
        out_specs=pl.BlockSpec((tm, D_MODEL), lambda i, e: (i, 0)),
        out_shape=jax.ShapeDtypeStruct((t, D_MODEL), f32),
        scratch_shapes=[pltpu.VMEM((tm, D_MODEL), f32)],
        compiler_params=_cparams(("arbitrary", "arbitrary")),
        name=name,
    )(h2, comb, x1, mod3, wg, wu, wdn, nfin)


def _rope_tables(seq_len):
    t = np.arange(seq_len)
    row = (t // GRID_W).astype(np.float64)
    col = (t % GRID_W).astype(np.float64)
    half = HEAD_DIM // 2
    inv = np.power(ROPE_THETA, -np.arange(0, half, 2, dtype=np.float64) / half)
    d = np.arange(HEAD_DIM)
    freq = inv[d % (half // 2)]
    pos = np.where(d[None, :] < half, row[:, None], col[:, None])
    ang = pos * freq[None, :]
    sign = np.where((d % half) < half // 2, -1.0, 1.0)
    cos = np.tile(np.cos(ang), (1, ATTN_HEADS)).astype(np.float32)
    sin = np.tile(np.sin(ang) * sign[None, :], (1, ATTN_HEADS)).astype(np.float32)
    return jnp.asarray(cos), jnp.asarray(sin)


def _ones_block_diag():
    i = np.arange(LANE_TILE)
    return jnp.asarray((i[:, None] // HEAD_DIM == i[None, :] // HEAD_DIM).astype(np.float32), dtype=bf16)


def _lane_replicator():
    i = np.arange(HEAD_DIM)
    j = np.arange(ATTN_GROUP * HEAD_DIM)
    return jnp.asarray((i[:, None] == j[None, :] % HEAD_DIM).astype(np.float32), dtype=bf16)


def _lane_replicator_cat():
    i = np.arange(ATTN_KV_W)
    j = np.arange(ATTN_GROUP * HEAD_DIM)
    mats = [(i[:, None] == kv * HEAD_DIM + j[None, :] % HEAD_DIM).astype(np.float32) for kv in range(ATTN_KV_HEADS)]
    return jnp.asarray(np.stack(mats), dtype=bf16)


def _trunk(x3, mod3, mod_row, wts, prefix):
    n_seq, seq_len, _ = x3.shape
    t = n_seq * seq_len
    x2d = x3.reshape(t, D_MODEL)
    is_lat = prefix is not None
    rope_tabs = _rope_tables(seq_len) if is_lat else None

    res = _proj_call(
        x2d, mod3, mod_row, wts["nmix"], wts["w_qkv"], wts["w_gates"], wts["wba"], wts["qn"],
        wts["kn"], wts["ones_bd"], rope_tabs, n_seq, seq_len)
    q, kcat, vcat = res[:3]
    k, v = (None, None) if is_lat else (jnp.swapaxes(res[3], 3, 4), jnp.swapaxes(res[4], 3, 4))
    dqkv, ba, bat, dgate, sga, sgd = res[-6:]

    q3 = q.reshape(n_seq, seq_len, ATTN_Q_W)
    kcat3 = kcat.reshape(n_seq, seq_len, ATTN_KV_W)
    vcat3 = vcat.reshape(n_seq, seq_len, ATTN_KV_W)
    if is_lat:
        pk, pv, s0 = prefix
        o_attn = _attn_call(q3, [pk, pv, kcat3, vcat3], wts["rep"], wts["repcat"], 1024, "attn_lat")
    else:
        s0 = None
        o_attn = _attn_call(q3, [kcat3, vcat3], wts["rep"], wts["repcat"], seq_len, "attn_ctx")

    n_chunks = seq_len // CHUNK
    bar4 = bat.reshape(2 * N_GATES, n_seq, n_chunks, CHUNK).transpose(1, 2, 0, 3)
    barp4 = bat[N_GATES:].reshape(N_GATES // 2, 2, n_seq, n_chunks, CHUNK).transpose(2, 3, 0, 1, 4).reshape(
        n_seq, n_chunks, N_GATES // 2, 2 * CHUNK)
    dn_consts = [wts[name] for name in ("convw", "alog_r", "dtb_r", "alog_p", "dtb_p", "dnn",
                                        "ones_bd", "expand")]
    o_dn, state = _dn_call(
        dqkv.reshape(n_seq, seq_len, 3 * DN_W), ba.reshape(n_seq, seq_len, LANE_TILE), bar4, barp4,
        dgate.reshape(n_seq, seq_len, DN_W), s0, dn_consts, not is_lat, "dn_lat" if is_lat else "dn_ctx")

    x1, h2, comb = _merge_call(
        o_attn.reshape(t, ATTN_Q_W), o_dn.reshape(t, DN_W), sga, sgd, x2d, mod3, mod_row,
        wts["wa"], wts["wd"], wts["wo"], wts["nffn"], wts["wr"], wts["br"], seq_len,
        "merge_lat" if is_lat else "merge_ctx")

    y = _moe_call(h2, comb, x1, mod3, mod_row, wts["wg"], wts["wu"], wts["wdn"], wts["nfin"], seq_len,
                  "moe_lat" if is_lat else "moe_ctx")
    return y.reshape(n_seq, seq_len, D_MODEL), k, v, state


def kernel(x_prompt, x_sample, c, cache_attn_k, cache_attn_v, state_delta, c_ctx, w_mod, b_mod, norm_mix, norm_ffn, norm_final, w_in, q_norm, k_norm, conv_w, a_log, dt_bias, dn_norm, w_attn_br, w_dn_br, w_out, w_rg, b_rg, w_re, b_re, w_gate_e, w_up_e, w_down_e):
    layer = 0
    n_lat = x_sample.shape[0]
    w_in_l = w_in[layer]
    ba_lo = OFF_DGATE
    wts = {
        "nmix": norm_mix[layer][None, :],
        "w_qkv": w_in_l[:, :ba_lo].astype(bf16),
        "w_gates": w_in_l[:, ba_lo + 2 * N_GATES:].astype(bf16),
        "wba": jnp.pad(w_in_l[:, ba_lo:ba_lo + 2 * N_GATES], ((0, 0), (0, LANE_TILE - 2 * N_GATES))).astype(bf16),
        "qn": jnp.tile(q_norm[layer], ATTN_HEADS)[None, :],
        "kn": jnp.tile(k_norm[layer], ATTN_KV_HEADS)[None, :],
        "ones_bd": _ones_block_diag(),
        "rep": _lane_replicator(),
        "repcat": _lane_replicator_cat(),
        "convw": conv_w[layer],
        "alog_r": a_log[layer].reshape(N_GATES, 1),
        "dtb_r": dt_bias[layer].reshape(N_GATES, 1),
        "alog_p": jnp.repeat(a_log[layer].reshape(N_GATES // 2, 2), CHUNK, axis=1),
        "dtb_p": jnp.repeat(dt_bias[layer].reshape(N_GATES // 2, 2), CHUNK, axis=1),
        "expand": jnp.asarray(np.repeat(np.eye(LANE_TILE, N_GATES, dtype=np.float32), DN_DK, axis=1), dtype=bf16),
        "dnn": jnp.tile(dn_norm[layer], DN_HEADS)[None, :],
        "wa": w_attn_br[layer].astype(bf16),
        "wd": w_dn_br[layer].astype(bf16),
        "wo": w_out[layer].astype(bf16),
        "nffn": norm_ffn[layer][None, :],
        "wr": jnp.concatenate([w_rg[layer], w_re[layer],
                               jnp.zeros((D_MODEL, ROUTER_LANES - N_GROUPS - N_EXPERTS), f32)], axis=1),
        "br": jnp.concatenate([b_rg[layer], b_re[layer],
                               jnp.zeros((ROUTER_LANES - N_GROUPS - N_EXPERTS,), f32)])[None, :],
        "wg": w_gate_e[layer].astype(bf16),
        "wu": w_up_e[layer].astype(bf16),
        "wdn": w_down_e[layer].reshape(N_GROUPS, EXPERTS_PER_GROUP * D_FF_EXPERT, D_MODEL).astype(bf16),
        "nfin": norm_final[None, :],
    }
    cond8 = jnp.concatenate([c_ctx[None, :], c, jnp.zeros((8 - 1 - n_lat, D_MODEL), f32)], axis=0)
    mod3 = _mod_call(cond8, w_mod[layer], b_mod[layer][None, :])[:, None, :]

    y_prompt, new_k, new_v, new_state = _trunk(x_prompt, mod3, lambda s: 0, wts, None)
    y_sample, _, _, _ = _trunk(x_sample, mod3, lambda s: s + 1, wts,
                               (cache_attn_k, cache_attn_v, state_delta))
    return (y_prompt, y_sample, new_k, new_v, new_state)
```

```python
import functools

import numpy as np
import jax
import jax.numpy as jnp
from jax import lax
from jax.experimental import pallas as pl
from jax.experimental.pallas import tpu as pltpu

f32 = jnp.float32
bf16 = jnp.bfloat16

D_MODEL = 1024
HEAD_DIM = 64
ATTN_HEADS = 8
ATTN_KV_HEADS = 2
ATTN_GROUP = ATTN_HEADS // ATTN_KV_HEADS
GRID_W = 64
ROPE_THETA = 10000.0
DN_HEADS = 8
DN_DK = 64
CHUNK = 64
N_GROUPS = 4
EXPERTS_PER_GROUP = 4
N_EXPERTS = 16
D_FF_EXPERT = 256
N_MOD = 6
EPS = 1e-6

ATTN_Q_W = ATTN_HEADS * HEAD_DIM
ATTN_KV_W = ATTN_KV_HEADS * HEAD_DIM
DN_W = DN_HEADS * DN_DK
N_GATES = 2 * DN_HEADS

OFF_Q = 0
OFF_K = ATTN_Q_W
OFF_V = OFF_K + ATTN_KV_W
OFF_DQKV = OFF_V + ATTN_KV_W
OFF_DGATE = OFF_DQKV + 3 * DN_W

LANE_TILE = 128
ROUTER_LANES = LANE_TILE
VMEM_LIMIT = 56 * 1024 * 1024
DN_CHUNKS_PER_ITER = 2

_TRANS_B = (((1,), (1,)), ((), ()))


def _cparams(sem):
    return pltpu.CompilerParams(dimension_semantics=sem, vmem_limit_bytes=VMEM_LIMIT)


def _dot(a, b):
    return jnp.dot(a, b, preferred_element_type=f32)


def _silu(x):
    return x * jax.nn.sigmoid(x)


def _softplus(x):
    return jnp.maximum(x, 0.0) + jnp.log1p(jnp.exp(-jnp.abs(x)))


def _split2(x):
    hi = x.astype(bf16)
    lo = (x - hi.astype(f32)).astype(bf16)
    return hi, lo


def _split3(x):
    a = x.astype(bf16)
    r = x - a.astype(f32)
    b = r.astype(bf16)
    c = (r - b.astype(f32)).astype(bf16)
    return a, b, c


def _group_sumsq_many(arrs, ones_pair):
    pieces = []
    for x in arrs:
        hi, lo = _split2(x * x)
        for s in range(x.shape[1] // LANE_TILE):
            pieces.append(hi[:, s * LANE_TILE:(s + 1) * LANE_TILE])
            pieces.append(lo[:, s * LANE_TILE:(s + 1) * LANE_TILE])
    res = _dot(jnp.concatenate(pieces, axis=0), ones_pair)
    out, off = [], 0
    for x in arrs:
        r = x.shape[0]
        slabs = []
        for s in range(x.shape[1] // LANE_TILE):
            slabs.append(res[off:off + r] + res[off + r:off + 2 * r])
            off += 2 * r
        out.append(slabs[0] if len(slabs) == 1 else jnp.concatenate(slabs, axis=1))
    return out


def _group_sumsq(x, ones_pair):
    return _group_sumsq_many([x], ones_pair)[0]


def _const_spec(shape):
    nd = len(shape)
    return pl.BlockSpec(shape, lambda *_: (0,) * nd, pipeline_mode=pl.Buffered(1))


def _mod_kernel(c_ref, w_ref, b_ref, o_ref):
    c = c_ref[...]
    o_ref[...] = _dot(_silu(c).astype(bf16), w_ref[...].astype(bf16)) + b_ref[...]


def _mod_call(cond8, w_mod, b_mod):
    tn = 1536
    n = w_mod.shape[1]
    return pl.pallas_call(
        _mod_kernel,
        grid=(n // tn,),
        in_specs=[_const_spec((8, D_MODEL)),
                  pl.BlockSpec((D_MODEL, tn), lambda j: (0, j)),
                  pl.BlockSpec((1, tn), lambda j: (0, j))],
        out_specs=pl.BlockSpec((8, tn), lambda j: (0, j)),
        out_shape=jax.ShapeDtypeStruct((8, n), f32),
        compiler_params=_cparams(("arbitrary",)),
        name="mod",
    )(cond8, w_mod, b_mod)


def _rope(x, cos, sin):
    w = x.shape[-1]
    half, quarter = HEAD_DIM // 2, HEAD_DIM // 4
    lane = lax.broadcasted_iota(jnp.int32, x.shape, 1)
    first = (lane % half) < quarter
    swapped = jnp.where(first, pltpu.roll(x, w - quarter, 1), pltpu.roll(x, quarter, 1))
    return x * cos + swapped * sin


def _proj_kernel(*refs, rope, seq_per_tile):
    (x_ref, mod_ref, nmix_ref, wa_ref, wb_ref, wba_ref, qn_ref, kn_ref, ones_ref) = refs[:9]
    pos = 9
    if rope:
        cos_ref, sin_ref = refs[9:11]
        pos = 11
    outs = list(refs[pos:])
    q_out, kcat_out, vcat_out = outs[:3]
    if not rope:
        k_out, v_out = outs[3:5]
        outs = outs[2:]
    (dqkv_out, ba_out, bat_out, dgate_out, sga_out, sgd_out) = outs[3:]

    x = x_ref[...]
    mod = mod_ref[0]
    shift1 = mod[:, 0:D_MODEL]
    scale1 = mod[:, D_MODEL:2 * D_MODEL]
    ms = jnp.mean(x * x, axis=-1, keepdims=True)
    h = x * lax.rsqrt(ms + EPS) * nmix_ref[...]
    h = h * (1.0 + scale1) + shift1
    hb = h.astype(bf16)

    aq = _dot(hb, wa_ref[:, OFF_Q:OFF_Q + ATTN_Q_W])
    ak = _dot(hb, wa_ref[:, OFF_K:OFF_K + ATTN_KV_W])
    av = _dot(hb, wa_ref[:, OFF_V:OFF_V + ATTN_KV_W])
    ssq, ssk = _group_sumsq_many([aq, ak], ones_ref[...])
    aq = aq * lax.rsqrt(ssq * (1.0 / HEAD_DIM) + EPS) * qn_ref[...]
    ak = ak * lax.rsqrt(ssk * (1.0 / HEAD_DIM) + EPS) * kn_ref[...]
    if not rope:
        tm = x.shape[0]
        seq = tm // seq_per_tile
        akt = jnp.transpose(ak)
        avt = jnp.transpose(av)
        for s in range(seq_per_tile):
            for kv in range(ATTN_KV_HEADS):
                k_out[s, 0, kv] = akt[kv * HEAD_DIM:(kv + 1) * HEAD_DIM, s * seq:(s + 1) * seq]
                v_out[s, 0, kv] = avt[kv * HEAD_DIM:(kv + 1) * HEAD_DIM, s * seq:(s + 1) * seq]
    else:
        aq = _rope(aq, cos_ref[...], sin_ref[...])
        ak = _rope(ak, cos_ref[:, 0:ATTN_KV_W], sin_ref[:, 0:ATTN_KV_W])
    q_out[...] = (aq * (HEAD_DIM ** -0.5)).astype(bf16)
    kcat_out[...] = ak.astype(bf16)
    vcat_out[...] = av.astype(bf16)

    dqkv_out[...] = _dot(hb, wa_ref[:, OFF_DQKV:OFF_DGATE]).astype(bf16)
    ba = _dot(hb, wba_ref[...])
    ba_out[...] = ba
    bat_out[...] = jnp.transpose(ba)[0:2 * N_GATES, :]
    dgate_out[...] = _dot(hb, wb_ref[:, 0:DN_W]).astype(bf16)
    sga_out[...] = jax.nn.sigmoid(_dot(hb, wb_ref[:, DN_W:DN_W + D_MODEL])).astype(bf16)
    sgd_out[...] = jax.nn.sigmoid(_dot(hb, wb_ref[:, DN_W + D_MODEL:DN_W + 2 * D_MODEL])).astype(bf16)


def _proj_call(x2d, mod3, mod_row, nmix, wa, wb, wba, qn, kn, ones_bd, rope_tabs, n_seq, seq_len):
    t = x2d.shape[0]
    tm = 512
    rope = rope_tabs is not None
    tiles_per_seq = max(seq_len // tm, 1)
    seq_per_tile = max(tm // seq_len, 1)
    in_specs = [pl.BlockSpec((tm, D_MODEL), lambda i: (i, 0)),
                pl.BlockSpec((1, 1, N_MOD * D_MODEL), lambda i: (mod_row(i // tiles_per_seq), 0, 0)),
                _const_spec((1, D_MODEL)),
                _const_spec(wa.shape),
                _const_spec(wb.shape),
                _const_spec((D_MODEL, LANE_TILE)),
                _const_spec((1, ATTN_Q_W)),
                _const_spec((1, ATTN_KV_W)),
                _const_spec((LANE_TILE, LANE_TILE))]
    args = [x2d, mod3, nmix, wa, wb, wba, qn, kn, ones_bd]

    def row_spec(w):
        return pl.BlockSpec((tm, w), lambda i: (i, 0))

    out_shape = [jax.ShapeDtypeStruct((t, ATTN_Q_W), bf16),
                 jax.ShapeDtypeStruct((t, ATTN_KV_W), bf16),
                 jax.ShapeDtypeStruct((t, ATTN_KV_W), bf16)]
    out_specs = [row_spec(ATTN_Q_W), row_spec(ATTN_KV_W), row_spec(ATTN_KV_W)]
    if rope:
        in_specs += [pl.BlockSpec((tm, ATTN_Q_W), lambda i: (i % tiles_per_seq, 0))] * 2
        args += list(rope_tabs)
    else:
        kv_shape = jax.ShapeDtypeStruct((n_seq, 1, ATTN_KV_HEADS, HEAD_DIM, seq_len), f32)
        kv_spec = pl.BlockSpec((seq_per_tile, 1, ATTN_KV_HEADS, HEAD_DIM, seq_len), lambda i: (i, 0, 0, 0, 0))
        out_shape += [kv_shape, kv_shape]
        out_specs += [kv_spec, kv_spec]
    out_shape += [jax.ShapeDtypeStruct((t, 3 * DN_W), bf16),
                  jax.ShapeDtypeStruct((t, LANE_TILE), f32),
                  jax.ShapeDtypeStruct((2 * N_GATES, t), f32),
                  jax.ShapeDtypeStruct((t, DN_W), bf16),
                  jax.ShapeDtypeStruct((t, D_MODEL), bf16),
                  jax.ShapeDtypeStruct((t, D_MODEL), bf16)]
    out_specs += [row_spec(3 * DN_W), row_spec(LANE_TILE),
                  pl.BlockSpec((2 * N_GATES, tm), lambda i: (0, i)),
                  row_spec(DN_W), row_spec(D_MODEL), row_spec(D_MODEL)]
    return pl.pallas_call(
        functools.partial(_proj_kernel, rope=rope, seq_per_tile=seq_per_tile),
        grid=(t // tm,),
        in_specs=in_specs,
        out_specs=tuple(out_specs),
        out_shape=tuple(out_shape),
        compiler_params=_cparams(("arbitrary",)),
        name="proj_lat" if rope else "proj_ctx",
    )(*args)


def _attn_kernel(*refs, n_sets, scores_first):
    q_ref = refs[0]
    kv_refs = refs[1:1 + 2 * n_sets]
    rep_ref, repcat_ref = refs[1 + 2 * n_sets:3 + 2 * n_sets]
    o_ref = refs[-1]
    width = ATTN_GROUP * HEAD_DIM
    blk = lax.broadcasted_iota(jnp.int32, (1, width), 1) // HEAD_DIM

    def head_slab(ref, kv):
        if len(ref.shape) == 5:
            return _dot(ref[0, 0, kv].astype(bf16), rep_ref[...])
        return _dot(ref[0], repcat_ref[kv])

    def scores_of(kv, g, q, k4):
        qg = jnp.where(blk == g, q, jnp.zeros((), bf16))
        return [lax.dot_general(qg, k, _TRANS_B, preferred_element_type=f32) for k in k4]

    def head_out(scores, v4):
        m = scores[0].max(axis=-1, keepdims=True)
        for s in scores[1:]:
            m = jnp.maximum(m, s.max(axis=-1, keepdims=True))
        probs = [jnp.exp(s - m) for s in scores]
        denom = probs[0].sum(axis=-1, keepdims=True)
        for p in probs[1:]:
            denom = denom + p.sum(axis=-1, keepdims=True)
        og = _dot(probs[0].astype(bf16), v4[0])
        for p, v in zip(probs[1:], v4[1:]):
            og = og + _dot(p.astype(bf16), v)
        return og * (1.0 / denom)

    qs, k4s, v4s = [], [], []
    for kv in range(ATTN_KV_HEADS):
        qs.append(q_ref[0, :, kv * width:(kv + 1) * width])
        k4s.append([head_slab(kv_refs[2 * s], kv).astype(bf16) for s in range(n_sets)])
        v4s.append([head_slab(kv_refs[2 * s + 1], kv).astype(bf16) for s in range(n_sets)])
    heads = [(kv, g) for kv in range(ATTN_KV_HEADS) for g in range(ATTN_GROUP)]
    if scores_first:
        all_scores = [scores_of(kv, g, qs[kv], k4s[kv]) for kv, g in heads]
    accs = [jnp.zeros((qs[0].shape[0], width), f32) for _ in range(ATTN_KV_HEADS)]
    for i, (kv, g) in enumerate(heads):
        sc = all_scores[i] if scores_first else scores_of(kv, g, qs[kv], k4s[kv])
        accs[kv] = jnp.where(blk == g, head_out(sc, v4s[kv]), accs[kv])
    for kv in range(ATTN_KV_HEADS):
        o_ref[0, :, kv * width:(kv + 1) * width] = accs[kv].astype(o_ref.dtype)


def _attn_call(q3, kv_sets, rep, repcat, tq, name):
    b, l, _ = q3.shape
    in_specs = [pl.BlockSpec((1, tq, ATTN_Q_W), lambda i, t: (i, t, 0))]
    args = [q3]
    for arr in kv_sets:
        nd = arr.ndim
        blk = (1,) + arr.shape[1:]
        in_specs.append(pl.BlockSpec(blk, lambda i, t, nd=nd: (i,) + (0,) * (nd - 1)))
        args.append(arr)
    in_specs += [_const_spec(rep.shape), _const_spec(repcat.shape)]
    args += [rep, repcat]
    return pl.pallas_call(
        functools.partial(_attn_kernel, n_sets=len(kv_sets) // 2, scores_first=len(kv_sets) == 2),
        grid=(b, l // tq),
        in_specs=in_specs,
        out_specs=pl.BlockSpec((1, tq, ATTN_Q_W), lambda i, t: (i, t, 0)),
        out_shape=jax.ShapeDtypeStruct((b, l, ATTN_Q_W), bf16),
        compiler_params=_cparams(("arbitrary", "arbitrary")),
        name=name,
    )(*args)


def _dn_kernel(*refs, seq_len, nseq, has_s0, want_state):
    it = iter(refs)
    dqkv_ref = next(it)
    ba_ref = next(it)
    bar_ref = next(it)
    barp_ref = next(it)
    dgate_ref = next(it)
    s0_ref = next(it) if has_s0 else None
    convw_ref = next(it)
    alog_r = next(it)
    dtb_r = next(it)
    alog_p = next(it)
    dtb_p = next(it)
    dnn_ref = next(it)
    ones_ref = next(it)
    expand_ref = next(it)
    o_ref = next(it)
    st_ref = next(it) if want_state else None
    q_s, k_s, v_s, kt_s, gx_s, bx_s, grp_s, gr_s, of_s, ob_s, s_s, cu_s, qn_s, el_s = it
    cpi = DN_CHUNKS_PER_ITER

    n = seq_len // CHUNK
    c = CHUNK
    pw = 2 * DN_DK
    n_pairs = DN_HEADS // 2
    rp = lax.broadcasted_iota(jnp.int32, (c, pw), 0)
    cp_raw = lax.broadcasted_iota(jnp.int32, (c, pw), 1)
    cp = cp_raw % c
    triu_pad = jnp.where((rp <= cp_raw) & (cp_raw < c), 1.0, 0.0).astype(bf16)
    tril_pad = jnp.where((rp >= cp_raw) & (cp_raw < c), 1.0, 0.0).astype(bf16)
    lane_t = lax.broadcasted_iota(jnp.int32, (1, LANE_TILE), 1)
    lower_p = rp >= cp
    upper_p = rp <= cp
    eye_p = jnp.where(rp == cp, 1.0, 0.0)
    same8 = (rp // 8) == (cp // 8)
    rb2 = lax.broadcasted_iota(jnp.int32, (pw, pw), 0)
    cb2 = lax.broadcasted_iota(jnp.int32, (pw, pw), 1)
    bd_mask = (rb2 // c) == (cb2 // c)
    bd_triu = jnp.where(bd_mask & (rb2 % c <= cb2 % c), 1.0, 0.0).astype(bf16)
    bd_tril = jnp.where(bd_mask & (rb2 % c >= cb2 % c), 1.0, 0.0).astype(bf16)
    sub16 = lax.broadcasted_iota(jnp.int32, (N_GATES, 1), 0)
    sub8 = lax.broadcasted_iota(jnp.int32, (2 * n_pairs, 1), 0)
    sub_pw = lax.broadcasted_iota(jnp.int32, (pw, 1), 0)
    row_c = lax.broadcasted_iota(jnp.int32, (c, 1), 0)

    zero_b = jnp.zeros((), bf16)
    rq = rb2 % c
    cq = cb2 % c
    bd_same8 = bd_mask & ((rq // 8) == (cq // 8))
    bd_off16 = bd_mask & ((rq // 16) == (cq // 16)) & ((rq // 8) != (cq // 8))
    bd_off32 = bd_mask & ((rq // 32) == (cq // 32)) & ((rq // 16) != (cq // 16))
    bd_off64 = bd_mask & ((rq // 32) != (cq // 32))

    def bdiag(x):
        xb = x.astype(bf16)
        return jnp.where(bd_mask, jnp.concatenate([xb, xb], axis=0), zero_b)

    def conv_silu(sq, ci):
        r0 = pl.multiple_of(ci * c, c)
        xb = dqkv_ref[sq, pl.ds(r0, c), :].astype(f32)
        p0 = pl.multiple_of(jnp.maximum(r0 - 16, 0), 16)
        n0 = pl.multiple_of(jnp.minimum(r0 + c, seq_len - 16), 16)
        prev = dqkv_ref[sq, pl.ds(p0, 16), :].astype(f32)[15:16, :]
        nxt = dqkv_ref[sq, pl.ds(n0, 16), :].astype(f32)[0:1, :]
        prev = jnp.where(ci > 0, prev, 0.0)
        nxt = jnp.where(ci < n - 1, nxt, 0.0)
        xm = jnp.where(row_c == 0, prev, pltpu.roll(xb, 1, 0))
        xp = jnp.where(row_c == c - 1, nxt, pltpu.roll(xb, c - 1, 0))
        w = convw_ref[...]
        return _silu(xm * w[0:1, :] + xb * w[1:2, :] + xp * w[2:3, :])

    def beta_col(sq, ci):
        ba = ba_ref[sq, pl.ds(pl.multiple_of(ci * c, c), c), :]
        return jnp.where(lane_t < N_GATES, jax.nn.sigmoid(ba), 0.0)

    def stacked_sums(parts_per_chunk, tri_a, tri_b):
        rows = parts_per_chunk[0][0].shape[0]
        stack = jnp.concatenate([p for parts in parts_per_chunk for p in parts], axis=0)
        res_a = _dot(stack, tri_a)
        res_b = _dot(stack, tri_b)
        out = []
        for s, parts in enumerate(parts_per_chunk):
            base = s * len(parts) * rows
            blocks = [slice(base + k * rows, base + (k + 1) * rows) for k in range(len(parts))]
            out.append((sum(res_a[b] for b in blocks), sum(res_b[b] for b in blocks)))
        return out

    def pre_phase_a(sq, it):
        cis = [it * cpi + s for s in range(cpi)]
        act = [conv_silu(sq, ci) for ci in cis]
        betas = [beta_col(sq, ci) for ci in cis]
        g_r = [_split3(-jnp.exp(alog_r[...]) * _softplus(bar_ref[sq, ci][N_GATES:2 * N_GATES, :] + dtb_r[...]))
               for ci in cis]
        g_p = [_split3(-jnp.exp(alog_p[...]) * _softplus(barp_ref[sq, ci] + dtb_p[...])) for ci in cis]
        sumsq = _group_sumsq_many([a[:, 0:2 * DN_W] for a in act], ones_ref[...])
        gr = [jnp.where(sub16 < DN_HEADS, f, b) for f, b in stacked_sums(g_r, triu_pad, tril_pad)]
        grp = [jnp.where(sub8 < n_pairs, f, b) for f, b in stacked_sums(g_p, bd_triu, bd_tril)]
        return act, sumsq, betas, gr, grp

    def pre_phase_b(pa):
        _, _, betas, gr, _ = pa
        zpad = jnp.zeros((LANE_TILE - N_GATES, LANE_TILE), f32)
        gcol = [jnp.transpose(jnp.concatenate([g16, zpad], axis=0))[0:c, :] for g16 in gr]
        parts = [p for g in gcol for p in _split3(g)] + [p for b in betas for p in _split2(b)]
        res = _dot(jnp.concatenate(parts, axis=0), expand_ref[...])
        gx = [sum(res[(3 * s + k) * c:(3 * s + k + 1) * c] for k in range(3)) for s in range(cpi)]
        off = 3 * cpi * c
        bx = [sum(res[off + (2 * s + k) * c:off + (2 * s + k + 1) * c] for k in range(2)) for s in range(cpi)]
        return gx, bx

    def pre_phase_c(pa, pb, slot):
        act, sumsq, _, gr, grp = pa
        gx, bx = pb
        for s in range(cpi):
            rs = slice(s * c, (s + 1) * c)
            a = act[s]
            scale = lax.rsqrt(sumsq[s] + EPS)
            q_s[slot, rs, :] = a[:, 0:DN_W] * scale[:, 0:DN_W] * (DN_DK ** -0.5)
            kn = a[:, DN_W:2 * DN_W] * scale[:, DN_W:2 * DN_W]
            k_s[slot, rs, :] = kn
            v_s[slot, rs, :] = a[:, 2 * DN_W:3 * DN_W]
            kt_s[slot, s] = jnp.transpose(jnp.concatenate([kn, kn], axis=0)).astype(bf16)
            gx_s[slot, rs, :] = gx[s]
            bx_s[slot, rs, :] = bx[s]
            gr_s[slot, s] = gr[s][:, 0:c]
            grp_s[slot, s] = grp[s]

    zero_blk = jnp.zeros((DN_DK, DN_DK), f32)
    for sq in range(nseq):
        for d in range(2):
            for p in range(n_pairs):
                if has_s0:
                    top = jnp.concatenate([s0_ref[sq, 0, d, 2 * p], zero_blk], axis=1)
                    bot = jnp.concatenate([zero_blk, s0_ref[sq, 0, d, 2 * p + 1]], axis=1)
                    s_s[sq, d * n_pairs + p] = jnp.concatenate([top, bot], axis=0)
                else:
                    s_s[sq, d * n_pairs + p] = jnp.zeros((pw, pw), f32)

    def each(fn, *lists):
        return [fn(*vals) for vals in zip(*lists)]

    def fused_half(sq, it, slot, nxt):
        pa = pre_phase_a(nxt[0], nxt[1]) if nxt is not None else None
        qs, ks, vs, bx, eg, decay, strict, ktd, kq, where_to = [], [], [], [], [], [], [], [], [], []
        for sub in range(cpi):
            ci = it * cpi + sub
            r0 = pl.multiple_of(ci * c, c)
            rs = slice(sub * c, (sub + 1) * c)
            grp = grp_s[slot, sub]
            gr16 = gr_s[slot, sub]
            for p in range(n_pairs):
                sl = slice(p * pw, (p + 1) * pw)
                q_p = q_s[slot, rs, sl]
                k_p = k_s[slot, rs, sl]
                v_p = v_s[slot, rs, sl]
                kt = kt_s[slot, sub, p * pw:(p + 1) * pw, :]
                kq_p = _dot(jnp.concatenate([k_p, q_p], axis=0).astype(bf16),
                            jnp.where(bd_mask, kt, jnp.zeros((), bf16)))
                for d in range(2):
                    pp = d * n_pairs + p
                    j0 = d * DN_HEADS + 2 * p
                    xl = slice(pp * pw, (pp + 1) * pw)
                    g_end = gr16[:, c - 1:c] if d == 0 else gr16[:, 0:1]
                    e_rest_t = jnp.exp(g_end - gr16)
                    incl = lower_p if d == 0 else upper_p
                    gx = gx_s[slot, rs, xl]
                    g_last = gx[c - 1:c, :] if d == 0 else gx[0:1, :]
                    el_s[sq, ci, pp:pp + 1, :] = jnp.exp(g_last)
                    qs.append(q_p)
                    ks.append(k_p)
                    vs.append(v_p)
                    kq.append(kq_p)
                    bx.append(bx_s[slot, rs, xl])
                    eg.append(jnp.exp(gx))
                    diff = gx - grp[pp:pp + 1, :]
                    decay.append(jnp.where(incl, jnp.exp(jnp.where(incl, diff, 0.0)), 0.0))
                    strict.append((rp > cp) if d == 0 else (rp < cp))
                    fac = jnp.where(sub_pw < c, e_rest_t[j0:j0 + 1, :], e_rest_t[j0 + 1:j0 + 2, :])
                    ktd.append((kt[:, 0:c].astype(f32) * fac).astype(bf16))
                    where_to.append((ci, pp, r0, sl, d))

        pb = pre_phase_b(pa) if pa is not None else None
        lmat = each(lambda m, x, b, dc: jnp.where(m, x[0:c] * b * dc, 0.0), strict, kq, bx, decay)
        amat = each(lambda x, dc: (x[c:2 * c] * dc).astype(bf16), kq, decay)
        l2 = each(lambda l: jnp.concatenate([l.astype(bf16)] * 2, axis=0), lmat)
        d1 = each(lambda l: jnp.where(same8, l, 0.0), lmat)
        d2 = each(lambda a, ll: _dot(a.astype(bf16), jnp.where(bd_same8, ll, zero_b)).astype(bf16), d1, l2)
        tmat = each(lambda a: eye_p - a, d1)
        tmat = each(lambda t, a: t + _dot(a, bdiag(t)), tmat, d2)
        d4 = each(lambda a: _dot(a, bdiag(a)).astype(bf16), d2)
        tmat = each(lambda t, a: t + _dot(a, bdiag(t)), tmat, d4)
        for bd_off in (bd_off16, bd_off32, bd_off64):
            tb = each(lambda t: t.astype(bf16), tmat)
            tl = each(lambda t, ll: _dot(t, jnp.where(bd_off, ll, zero_b)).astype(bf16), tb, l2)
            tmat = each(lambda t, a, b: t - _dot(a, bdiag(b)), tmat, tl, tb)
        uw = each(lambda t, v, k, b, e: _dot(t.astype(bf16), jnp.concatenate([bdiag(v * b), bdiag(k * (b * e))], axis=1)),
                  tmat, vs, ks, bx, eg)
        ku_kw = each(lambda kd, y: _dot(kd, y.astype(bf16)), ktd, uw)
        au_aw = each(lambda a, y: _dot(a, jnp.concatenate([bdiag(y[:, 0:pw]), bdiag(y[:, pw:2 * pw])], axis=1)),
                     amat, uw)
        if pa is not None:
            pre_phase_c(pa, pb, nxt[2])
        for (ci, pp, r0, sl, d), q_p, e, kk, aa in zip(where_to, qs, eg, ku_kw, au_aw):
            cu_s[sq, ci, pp] = jnp.where(bd_mask, kk[:, 0:pw], 0.0)
            qn_s[sq, ci, pp, 0:c] = (q_p * e - aa[:, pw:2 * pw]).astype(bf16)
            qn_s[sq, ci, pp, c:c + pw] = jnp.where(bd_mask, -kk[:, pw:2 * pw], 0.0).astype(bf16)
            dst = of_s if d == 0 else ob_s
            dst[sq, pl.ds(r0, c), sl] = aa[:, 0:pw]

    n_it = n // cpi
    first = pre_phase_a(0, 0)
    pre_phase_c(first, pre_phase_b(first), 0)
    if nseq * n_it <= 4:
        items = [(sq, it_) for sq in range(nseq) for it_ in range(n_it)]
        for k, (sq, it_) in enumerate(items):
            nxt = items[k + 1] + ((k + 1) % 2,) if k + 1 < len(items) else None
            fused_half(sq, it_, k % 2, nxt)
    else:
        assert nseq == 1 and n_it % 2 == 0

        def pipe_body(j, carry):
            fused_half(0, 2 * j, 0, (0, 2 * j + 1, 1))
            fused_half(0, 2 * j + 1, 1, (0, 2 * j + 2, 0))
            return carry

        lax.fori_loop(0, n_it // 2 - 1, pipe_body, 0)
        fused_half(0, n_it - 2, 0, (0, n_it - 1, 1))
        fused_half(0, n_it - 1, 1, None)

    def scan_body(i, carry):
        cis = [i, n - 1 - i]
        pairs = [(sq, d, p) for sq in range(nseq) for d in range(2) for p in range(n_pairs)]
        s_old = [s_s[sq, d * n_pairs + p] for sq, d, p in pairs]
        s_b = [s.astype(bf16) for s in s_old]
        both = [_dot(qn_s[sq, cis[d], d * n_pairs + p], sb) for (sq, d, p), sb in zip(pairs, s_b)]
        for (sq, d, p), s, r in zip(pairs, s_old, both):
            pp = d * n_pairs + p
            r0 = pl.multiple_of(cis[d] * c, c)
            dst = of_s if d == 0 else ob_s
            dst[sq, pl.ds(r0, c), p * pw:(p + 1) * pw] += r[0:c]
            s_s[sq, pp] = s * el_s[sq, cis[d], pp:pp + 1, :] + r[c:c + pw] + cu_s[sq, cis[d], pp]
        return carry

    lax.fori_loop(0, n, scan_body, 0)

    rb = 128
    for sq in range(nseq):
        def post_body(bi, carry, sq=sq):
            r0 = pl.multiple_of(bi * rb, rb)
            o = of_s[sq, pl.ds(r0, rb), :] + ob_s[sq, pl.ds(r0, rb), :]
            o = o * lax.rsqrt(_group_sumsq(o, ones_ref[...]) * (1.0 / DN_DK) + EPS) * dnn_ref[...]
            gate = dgate_ref[sq, pl.ds(r0, rb), :].astype(f32)
            o_ref[sq, pl.ds(r0, rb), :] = (o * _silu(gate)).astype(o_ref.dtype)
            return carry

        lax.fori_loop(0, seq_len // rb, post_body, 0)

    if want_state:
        for sq in range(nseq):
            for d in range(2):
                for p in range(n_pairs):
                    s = s_s[sq, d * n_pairs + p]
                    st_ref[sq, 0, d, 2 * p] = s[0:DN_DK, 0:DN_DK]
                    st_ref[sq, 0, d, 2 * p + 1] = s[DN_DK:pw, DN_DK:pw]


def _dn_call(dqkv3, ba3, bar4, barp4, dgate3, s0, consts, want_state, name):
    b, l, _ = dqkv3.shape
    n = l // CHUNK
    n_pairs = DN_HEADS // 2
    pw = 2 * DN_DK
    cpi = DN_CHUNKS_PER_ITER
    nseq = 2 if (n // cpi <= 2 and b % 2 == 0) else 1
    state_blk = (nseq, 1, 2, DN_HEADS, DN_DK, DN_DK)
    state_spec = pl.BlockSpec(state_blk, lambda i: (i, 0, 0, 0, 0, 0))
    in_specs = [pl.BlockSpec((nseq, l, 3 * DN_W), lambda i: (i, 0, 0)),
                pl.BlockSpec((nseq, l, LANE_TILE), lambda i: (i, 0, 0)),
                pl.BlockSpec((nseq, n, 2 * N_GATES, CHUNK), lambda i: (i, 0, 0, 0)),
                pl.BlockSpec((nseq, n, 2 * n_pairs, pw), lambda i: (i, 0, 0, 0)),
                pl.BlockSpec((nseq, l, DN_W), lambda i: (i, 0, 0))]
    args = [dqkv3, ba3, bar4, barp4, dgate3]
    if s0 is not None:
        in_specs.append(state_spec)
        args.append(s0)
    in_specs += [_const_spec(a.shape) for a in consts]
    args += consts
    out_shape = [jax.ShapeDtypeStruct((b, l, DN_W), bf16)]
    out_specs = [pl.BlockSpec((nseq, l, DN_W), lambda i: (i, 0, 0))]
    if want_state:
        out_shape.append(jax.ShapeDtypeStruct((b,) + state_blk[1:], f32))
        out_specs.append(state_spec)
    rows = cpi * CHUNK
    scratch = [pltpu.VMEM((2, rows, DN_W), f32), pltpu.VMEM((2, rows, DN_W), f32),
               pltpu.VMEM((2, rows, DN_W), f32),
               pltpu.VMEM((2, cpi, DN_W, pw), bf16),
               pltpu.VMEM((2, rows, N_GATES * DN_DK), f32), pltpu.VMEM((2, rows, N_GATES * DN_DK), f32),
               pltpu.VMEM((2, cpi, 2 * n_pairs, pw), f32), pltpu.VMEM((2, cpi, N_GATES, CHUNK), f32),
               pltpu.VMEM((nseq, l, DN_W), f32), pltpu.VMEM((nseq, l, DN_W), f32),
               pltpu.VMEM((nseq, 2 * n_pairs, pw, pw), f32),
               pltpu.VMEM((nseq, n, 2 * n_pairs, pw, pw), f32),
               pltpu.VMEM((nseq, n, 2 * n_pairs, CHUNK + pw, pw), bf16),
               pltpu.VMEM((nseq, n, 2 * n_pairs, pw), f32)]
    res = pl.pallas_call(
        functools.partial(_dn_kernel, seq_len=l, nseq=nseq, has_s0=s0 is not None, want_state=want_state),
        grid=(b // nseq,),
        in_specs=in_specs,
        out_specs=tuple(out_specs),
        out_shape=tuple(out_shape),
        scratch_shapes=scratch,
        compiler_params=_cparams(("arbitrary",)),
        name=name,
    )(*args)
    return res if want_state else (res[0], None)


def _merge_kernel(oa_ref, od_ref, sga_ref, sgd_ref, x_ref, mod_ref, wa_ref, wd_ref, wo_ref, nffn_ref,
                  wr_ref, br_ref, x1_out, h2_out, comb_out):
    mod = mod_ref[0]
    gate1 = mod[:, 2 * D_MODEL:3 * D_MODEL]
    shift2 = mod[:, 3 * D_MODEL:4 * D_MODEL]
    scale2 = mod[:, 4 * D_MODEL:5 * D_MODEL]
    merged = (sga_ref[...].astype(f32) * _dot(oa_ref[...], wa_ref[...])
              + sgd_ref[...].astype(f32) * _dot(od_ref[...], wd_ref[...]))
    m = _dot(merged.astype(bf16), wo_ref[...])
    x1 = x_ref[...] + gate1 * m
    x1_out[...] = x1
    ms = jnp.mean(x1 * x1, axis=-1, keepdims=True)
    h2 = x1 * lax.rsqrt(ms + EPS) * nffn_ref[...]
    h2 = h2 * (1.0 + scale2) + shift2
    h2_out[...] = h2.astype(bf16)

    h_hi, h_lo = _split2(h2)
    w_hi, w_lo = _split2(wr_ref[...])
    logits = _dot(h_hi, w_hi) + _dot(h_hi, w_lo) + _dot(h_lo, w_hi) + br_ref[...]
    n_live = 24
    lt = jnp.transpose(logits)[0:n_live, :]
    row = lax.broadcasted_iota(jnp.int32, lt.shape, 0)
    big = ROUTER_LANES
    neg = -jnp.inf

    def first_row(hit):
        return jnp.min(jnp.where(hit, row, big), axis=0, keepdims=True)

    is_g = row < N_GROUPS
    gl = jnp.where(is_g, lt, neg)
    gexp = jnp.where(is_g, jnp.exp(gl - gl.max(axis=0, keepdims=True)), 0.0)
    gp = gexp / gexp.sum(axis=0, keepdims=True)
    g_top = gp.max(axis=0, keepdims=True)
    g_idx = first_row(is_g & (gp == g_top))
    lo_row = N_GROUPS + g_idx * EXPERTS_PER_GROUP
    is_e = (row >= lo_row) & (row < lo_row + EXPERTS_PER_GROUP)
    el = jnp.where(is_e, lt, neg)
    eexp = jnp.where(is_e, jnp.exp(el - el.max(axis=0, keepdims=True)), 0.0)
    ep = jnp.where(is_e, eexp / eexp.sum(axis=0, keepdims=True), -1.0)
    p1 = ep.max(axis=0, keepdims=True)
    i1 = first_row(ep == p1)
    ep2 = jnp.where(row == i1, -1.0, ep)
    p2 = ep2.max(axis=0, keepdims=True)
    i2 = first_row(ep2 == p2)
    tot = p1 + p2
    comb_t = (jnp.where(row == i1, p1 / tot * g_top, 0.0) + jnp.where(row == i2, p2 / tot * g_top, 0.0))
    pad = jnp.zeros((ROUTER_LANES - n_live, lt.shape[1]), f32)
    comb_out[...] = jnp.transpose(jnp.concatenate([comb_t, pad], axis=0))


def _merge_call(oa, od, sga, sgd, x2d, mod3, mod_row, wa, wd, wo, nffn, wr, br, seq_len, name):
    t = x2d.shape[0]
    tm = 1024
    tiles_per_seq = max(seq_len // tm, 1)

    def row_spec(w):
        return pl.BlockSpec((tm, w), lambda i: (i, 0))

    return pl.pallas_call(
        _merge_kernel,
        grid=(t // tm,),
        in_specs=[row_spec(ATTN_Q_W), row_spec(DN_W), row_spec(D_MODEL), row_spec(D_MODEL), row_spec(D_MODEL),
                  pl.BlockSpec((1, 1, N_MOD * D_MODEL), lambda i: (mod_row(i // tiles_per_seq), 0, 0)),
                  _const_spec(wa.shape), _const_spec(wd.shape), _const_spec(wo.shape),
                  _const_spec((1, D_MODEL)), _const_spec(wr.shape), _const_spec(br.shape)],
        out_specs=(row_spec(D_MODEL), row_spec(D_MODEL), row_spec(ROUTER_LANES)),
        out_shape=(jax.ShapeDtypeStruct((t, D_MODEL), f32),
                   jax.ShapeDtypeStruct((t, D_MODEL), bf16),
                   jax.ShapeDtypeStruct((t, ROUTER_LANES), f32)),
        compiler_params=_cparams(("arbitrary",)),
        name=name,
    )(oa, od, sga, sgd, x2d, mod3, wa, wd, wo, nffn, wr, br)


def _moe_kernel(h2_ref, comb_ref, x1_ref, mod_ref, wg_ref, wu_ref, wdn_ref, nfin_ref, y_ref, acc_ref):
    g = pl.program_id(1)

    @pl.when(g == 0)
    def _():
        acc_ref[...] = jnp.zeros_like(acc_ref)

    h2 = h2_ref[...]
    comb = comb_ref[...]
    lane = lax.broadcasted_iota(jnp.int32, comb.shape, 1)
    first = N_GROUPS + g * EXPERTS_PER_GROUP
    parts = []
    for e in range(EXPERTS_PER_GROUP):
        hid = _silu(_dot(h2, wg_ref[e])) * _dot(h2, wu_ref[e])
        ce = jnp.sum(jnp.where(lane == first + e, comb, 0.0), axis=-1, keepdims=True)
        parts.append((hid * ce).astype(bf16))
    acc_ref[...] += _dot(jnp.concatenate(parts, axis=1), wdn_ref[0])

    @pl.when(g == N_GROUPS - 1)
    def _():
        gate2 = mod_ref[0][:, 5 * D_MODEL:6 * D_MODEL]
        x2 = x1_ref[...] + gate2 * acc_ref[...]
        ms = jnp.mean(x2 * x2, axis=-1, keepdims=True)
        y_ref[...] = x2 * lax.rsqrt(ms + EPS) * nfin_ref[...]


def _moe_call(h2, comb, x1, mod3, mod_row, wg, wu, wdn, nfin, seq_len, name):
    t = h2.shape[0]
    tm = 1024
    tiles_per_seq = max(seq_len // tm, 1)
    return pl.pallas_call(
        _moe_kernel,
        grid=(t // tm, N_GROUPS),
        in_specs=[pl.BlockSpec((tm, D_MODEL), lambda i, e: (i, 0)),
                  pl.BlockSpec((tm, ROUTER_LANES), lambda i, e: (i, 0)),
                  pl.BlockSpec((tm, D_MODEL), lambda i, e: (i, 0)),
                  pl.BlockSpec((1, 1, N_MOD * D_MODEL), lambda i, e: (mod_row(i // tiles_per_seq), 0, 0)),
                  pl.BlockSpec((EXPERTS_PER_GROUP, D_MODEL, D_FF_EXPERT), lambda i, e: (e, 0, 0)),
                  pl.BlockSpec((EXPERTS_PER_GROUP, D_MODEL, D_FF_EXPERT), lambda i, e: (e, 0, 0)),
                  pl.BlockSpec((1, EXPERTS_PER_GROUP * D_FF_EXPERT, D_MODEL), lambda i, e: (e, 0, 0)),
                  pl.BlockSpec((1, D_MODEL), lambda i, e: (0, 0))],
        out_specs=pl.BlockSpec((tm, D_MODEL), lambda i, e: (i, 0)),
        out_shape=jax.ShapeDtypeStruct((t, D_MODEL), f32),
        scratch_shapes=[pltpu.VMEM((tm, D_MODEL), f32)],
        compiler_params=_cparams(("arbitrary", "arbitrary")),
        name=name,
    )(h2, comb, x1, mod3, wg, wu, wdn, nfin)


def _rope_tables(seq_len):
    t = np.arange(seq_len)
    row = (t // GRID_W).astype(np.float64)
    col = (t % GRID_W).astype(np.float64)
    half = HEAD_DIM // 2
    inv = np.power(ROPE_THETA, -np.arange(0, half, 2, dtype=np.float64) / half)
    d = np.arange(HEAD_DIM)
    freq = inv[d % (half // 2)]
    pos = np.where(d[None, :] < half, row[:, None], col[:, None])
    ang = pos * freq[None, :]
    sign = np.where((d % half) < half // 2, -1.0, 1.0)
    cos = np.tile(np.cos(ang), (1, ATTN_HEADS)).astype(np.float32)
    sin = np.tile(np.sin(ang) * sign[None, :], (1, ATTN_HEADS)).astype(np.float32)
    return jnp.asarray(cos), jnp.asarray(sin)


def _ones_block_diag():
    i = np.arange(LANE_TILE)
    return jnp.asarray((i[:, None] // HEAD_DIM == i[None, :] // HEAD_DIM).astype(np.float32), dtype=bf16)


def _lane_replicator():
    i = np.arange(HEAD_DIM)
    j = np.arange(ATTN_GROUP * HEAD_DIM)
    return jnp.asarray((i[:, None] == j[None, :] % HEAD_DIM).astype(np.float32), dtype=bf16)


def _lane_replicator_cat():
    i = np.arange(ATTN_KV_W)
    j = np.arange(ATTN_GROUP * HEAD_DIM)
    mats = [(i[:, None] == kv * HEAD_DIM + j[None, :] % HEAD_DIM).astype(np.float32) for kv in range(ATTN_KV_HEADS)]
    return jnp.asarray(np.stack(mats), dtype=bf16)


def _trunk(x3, mod3, mod_row, wts, prefix):
    n_seq, seq_len, _ = x3.shape
    t = n_seq * seq_len
    x2d = x3.reshape(t, D_MODEL)
    is_lat = prefix is not None
    rope_tabs = _rope_tables(seq_len) if is_lat else None

    res = _proj_call(
        x2d, mod3, mod_row, wts["nmix"], wts["w_qkv"], wts["w_gates"], wts["wba"], wts["qn"],
        wts["kn"], wts["ones_bd"], rope_tabs, n_seq, seq_len)
    q, kcat, vcat = res[:3]
    k, v = (None, None) if is_lat else (jnp.swapaxes(res[3], 3, 4), jnp.swapaxes(res[4], 3, 4))
    dqkv, ba, bat, dgate, sga, sgd = res[-6:]

    q3 = q.reshape(n_seq, seq_len, ATTN_Q_W)
    kcat3 = kcat.reshape(n_seq, seq_len, ATTN_KV_W)
    vcat3 = vcat.reshape(n_seq, seq_len, ATTN_KV_W)
    if is_lat:
        pk, pv, s0 = prefix
        o_attn = _attn_call(q3, [pk, pv, kcat3, vcat3], wts["rep"], wts["repcat"], 1024, "attn_lat")
    else:
        s0 = None
        o_attn = _attn_call(q3, [kcat3, vcat3], wts["rep"], wts["repcat"], seq_len, "attn_ctx")

    n_chunks = seq_len // CHUNK
    bar4 = bat.reshape(2 * N_GATES, n_seq, n_chunks, CHUNK).transpose(1, 2, 0, 3)
    barp4 = bat[N_GATES:].reshape(N_GATES // 2, 2, n_seq, n_chunks, CHUNK).transpose(2, 3, 0, 1, 4).reshape(
        n_seq, n_chunks, N_GATES // 2, 2 * CHUNK)
    dn_consts = [wts[name] for name in ("convw", "alog_r", "dtb_r", "alog_p", "dtb_p", "dnn",
                                        "ones_bd", "expand")]
    o_dn, state = _dn_call(
        dqkv.reshape(n_seq, seq_len, 3 * DN_W), ba.reshape(n_seq, seq_len, LANE_TILE), bar4, barp4,
        dgate.reshape(n_seq, seq_len, DN_W), s0, dn_consts, not is_lat, "dn_lat" if is_lat else "dn_ctx")

    x1, h2, comb = _merge_call(
        o_attn.reshape(t, ATTN_Q_W), o_dn.reshape(t, DN_W), sga, sgd, x2d, mod3, mod_row,
        wts["wa"], wts["wd"], wts["wo"], wts["nffn"], wts["wr"], wts["br"], seq_len,
        "merge_lat" if is_lat else "merge_ctx")

    y = _moe_call(h2, comb, x1, mod3, mod_row, wts["wg"], wts["wu"], wts["wdn"], wts["nfin"], seq_len,
                  "moe_lat" if is_lat else "moe_ctx")
    return y.reshape(n_seq, seq_len, D_MODEL), k, v, state


def kernel(x_prompt, x_sample, c, cache_attn_k, cache_attn_v, state_delta, c_ctx, w_mod, b_mod, norm_mix, norm_ffn, norm_final, w_in, q_norm, k_norm, conv_w, a_log, dt_bias, dn_norm, w_attn_br, w_dn_br, w_out, w_rg, b_rg, w_re, b_re, w_gate_e, w_up_e, w_down_e):
    layer = 0
    n_lat = x_sample.shape[0]
    w_in_l = w_in[layer]
    ba_lo = OFF_DGATE
    wts = {
        "nmix": norm_mix[layer][None, :],
        "w_qkv": w_in_l[:, :ba_lo].astype(bf16),
        "w_gates": w_in_l[:, ba_lo + 2 * N_GATES:].astype(bf16),
        "wba": jnp.pad(w_in_l[:, ba_lo:ba_lo + 2 * N_GATES], ((0, 0), (0, LANE_TILE - 2 * N_GATES))).astype(bf16),
        "qn": jnp.tile(q_norm[layer], ATTN_HEADS)[None, :],
        "kn": jnp.tile(k_norm[layer], ATTN_KV_HEADS)[None, :],
        "ones_bd": _ones_block_diag(),
        "rep": _lane_replicator(),
        "repcat": _lane_replicator_cat(),
        "convw": conv_w[layer],
        "alog_r": a_log[layer].reshape(N_GATES, 1),
        "dtb_r": dt_bias[layer].reshape(N_GATES, 1),
        "alog_p": jnp.repeat(a_log[layer].reshape(N_GATES // 2, 2), CHUNK, axis=1),
        "dtb_p": jnp.repeat(dt_bias[layer].reshape(N_GATES // 2, 2), CHUNK, axis=1),
        "expand": jnp.asarray(np.repeat(np.eye(LANE_TILE, N_GATES, dtype=np.float32), DN_DK, axis=1), dtype=bf16),
        "dnn": jnp.tile(dn_norm[layer], DN_HEADS)[None, :],
        "wa": w_attn_br[layer].astype(bf16),
        "wd": w_dn_br[layer].astype(bf16),
        "wo": w_out[layer].astype(bf16),
        "nffn": norm_ffn[layer][None, :],
        "wr": jnp.concatenate([w_rg[layer], w_re[layer],
                               jnp.zeros((D_MODEL, ROUTER_LANES - N_GROUPS - N_EXPERTS), f32)], axis=1),
        "br": jnp.concatenate([b_rg[layer], b_re[layer],
                               jnp.zeros((ROUTER_LANES - N_GROUPS - N_EXPERTS,), f32)])[None, :],
        "wg": w_gate_e[layer].astype(bf16),
        "wu": w_up_e[layer].astype(bf16),
        "wdn": w_down_e[layer].reshape(N_GROUPS, EXPERTS_PER_GROUP * D_FF_EXPERT, D_MODEL).astype(bf16),
        "nfin": norm_final[None, :],
    }
    cond8 = jnp.concatenate([c_ctx[None, :], c, jnp.zeros((8 - 1 - n_lat, D_MODEL), f32)], axis=0)
    mod3 = _mod_call(cond8, w_mod[layer], b_mod[layer][None, :])[:, None, :]

    y_prompt, new_k, new_v, new_state = _trunk(x_prompt, mod3, lambda s: 0, wts, None)
    y_sample, _, _, _ = _trunk(x_sample, mod3, lambda s: s + 1, wts,
                               (cache_attn_k, cache_attn_v, state_delta))
    return (y_prompt, y_sample, new_k, new_v, new_state)
```

```python
import functools

import numpy as np
import jax
import jax.numpy as jnp
from jax import lax
from jax.experimental import pallas as pl
from jax.experimental.pallas import tpu as pltpu

f32 = jnp.float32
bf16 = jnp.bfloat16

D_MODEL = 1024
HEAD_DIM = 64
ATTN_HEADS = 8
ATTN_KV_HEADS = 2
ATTN_GROUP = ATTN_HEADS // ATTN_KV_HEADS
GRID_W = 64
ROPE_THETA = 10000.0
DN_HEADS = 8
DN_DK = 64
CHUNK = 64
N_GROUPS = 4
EXPERTS_PER_GROUP = 4
N_EXPERTS = 16
D_FF_EXPERT = 256
N_MOD = 6
EPS = 1e-6

ATTN_Q_W = ATTN_HEADS * HEAD_DIM
ATTN_KV_W = ATTN_KV_HEADS * HEAD_DIM
DN_W = DN_HEADS * DN_DK
N_GATES = 2 * DN_HEADS

OFF_Q = 0
OFF_K = ATTN_Q_W
OFF_V = OFF_K + ATTN_KV_W
OFF_DQKV = OFF_V + ATTN_KV_W
OFF_BA = OFF_DQKV + 3 * DN_W
OFF_DGATE = OFF_BA + 2 * N_GATES
OFF_GATTN = OFF_DGATE + DN_W
OFF_GDN = OFF_GATTN + D_MODEL

LANE_TILE = 128
BF16_ROW_TILE = 16
ROUTER_LANES = LANE_TILE
VMEM_LIMIT = 56 * 1024 * 1024
DN_CHUNKS_PER_ITER = 2

_TRANS_B = (((1,), (1,)), ((), ()))


def _cparams(sem):
    return pltpu.CompilerParams(dimension_semantics=sem, vmem_limit_bytes=VMEM_LIMIT)


def _dot(a, b):
    return jnp.dot(a, b, preferred_element_type=f32)


def _silu(x):
    return x * jax.nn.sigmoid(x)


def _softplus(x):
    return jnp.maximum(x, 0.0) + jnp.log1p(jnp.exp(-jnp.abs(x)))


def _split2(x):
    hi = x.astype(bf16)
    lo = (x - hi.astype(f32)).astype(bf16)
    return hi, lo


def _split3(x):
    a = x.astype(bf16)
    r = x - a.astype(f32)
    b = r.astype(bf16)
    c = (r - b.astype(f32)).astype(bf16)
    return a, b, c


def _group_sumsq_many(arrs, ones_pair):
    pieces = []
    for x in arrs:
        hi, lo = _split2(x * x)
        for s in range(x.shape[1] // LANE_TILE):
            pieces.append(hi[:, s * LANE_TILE:(s + 1) * LANE_TILE])
            pieces.append(lo[:, s * LANE_TILE:(s + 1) * LANE_TILE])
    res = _dot(jnp.concatenate(pieces, axis=0), ones_pair)
    out, off = [], 0
    for x in arrs:
        r = x.shape[0]
        slabs = []
        for s in range(x.shape[1] // LANE_TILE):
            slabs.append(res[off:off + r] + res[off + r:off + 2 * r])
            off += 2 * r
        out.append(slabs[0] if len(slabs) == 1 else jnp.concatenate(slabs, axis=1))
    return out


def _group_sumsq(x, ones_pair):
    return _group_sumsq_many([x], ones_pair)[0]


def _const_spec(shape):
    nd = len(shape)
    return pl.BlockSpec(shape, lambda *_: (0,) * nd, pipeline_mode=pl.Buffered(1))


def _mod_kernel(c_ref, w_ref, b_ref, o_ref):
    c = c_ref[...]
    o_ref[...] = _dot(_silu(c).astype(bf16), w_ref[...].astype(bf16)) + b_ref[...]


def _mod_call(cond8, w_mod, b_mod):
    tn = 1536
    n = w_mod.shape[1]
    return pl.pallas_call(
        _mod_kernel,
        grid=(n // tn,),
        in_specs=[_const_spec((8, D_MODEL)),
                  pl.BlockSpec((D_MODEL, tn), lambda j: (0, j)),
                  pl.BlockSpec((1, tn), lambda j: (0, j))],
        out_specs=pl.BlockSpec((8, tn), lambda j: (0, j)),
        out_shape=jax.ShapeDtypeStruct((8, n), f32),
        compiler_params=_cparams(("arbitrary",)),
        name="mod",
    )(cond8, w_mod, b_mod)


def _rope(x, cos, sin):
    w = x.shape[-1]
    half, quarter = HEAD_DIM // 2, HEAD_DIM // 4
    lane = lax.broadcasted_iota(jnp.int32, x.shape, 1)
    first = (lane % half) < quarter
    swapped = jnp.where(first, pltpu.roll(x, w - quarter, 1), pltpu.roll(x, quarter, 1))
    return x * cos + swapped * sin


def _proj_kernel(*refs, rope, seq_per_tile):
    (x_ref, mod_ref, nmix_ref, wt_ref, qn_ref, kn_ref, ones_ref) = refs[:7]
    pos = 7
    if rope:
        cos_ref, sin_ref = refs[7:9]
        pos = 9
    outs = list(refs[pos:])
    q_out, kcat_out, vcat_out = outs[:3]
    if not rope:
        k_out, v_out = outs[3:5]
        outs = outs[2:]
    (dqkv_out, ba_out, bat_out, dgate_out, sga_out, sgd_out) = outs[3:]

    x = x_ref[...]
    mod = mod_ref[0]
    shift1 = mod[:, 0:D_MODEL]
    scale1 = mod[:, D_MODEL:2 * D_MODEL]
    ms = jnp.mean(x * x, axis=-1, keepdims=True)
    h = x * lax.rsqrt(ms + EPS) * nmix_ref[...]
    h = h * (1.0 + scale1) + shift1
    hb = h.astype(bf16)

    def proj(lo, hi):
        return lax.dot_general(hb, wt_ref[lo:hi, :], _TRANS_B, preferred_element_type=f32)

    aq = proj(OFF_Q, OFF_Q + ATTN_Q_W)
    ak = proj(OFF_K, OFF_K + ATTN_KV_W)
    av = proj(OFF_V, OFF_V + ATTN_KV_W)
    ssq, ssk = _group_sumsq_many([aq, ak], ones_ref[...])
    aq = aq * lax.rsqrt(ssq * (1.0 / HEAD_DIM) + EPS) * qn_ref[...]
    ak = ak * lax.rsqrt(ssk * (1.0 / HEAD_DIM) + EPS) * kn_ref[...]
    if not rope:
        tm = x.shape[0]
        seq = tm // seq_per_tile
        akt = jnp.transpose(ak)
        avt = jnp.transpose(av)
        for s in range(seq_per_tile):
            for kv in range(ATTN_KV_HEADS):
                k_out[s, 0, kv] = akt[kv * HEAD_DIM:(kv + 1) * HEAD_DIM, s * seq:(s + 1) * seq]
                v_out[s, 0, kv] = avt[kv * HEAD_DIM:(kv + 1) * HEAD_DIM, s * seq:(s + 1) * seq]
    else:
        aq = _rope(aq, cos_ref[...], sin_ref[...])
        ak = _rope(ak, cos_ref[:, 0:ATTN_KV_W], sin_ref[:, 0:ATTN_KV_W])
    q_out[...] = (aq * (HEAD_DIM ** -0.5)).astype(bf16)
    kcat_out[...] = ak.astype(bf16)
    vcat_out[...] = av.astype(bf16)

    dqkv_out[...] = proj(OFF_DQKV, OFF_BA).astype(bf16)
    ba = proj(OFF_BA, OFF_BA + LANE_TILE)
    ba_out[...] = ba
    bat_out[...] = jnp.transpose(ba)[0:2 * N_GATES, :]
    dgate_out[...] = proj(OFF_DGATE, OFF_GATTN).astype(bf16)
    sga_out[...] = jax.nn.sigmoid(proj(OFF_GATTN, OFF_GDN)).astype(bf16)
    sgd_out[...] = jax.nn.sigmoid(proj(OFF_GDN, OFF_GDN + D_MODEL)).astype(bf16)


def _proj_call(x2d, mod3, mod_row, nmix, w_t, qn, kn, ones_bd, rope_tabs, n_seq, seq_len):
    t = x2d.shape[0]
    tm = 512
    rope = rope_tabs is not None
    tiles_per_seq = max(seq_len // tm, 1)
    seq_per_tile = max(tm // seq_len, 1)
    in_specs = [pl.BlockSpec((tm, D_MODEL), lambda i: (i, 0)),
                pl.BlockSpec((1, 1, N_MOD * D_MODEL), lambda i: (mod_row(i // tiles_per_seq), 0, 0)),
                _const_spec((1, D_MODEL)),
                _const_spec(w_t.shape),
                _const_spec((1, ATTN_Q_W)),
                _const_spec((1, ATTN_KV_W)),
                _const_spec((LANE_TILE, LANE_TILE))]
    args = [x2d, mod3, nmix, w_t, qn, kn, ones_bd]

    def row_spec(w):
        return pl.BlockSpec((tm, w), lambda i: (i, 0))

    out_shape = [jax.ShapeDtypeStruct((t, ATTN_Q_W), bf16),
                 jax.ShapeDtypeStruct((t, ATTN_KV_W), bf16),
                 jax.ShapeDtypeStruct((t, ATTN_KV_W), bf16)]
    out_specs = [row_spec(ATTN_Q_W), row_spec(ATTN_KV_W), row_spec(ATTN_KV_W)]
    if rope:
        in_specs += [pl.BlockSpec((tm, ATTN_Q_W), lambda i: (i % tiles_per_seq, 0))] * 2
        args += list(rope_tabs)
    else:
        kv_shape = jax.ShapeDtypeStruct((n_seq, 1, ATTN_KV_HEADS, HEAD_DIM, seq_len), f32)
        kv_spec = pl.BlockSpec((seq_per_tile, 1, ATTN_KV_HEADS, HEAD_DIM, seq_len), lambda i: (i, 0, 0, 0, 0))
        out_shape += [kv_shape, kv_shape]
        out_specs += [kv_spec, kv_spec]
    out_shape += [jax.ShapeDtypeStruct((t, 3 * DN_W), bf16),
                  jax.ShapeDtypeStruct((t, LANE_TILE), f32),
                  jax.ShapeDtypeStruct((2 * N_GATES, t), f32),
                  jax.ShapeDtypeStruct((t, DN_W), bf16),
                  jax.ShapeDtypeStruct((t, D_MODEL), bf16),
                  jax.ShapeDtypeStruct((t, D_MODEL), bf16)]
    out_specs += [row_spec(3 * DN_W), row_spec(LANE_TILE),
                  pl.BlockSpec((2 * N_GATES, tm), lambda i: (0, i)),
                  row_spec(DN_W), row_spec(D_MODEL), row_spec(D_MODEL)]
    return pl.pallas_call(
        functools.partial(_proj_kernel, rope=rope, seq_per_tile=seq_per_tile),
        grid=(t // tm,),
        in_specs=in_specs,
        out_specs=tuple(out_specs),
        out_shape=tuple(out_shape),
        compiler_params=_cparams(("arbitrary",)),
        name="proj_lat" if rope else "proj_ctx",
    )(*args)


def _attn_kernel(*refs, n_sets, scores_first):
    q_ref = refs[0]
    kv_refs = refs[1:1 + 2 * n_sets]
    rep_ref, repcat_ref = refs[1 + 2 * n_sets:3 + 2 * n_sets]
    o_ref = refs[-1]
    width = ATTN_GROUP * HEAD_DIM
    blk = lax.broadcasted_iota(jnp.int32, (1, width), 1) // HEAD_DIM

    def head_slab(ref, kv):
        if len(ref.shape) == 5:
            return _dot(ref[0, 0, kv].astype(bf16), rep_ref[...])
        return _dot(ref[0], repcat_ref[kv])

    def scores_of(kv, g, q, k4):
        qg = jnp.where(blk == g, q, jnp.zeros((), bf16))
        return [lax.dot_general(qg, k, _TRANS_B, preferred_element_type=f32) for k in k4]

    def head_out(scores, v4):
        m = scores[0].max(axis=-1, keepdims=True)
        for s in scores[1:]:
            m = jnp.maximum(m, s.max(axis=-1, keepdims=True))
        probs = [jnp.exp(s - m) for s in scores]
        denom = probs[0].sum(axis=-1, keepdims=True)
        for p in probs[1:]:
            denom = denom + p.sum(axis=-1, keepdims=True)
        og = _dot(probs[0].astype(bf16), v4[0])
        for p, v in zip(probs[1:], v4[1:]):
            og = og + _dot(p.astype(bf16), v)
        return og * (1.0 / denom)

    qs, k4s, v4s = [], [], []
    for kv in range(ATTN_KV_HEADS):
        qs.append(q_ref[0, :, kv * width:(kv + 1) * width])
        k4s.append([head_slab(kv_refs[2 * s], kv).astype(bf16) for s in range(n_sets)])
        v4s.append([head_slab(kv_refs[2 * s + 1], kv).astype(bf16) for s in range(n_sets)])
    heads = [(kv, g) for kv in range(ATTN_KV_HEADS) for g in range(ATTN_GROUP)]
    if scores_first:
        all_scores = [scores_of(kv, g, qs[kv], k4s[kv]) for kv, g in heads]
    accs = [jnp.zeros((qs[0].shape[0], width), f32) for _ in range(ATTN_KV_HEADS)]
    for i, (kv, g) in enumerate(heads):
        sc = all_scores[i] if scores_first else scores_of(kv, g, qs[kv], k4s[kv])
        accs[kv] = jnp.where(blk == g, head_out(sc, v4s[kv]), accs[kv])
    for kv in range(ATTN_KV_HEADS):
        o_ref[0, :, kv * width:(kv + 1) * width] = accs[kv].astype(o_ref.dtype)


def _attn_call(q3, kv_sets, rep, repcat, tq, name):
    b, l, _ = q3.shape
    in_specs = [pl.BlockSpec((1, tq, ATTN_Q_W), lambda i, t: (i, t, 0))]
    args = [q3]
    for arr in kv_sets:
        nd = arr.ndim
        blk = (1,) + arr.shape[1:]
        in_specs.append(pl.BlockSpec(blk, lambda i, t, nd=nd: (i,) + (0,) * (nd - 1)))
        args.append(arr)
    in_specs += [_const_spec(rep.shape), _const_spec(repcat.shape)]
    args += [rep, repcat]
    return pl.pallas_call(
        functools.partial(_attn_kernel, n_sets=len(kv_sets) // 2, scores_first=len(kv_sets) == 2),
        grid=(b, l // tq),
        in_specs=in_specs,
        out_specs=pl.BlockSpec((1, tq, ATTN_Q_W), lambda i, t: (i, t, 0)),
        out_shape=jax.ShapeDtypeStruct((b, l, ATTN_Q_W), bf16),
        compiler_params=_cparams(("arbitrary", "arbitrary")),
        name=name,
    )(*args)


def _dn_kernel(*refs, seq_len, nseq, has_s0, want_state):
    it = iter(refs)
    dqkv_ref = next(it)
    ba_ref = next(it)
    bar_ref = next(it)
    barp_ref = next(it)
    dgate_ref = next(it)
    s0_ref = next(it) if has_s0 else None
    convw_ref = next(it)
    alog_r = next(it)
    dtb_r = next(it)
    alog_p = next(it)
    dtb_p = next(it)
    dnn_ref = next(it)
    ones_ref = next(it)
    expand_ref = next(it)
    o_ref = next(it)
    st_ref = next(it) if want_state else None
    q_s, k_s, v_s, kt_s, gx_s, bx_s, grp_s, gr_s, of_s, ob_s, s_s, cu_s, qn_s, el_s = it
    cpi = DN_CHUNKS_PER_ITER

    n = seq_len // CHUNK
    c = CHUNK
    pw = 2 * DN_DK
    n_pairs = DN_HEADS // 2
    rp = lax.broadcasted_iota(jnp.int32, (c, pw), 0)
    cp_raw = lax.broadcasted_iota(jnp.int32, (c, pw), 1)
    cp = cp_raw % c
    triu_pad = jnp.where((rp <= cp_raw) & (cp_raw < c), 1.0, 0.0).astype(bf16)
    tril_pad = jnp.where((rp >= cp_raw) & (cp_raw < c), 1.0, 0.0).astype(bf16)
    lane_t = lax.broadcasted_iota(jnp.int32, (1, LANE_TILE), 1)
    lower_p = rp >= cp
    upper_p = rp <= cp
    eye_p = jnp.where(rp == cp, 1.0, 0.0)
    same8 = (rp // 8) == (cp // 8)
    rb2 = lax.broadcasted_iota(jnp.int32, (pw, pw), 0)
    cb2 = lax.broadcasted_iota(jnp.int32, (pw, pw), 1)
    bd_mask = (rb2 // c) == (cb2 // c)
    bd_triu = jnp.where(bd_mask & (rb2 % c <= cb2 % c), 1.0, 0.0).astype(bf16)
    bd_tril = jnp.where(bd_mask & (rb2 % c >= cb2 % c), 1.0, 0.0).astype(bf16)
    sub16 = lax.broadcasted_iota(jnp.int32, (N_GATES, 1), 0)
    sub8 = lax.broadcasted_iota(jnp.int32, (2 * n_pairs, 1), 0)
    sub_pw = lax.broadcasted_iota(jnp.int32, (pw, 1), 0)
    row_c = lax.broadcasted_iota(jnp.int32, (c, 1), 0)

    zero_b = jnp.zeros((), bf16)
    rq = rb2 % c
    cq = cb2 % c
    bd_same8 = bd_mask & ((rq // 8) == (cq // 8))
    bd_off16 = bd_mask & ((rq // 16) == (cq // 16)) & ((rq // 8) != (cq // 8))
    bd_off32 = bd_mask & ((rq // 32) == (cq // 32)) & ((rq // 16) != (cq // 16))
    bd_off64 = bd_mask & ((rq // 32) != (cq // 32))

    def bdiag(x):
        xb = x.astype(bf16)
        return jnp.where(bd_mask, jnp.concatenate([xb, xb], axis=0), zero_b)

    def conv_silu(sq, ci):
        r0 = pl.multiple_of(ci * c, c)
        xb = dqkv_ref[sq, pl.ds(r0, c), :].astype(f32)
        bt = BF16_ROW_TILE
        p0 = pl.multiple_of(jnp.maximum(r0 - bt, 0), bt)
        n0 = pl.multiple_of(jnp.minimum(r0 + c, seq_len - bt), bt)
        prev = dqkv_ref[sq, pl.ds(p0, bt), :].astype(f32)[bt - 1:bt, :]
        nxt = dqkv_ref[sq, pl.ds(n0, bt), :].astype(f32)[0:1, :]
        prev = jnp.where(ci > 0, prev, 0.0)
        nxt = jnp.where(ci < n - 1, nxt, 0.0)
        xm = jnp.where(row_c == 0, prev, pltpu.roll(xb, 1, 0))
        xp = jnp.where(row_c == c - 1, nxt, pltpu.roll(xb, c - 1, 0))
        w = convw_ref[...]
        return _silu(xm * w[0:1, :] + xb * w[1:2, :] + xp * w[2:3, :])

    def beta_col(sq, ci):
        ba = ba_ref[sq, pl.ds(pl.multiple_of(ci * c, c), c), :]
        return jnp.where(lane_t < N_GATES, jax.nn.sigmoid(ba), 0.0)

    def stacked_sums(parts_per_chunk, tri_a, tri_b):
        rows = parts_per_chunk[0][0].shape[0]
        stack = jnp.concatenate([p for parts in parts_per_chunk for p in parts], axis=0)
        res_a = _dot(stack, tri_a)
        res_b = _dot(stack, tri_b)
        out = []
        for s, parts in enumerate(parts_per_chunk):
            base = s * len(parts) * rows
            blocks = [slice(base + k * rows, base + (k + 1) * rows) for k in range(len(parts))]
            out.append((sum(res_a[b] for b in blocks), sum(res_b[b] for b in blocks)))
        return out

    def pre_phase_a(sq, it):
        cis = [it * cpi + s for s in range(cpi)]
        act = [conv_silu(sq, ci) for ci in cis]
        betas = [beta_col(sq, ci) for ci in cis]
        g_r = [_split3(-jnp.exp(alog_r[...]) * _softplus(bar_ref[sq, ci][N_GATES:2 * N_GATES, :] + dtb_r[...]))
               for ci in cis]
        g_p = [_split3(-jnp.exp(alog_p[...]) * _softplus(barp_ref[sq, ci] + dtb_p[...])) for ci in cis]
        sumsq = _group_sumsq_many([a[:, 0:2 * DN_W] for a in act], ones_ref[...])
        gr = [jnp.where(sub16 < DN_HEADS, f, b) for f, b in stacked_sums(g_r, triu_pad, tril_pad)]
        grp = [jnp.where(sub8 < n_pairs, f, b) for f, b in stacked_sums(g_p, bd_triu, bd_tril)]
        return act, sumsq, betas, gr, grp

    def pre_phase_b(pa):
        _, _, betas, gr, _ = pa
        zpad = jnp.zeros((LANE_TILE - N_GATES, LANE_TILE), f32)
        gcol = [jnp.transpose(jnp.concatenate([g16, zpad], axis=0))[0:c, :] for g16 in gr]
        parts = [p for g in gcol for p in _split3(g)] + [p for b in betas for p in _split2(b)]
        res = _dot(jnp.concatenate(parts, axis=0), expand_ref[...])
        gx = [sum(res[(3 * s + k) * c:(3 * s + k + 1) * c] for k in range(3)) for s in range(cpi)]
        off = 3 * cpi * c
        bx = [sum(res[off + (2 * s + k) * c:off + (2 * s + k + 1) * c] for k in range(2)) for s in range(cpi)]
        return gx, bx

    def pre_phase_c(pa, pb, slot):
        act, sumsq, _, gr, grp = pa
        gx, bx = pb
        for s in range(cpi):
            rs = slice(s * c, (s + 1) * c)
            a = act[s]
            scale = lax.rsqrt(sumsq[s] + EPS)
            q_s[slot, rs, :] = a[:, 0:DN_W] * scale[:, 0:DN_W] * (DN_DK ** -0.5)
            kn = a[:, DN_W:2 * DN_W] * scale[:, DN_W:2 * DN_W]
            k_s[slot, rs, :] = kn
            v_s[slot, rs, :] = a[:, 2 * DN_W:3 * DN_W]
            kt_s[slot, s] = jnp.transpose(jnp.concatenate([kn, kn], axis=0)).astype(bf16)
            gx_s[slot, rs, :] = gx[s]
            bx_s[slot, rs, :] = bx[s]
            gr_s[slot, s] = gr[s][:, 0:c]
            grp_s[slot, s] = grp[s]

    zero_blk = jnp.zeros((DN_DK, DN_DK), f32)
    for sq in range(nseq):
        for d in range(2):
            for p in range(n_pairs):
                if has_s0:
                    top = jnp.concatenate([s0_ref[sq, 0, d, 2 * p], zero_blk], axis=1)
                    bot = jnp.concatenate([zero_blk, s0_ref[sq, 0, d, 2 * p + 1]], axis=1)
                    s_s[sq, d * n_pairs + p] = jnp.concatenate([top, bot], axis=0)
                else:
                    s_s[sq, d * n_pairs + p] = jnp.zeros((pw, pw), f32)

    def each(fn, *lists):
        return [fn(*vals) for vals in zip(*lists)]

    def fused_half(sq, it, slot, nxt):
        pa = pre_phase_a(nxt[0], nxt[1]) if nxt is not None else None
        qs, ks, vs, bx, eg, decay, strict, ktd, kq, where_to = [], [], [], [], [], [], [], [], [], []
        for sub in range(cpi):
            ci = it * cpi + sub
            r0 = pl.multiple_of(ci * c, c)
            rs = slice(sub * c, (sub + 1) * c)
            grp = grp_s[slot, sub]
            gr16 = gr_s[slot, sub]
            for p in range(n_pairs):
                sl = slice(p * pw, (p + 1) * pw)
                q_p = q_s[slot, rs, sl]
                k_p = k_s[slot, rs, sl]
                v_p = v_s[slot, rs, sl]
                kt = kt_s[slot, sub, p * pw:(p + 1) * pw, :]
                kq_p = _dot(jnp.concatenate([k_p, q_p], axis=0).astype(bf16),
                            jnp.where(bd_mask, kt, jnp.zeros((), bf16)))
                for d in range(2):
                    pp = d * n_pairs + p
                    j0 = d * DN_HEADS + 2 * p
                    xl = slice(pp * pw, (pp + 1) * pw)
                    g_end = gr16[:, c - 1:c] if d == 0 else gr16[:, 0:1]
                    e_rest_t = jnp.exp(g_end - gr16)
                    incl = lower_p if d == 0 else upper_p
                    gx = gx_s[slot, rs, xl]
                    g_last = gx[c - 1:c, :] if d == 0 else gx[0:1, :]
                    el_s[sq, ci, pp:pp + 1, :] = jnp.exp(g_last)
                    qs.append(q_p)
                    ks.append(k_p)
                    vs.append(v_p)
                    kq.append(kq_p)
                    bx.append(bx_s[slot, rs, xl])
                    eg.append(jnp.exp(gx))
                    diff = gx - grp[pp:pp + 1, :]
                    decay.append(jnp.where(incl, jnp.exp(jnp.where(incl, diff, 0.0)), 0.0))
                    strict.append((rp > cp) if d == 0 else (rp < cp))
                    fac = jnp.where(sub_pw < c, e_rest_t[j0:j0 + 1, :], e_rest_t[j0 + 1:j0 + 2, :])
                    ktd.append((kt[:, 0:c].astype(f32) * fac).astype(bf16))
                    where_to.append((ci, pp, r0, sl, d))

        pb = pre_phase_b(pa) if pa is not None else None
        lmat = each(lambda m, x, b, dc: jnp.where(m, x[0:c] * b * dc, 0.0), strict, kq, bx, decay)
        amat = each(lambda x, dc: (x[c:2 * c] * dc).astype(bf16), kq, decay)
        l2 = each(lambda l: jnp.concatenate([l.astype(bf16)] * 2, axis=0), lmat)
        d1 = each(lambda l: jnp.where(same8, l, 0.0), lmat)
        d2 = each(lambda a, ll: _dot(a.astype(bf16), jnp.where(bd_same8, ll, zero_b)).astype(bf16), d1, l2)
        tmat = each(lambda a: eye_p - a, d1)
        tmat = each(lambda t, a: t + _dot(a, bdiag(t)), tmat, d2)
        d4 = each(lambda a: _dot(a, bdiag(a)).astype(bf16), d2)
        tmat = each(lambda t, a: t + _dot(a, bdiag(t)), tmat, d4)
        for bd_off in (bd_off16, bd_off32, bd_off64):
            tb = each(lambda t: t.astype(bf16), tmat)
            tl = each(lambda t, ll: _dot(t, jnp.where(bd_off, ll, zero_b)).astype(bf16), tb, l2)
            tmat = each(lambda t, a, b: t - _dot(a, bdiag(b)), tmat, tl, tb)
        uw = each(lambda t, v, k, b, e: _dot(t.astype(bf16), jnp.concatenate([bdiag(v * b), bdiag(k * (b * e))], axis=1)),
                  tmat, vs, ks, bx, eg)
        ku_kw = each(lambda kd, y: _dot(kd, y.astype(bf16)), ktd, uw)
        au_aw = each(lambda a, y: _dot(a, jnp.concatenate([bdiag(y[:, 0:pw]), bdiag(y[:, pw:2 * pw])], axis=1)),
                     amat, uw)
        if pa is not None:
            pre_phase_c(pa, pb, nxt[2])
        for (ci, pp, r0, sl, d), q_p, e, kk, aa in zip(where_to, qs, eg, ku_kw, au_aw):
            cu_s[sq, ci, pp] = jnp.where(bd_mask, kk[:, 0:pw], 0.0)
            qn_s[sq, ci, pp, 0:c] = (q_p * e - aa[:, pw:2 * pw]).astype(bf16)
            qn_s[sq, ci, pp, c:c + pw] = jnp.where(bd_mask, -kk[:, pw:2 * pw], 0.0).astype(bf16)
            dst = of_s if d == 0 else ob_s
            dst[sq, pl.ds(r0, c), sl] = aa[:, 0:pw]

    n_it = n // cpi
    first = pre_phase_a(0, 0)
    pre_phase_c(first, pre_phase_b(first), 0)
    if nseq * n_it <= 4:
        items = [(sq, it_) for sq in range(nseq) for it_ in range(n_it)]
        for k, (sq, it_) in enumerate(items):
            nxt = items[k + 1] + ((k + 1) % 2,) if k + 1 < len(items) else None
            fused_half(sq, it_, k % 2, nxt)
    else:
        assert nseq == 1 and n_it % 2 == 0

        def pipe_body(j, carry):
            fused_half(0, 2 * j, 0, (0, 2 * j + 1, 1))
            fused_half(0, 2 * j + 1, 1, (0, 2 * j + 2, 0))
            return carry

        lax.fori_loop(0, n_it // 2 - 1, pipe_body, 0)
        fused_half(0, n_it - 2, 0, (0, n_it - 1, 1))
        fused_half(0, n_it - 1, 1, None)

    def scan_body(i, carry):
        cis = [i, n - 1 - i]
        pairs = [(sq, d, p) for sq in range(nseq) for d in range(2) for p in range(n_pairs)]
        s_old = [s_s[sq, d * n_pairs + p] for sq, d, p in pairs]
        s_b = [s.astype(bf16) for s in s_old]
        both = [_dot(qn_s[sq, cis[d], d * n_pairs + p], sb) for (sq, d, p), sb in zip(pairs, s_b)]
        for (sq, d, p), s, r in zip(pairs, s_old, both):
            pp = d * n_pairs + p
            r0 = pl.multiple_of(cis[d] * c, c)
            dst = of_s if d == 0 else ob_s
            dst[sq, pl.ds(r0, c), p * pw:(p + 1) * pw] += r[0:c]
            s_s[sq, pp] = s * el_s[sq, cis[d], pp:pp + 1, :] + r[c:c + pw] + cu_s[sq, cis[d], pp]
        return carry

    lax.fori_loop(0, n, scan_body, 0)

    rb = 128
    for sq in range(nseq):
        def post_body(bi, carry, sq=sq):
            r0 = pl.multiple_of(bi * rb, rb)
            o = of_s[sq, pl.ds(r0, rb), :] + ob_s[sq, pl.ds(r0, rb), :]
            o = o * lax.rsqrt(_group_sumsq(o, ones_ref[...]) * (1.0 / DN_DK) + EPS) * dnn_ref[...]
            gate = dgate_ref[sq, pl.ds(r0, rb), :].astype(f32)
            o_ref[sq, pl.ds(r0, rb), :] = (o * _silu(gate)).astype(o_ref.dtype)
            return carry

        lax.fori_loop(0, seq_len // rb, post_body, 0)

    if want_state:
        for sq in range(nseq):
            for d in range(2):
                for p in range(n_pairs):
                    s = s_s[sq, d * n_pairs + p]
                    st_ref[sq, 0, d, 2 * p] = s[0:DN_DK, 0:DN_DK]
                    st_ref[sq, 0, d, 2 * p + 1] = s[DN_DK:pw, DN_DK:pw]


def _dn_call(dqkv3, ba3, bar4, barp4, dgate3, s0, consts, want_state, name):
    b, l, _ = dqkv3.shape
    n = l // CHUNK
    n_pairs = DN_HEADS // 2
    pw = 2 * DN_DK
    cpi = DN_CHUNKS_PER_ITER
    nseq = 2 if (n // cpi <= 2 and b % 2 == 0) else 1
    state_blk = (nseq, 1, 2, DN_HEADS, DN_DK, DN_DK)
    state_spec = pl.BlockSpec(state_blk, lambda i: (i, 0, 0, 0, 0, 0))
    in_specs = [pl.BlockSpec((nseq, l, 3 * DN_W), lambda i: (i, 0, 0)),
                pl.BlockSpec((nseq, l, LANE_TILE), lambda i: (i, 0, 0)),
                pl.BlockSpec((nseq, n, 2 * N_GATES, CHUNK), lambda i: (i, 0, 0, 0)),
                pl.BlockSpec((nseq, n, 2 * n_pairs, pw), lambda i: (i, 0, 0, 0)),
                pl.BlockSpec((nseq, l, DN_W), lambda i: (i, 0, 0))]
    args = [dqkv3, ba3, bar4, barp4, dgate3]
    if s0 is not None:
        in_specs.append(state_spec)
        args.append(s0)
    in_specs += [_const_spec(a.shape) for a in consts]
    args += consts
    out_shape = [jax.ShapeDtypeStruct((b, l, DN_W), bf16)]
    out_specs = [pl.BlockSpec((nseq, l, DN_W), lambda i: (i, 0, 0))]
    if want_state:
        out_shape.append(jax.ShapeDtypeStruct((b,) + state_blk[1:], f32))
        out_specs.append(state_spec)
    rows = cpi * CHUNK
    scratch = [pltpu.VMEM((2, rows, DN_W), f32), pltpu.VMEM((2, rows, DN_W), f32),
               pltpu.VMEM((2, rows, DN_W), f32),
               pltpu.VMEM((2, cpi, DN_W, pw), bf16),
               pltpu.VMEM((2, rows, N_GATES * DN_DK), f32), pltpu.VMEM((2, rows, N_GATES * DN_DK), f32),
               pltpu.VMEM((2, cpi, 2 * n_pairs, pw), f32), pltpu.VMEM((2, cpi, N_GATES, CHUNK), f32),
               pltpu.VMEM((nseq, l, DN_W), f32), pltpu.VMEM((nseq, l, DN_W), f32),
               pltpu.VMEM((nseq, 2 * n_pairs, pw, pw), f32),
               pltpu.VMEM((nseq, n, 2 * n_pairs, pw, pw), f32),
               pltpu.VMEM((nseq, n, 2 * n_pairs, CHUNK + pw, pw), bf16),
               pltpu.VMEM((nseq, n, 2 * n_pairs, pw), f32)]
    res = pl.pallas_call(
        functools.partial(_dn_kernel, seq_len=l, nseq=nseq, has_s0=s0 is not None, want_state=want_state),
        grid=(b // nseq,),
        in_specs=in_specs,
        out_specs=tuple(out_specs),
        out_shape=tuple(out_shape),
        scratch_shapes=scratch,
        compiler_params=_cparams(("arbitrary",)),
        name=name,
    )(*args)
    return res if want_state else (res[0], None)


def _merge_kernel(oa_ref, od_ref, sga_ref, sgd_ref, x_ref, mod_ref, wa_ref, wd_ref, wo_ref, nffn_ref,
                  wr_ref, br_ref, x1_out, h2_out, comb_out):
    mod = mod_ref[0]
    gate1 = mod[:, 2 * D_MODEL:3 * D_MODEL]
    shift2 = mod[:, 3 * D_MODEL:4 * D_MODEL]
    scale2 = mod[:, 4 * D_MODEL:5 * D_MODEL]
    merged = (sga_ref[...].astype(f32) * _dot(oa_ref[...], wa_ref[...])
              + sgd_ref[...].astype(f32) * _dot(od_ref[...], wd_ref[...]))
    m = _dot(merged.astype(bf16), wo_ref[...])
    x1 = x_ref[...] + gate1 * m
    x1_out[...] = x1
    ms = jnp.mean(x1 * x1, axis=-1, keepdims=True)
    h2 = x1 * lax.rsqrt(ms + EPS) * nffn_ref[...]
    h2 = h2 * (1.0 + scale2) + shift2
    h2_out[...] = h2.astype(bf16)

    h_hi, h_lo = _split2(h2)
    w_hi, w_lo = _split2(wr_ref[...])
    logits = _dot(h_hi, w_hi) + _dot(h_hi, w_lo) + _dot(h_lo, w_hi) + br_ref[...]
    n_live = 24
    lt = jnp.transpose(logits)[0:n_live, :]
    row = lax.broadcasted_iota(jnp.int32, lt.shape, 0)
    big = ROUTER_LANES
    neg = -jnp.inf

    def first_row(hit):
        return jnp.min(jnp.where(hit, row, big), axis=0, keepdims=True)

    is_g = row < N_GROUPS
    gl = jnp.where(is_g, lt, neg)
    gexp = jnp.where(is_g, jnp.exp(gl - gl.max(axis=0, keepdims=True)), 0.0)
    gp = gexp / gexp.sum(axis=0, keepdims=True)
    g_top = gp.max(axis=0, keepdims=True)
    g_idx = first_row(is_g & (gp == g_top))
    lo_row = N_GROUPS + g_idx * EXPERTS_PER_GROUP
    is_e = (row >= lo_row) & (row < lo_row + EXPERTS_PER_GROUP)
    el = jnp.where(is_e, lt, neg)
    eexp = jnp.where(is_e, jnp.exp(el - el.max(axis=0, keepdims=True)), 0.0)
    ep = jnp.where(is_e, eexp / eexp.sum(axis=0, keepdims=True), -1.0)
    p1 = ep.max(axis=0, keepdims=True)
    i1 = first_row(ep == p1)
    ep2 = jnp.where(row == i1, -1.0, ep)
    p2 = ep2.max(axis=0, keepdims=True)
    i2 = first_row(ep2 == p2)
    tot = p1 + p2
    comb_t = (jnp.where(row == i1, p1 / tot * g_top, 0.0) + jnp.where(row == i2, p2 / tot * g_top, 0.0))
    pad = jnp.zeros((ROUTER_LANES - n_live, lt.shape[1]), f32)
    comb_out[...] = jnp.transpose(jnp.concatenate([comb_t, pad], axis=0))


def _merge_call(oa, od, sga, sgd, x2d, mod3, mod_row, wa, wd, wo, nffn, wr, br, seq_len, name):
    t = x2d.shape[0]
    tm = 1024
    tiles_per_seq = max(seq_len // tm, 1)

    def row_spec(w):
        return pl.BlockSpec((tm, w), lambda i: (i, 0))

    return pl.pallas_call(
        _merge_kernel,
        grid=(t // tm,),
        in_specs=[row_spec(ATTN_Q_W), row_spec(DN_W), row_spec(D_MODEL), row_spec(D_MODEL), row_spec(D_MODEL),
                  pl.BlockSpec((1, 1, N_MOD * D_MODEL), lambda i: (mod_row(i // tiles_per_seq), 0, 0)),
                  _const_spec(wa.shape), _const_spec(wd.shape), _const_spec(wo.shape),
                  _const_spec((1, D_MODEL)), _const_spec(wr.shape), _const_spec(br.shape)],
        out_specs=(row_spec(D_MODEL), row_spec(D_MODEL), row_spec(ROUTER_LANES)),
        out_shape=(jax.ShapeDtypeStruct((t, D_MODEL), f32),
                   jax.ShapeDtypeStruct((t, D_MODEL), bf16),
                   jax.ShapeDtypeStruct((t, ROUTER_LANES), f32)),
        compiler_params=_cparams(("arbitrary",)),
        name=name,
    )(oa, od, sga, sgd, x2d, mod3, wa, wd, wo, nffn, wr, br)


def _moe_kernel(h2_ref, comb_ref, x1_ref, mod_ref, wg_ref, wu_ref, wdn_ref, nfin_ref, y_ref, acc_ref):
    g = pl.program_id(1)

    @pl.when(g == 0)
    def _():
        acc_ref[...] = jnp.zeros_like(acc_ref)

    h2 = h2_ref[...]
    comb = comb_ref[...]
    lane = lax.broadcasted_iota(jnp.int32, comb.shape, 1)
    first = N_GROUPS + g * EXPERTS_PER_GROUP
    parts = []
    for e in range(EXPERTS_PER_GROUP):
        hid = _silu(_dot(h2, wg_ref[e])) * _dot(h2, wu_ref[e])
        ce = jnp.sum(jnp.where(lane == first + e, comb, 0.0), axis=-1, keepdims=True)
        parts.append((hid * ce).astype(bf16))
    acc_ref[...] += _dot(jnp.concatenate(parts, axis=1), wdn_ref[0])

    @pl.when(g == N_GROUPS - 1)
    def _():
        gate2 = mod_ref[0][:, 5 * D_MODEL:6 * D_MODEL]
        x2 = x1_ref[...] + gate2 * acc_ref[...]
        ms = jnp.mean(x2 * x2, axis=-1, keepdims=True)
        y_ref[...] = x2 * lax.rsqrt(ms + EPS) * nfin_ref[...]


def _moe_call(h2, comb, x1, mod3, mod_row, wg, wu, wdn, nfin, seq_len, name):
    t = h2.shape[0]
    tm = 1024
    tiles_per_seq = max(seq_len // tm, 1)
    return pl.pallas_call(
        _moe_kernel,
        grid=(t // tm, N_GROUPS),
        in_specs=[pl.BlockSpec((tm, D_MODEL), lambda i, e: (i, 0)),
                  pl.BlockSpec((tm, ROUTER_LANES), lambda i, e: (i, 0)),
                  pl.BlockSpec((tm, D_MODEL), lambda i, e: (i, 0)),
                  pl.BlockSpec((1, 1, N_MOD * D_MODEL), lambda i, e: (mod_row(i // tiles_per_seq), 0, 0)),
                  pl.BlockSpec((EXPERTS_PER_GROUP, D_MODEL, D_FF_EXPERT), lambda i, e: (e, 0, 0)),
                  pl.BlockSpec((EXPERTS_PER_GROUP, D_MODEL, D_FF_EXPERT), lambda i, e: (e, 0, 0)),
                  pl.BlockSpec((1, EXPERTS_PER_GROUP * D_FF_EXPERT, D_MODEL), lambda i, e: (e, 0, 0)),
                  pl.BlockSpec((1, D_MODEL), lambda i, e: (0, 0))],
        out_specs=pl.BlockSpec((tm, D_MODEL), lambda i, e: (i, 0)),
        out_shape=jax.ShapeDtypeStruct((t, D_MODEL), f32),
        scratch_shapes=[pltpu.VMEM((tm, D_MODEL), f32)],
        compiler_params=_cparams(("arbitrary", "arbitrary")),
        name=name,
    )(h2, comb, x1, mod3, wg, wu, wdn, nfin)


def _rope_tables(seq_len):
    t = np.arange(seq_len)
    row = (t // GRID_W).astype(np.float64)
    col = (t % GRID_W).astype(np.float64)
    half = HEAD_DIM // 2
    inv = np.power(ROPE_THETA, -np.arange(0, half, 2, dtype=np.float64) / half)
    d = np.arange(HEAD_DIM)
    freq = inv[d % (half // 2)]
    pos = np.where(d[None, :] < half, row[:, None], col[:, None])
    ang = pos * freq[None, :]
    sign = np.where((d % half) < half // 2, -1.0, 1.0)
    cos = np.tile(np.cos(ang), (1, ATTN_HEADS)).astype(np.float32)
    sin = np.tile(np.sin(ang) * sign[None, :], (1, ATTN_HEADS)).astype(np.float32)
    return jnp.asarray(cos), jnp.asarray(sin)


def _ones_block_diag():
    i = np.arange(LANE_TILE)
    return jnp.asarray((i[:, None] // HEAD_DIM == i[None, :] // HEAD_DIM).astype(np.float32), dtype=bf16)


def _lane_replicator():
    i = np.arange(HEAD_DIM)
    j = np.arange(ATTN_GROUP * HEAD_DIM)
    return jnp.asarray((i[:, None] == j[None, :] % HEAD_DIM).astype(np.float32), dtype=bf16)


def _lane_replicator_cat():
    i = np.arange(ATTN_KV_W)
    j = np.arange(ATTN_GROUP * HEAD_DIM)
    mats = [(i[:, None] == kv * HEAD_DIM + j[None, :] % HEAD_DIM).astype(np.float32) for kv in range(ATTN_KV_HEADS)]
    return jnp.asarray(np.stack(mats), dtype=bf16)


def _trunk(x3, mod3, mod_row, wts, prefix):
    n_seq, seq_len, _ = x3.shape
    t = n_seq * seq_len
    x2d = x3.reshape(t, D_MODEL)
    is_lat = prefix is not None
    rope_tabs = _rope_tables(seq_len) if is_lat else None

    res = _proj_call(
        x2d, mod3, mod_row, wts["nmix"], wts["w_in_t"], wts["qn"],
        wts["kn"], wts["ones_bd"], rope_tabs, n_seq, seq_len)
    q, kcat, vcat = res[:3]
    k, v = (None, None) if is_lat else (jnp.swapaxes(res[3], 3, 4), jnp.swapaxes(res[4], 3, 4))
    dqkv, ba, bat, dgate, sga, sgd = res[-6:]

    q3 = q.reshape(n_seq, seq_len, ATTN_Q_W)
    kcat3 = kcat.reshape(n_seq, seq_len, ATTN_KV_W)
    vcat3 = vcat.reshape(n_seq, seq_len, ATTN_KV_W)
    if is_lat:
        pk, pv, s0 = prefix
        o_attn = _attn_call(q3, [pk, pv, kcat3, vcat3], wts["rep"], wts["repcat"], 1024, "attn_lat")
    else:
        s0 = None
        o_attn = _attn_call(q3, [kcat3, vcat3], wts["rep"], wts["repcat"], seq_len, "attn_ctx")

    n_chunks = seq_len // CHUNK
    bar4 = bat.reshape(2 * N_GATES, n_seq, n_chunks, CHUNK).transpose(1, 2, 0, 3)
    barp4 = bat[N_GATES:].reshape(N_GATES // 2, 2, n_seq, n_chunks, CHUNK).transpose(2, 3, 0, 1, 4).reshape(
        n_seq, n_chunks, N_GATES // 2, 2 * CHUNK)
    dn_consts = [wts[name] for name in ("convw", "alog_r", "dtb_r", "alog_p", "dtb_p", "dnn",
                                        "ones_bd", "expand")]
    o_dn, state = _dn_call(
        dqkv.reshape(n_seq, seq_len, 3 * DN_W), ba.reshape(n_seq, seq_len, LANE_TILE), bar4, barp4,
        dgate.reshape(n_seq, seq_len, DN_W), s0, dn_consts, not is_lat, "dn_lat" if is_lat else "dn_ctx")

    x1, h2, comb = _merge_call(
        o_attn.reshape(t, ATTN_Q_W), o_dn.reshape(t, DN_W), sga, sgd, x2d, mod3, mod_row,
        wts["wa"], wts["wd"], wts["wo"], wts["nffn"], wts["wr"], wts["br"], seq_len,
        "merge_lat" if is_lat else "merge_ctx")

    y = _moe_call(h2, comb, x1, mod3, mod_row, wts["wg"], wts["wu"], wts["wdn"], wts["nfin"], seq_len,
                  "moe_lat" if is_lat else "moe_ctx")
    return y.reshape(n_seq, seq_len, D_MODEL), k, v, state


def kernel(x_prompt, x_sample, c, cache_attn_k, cache_attn_v, state_delta, c_ctx, w_mod, b_mod, norm_mix, norm_ffn, norm_final, w_in, q_norm, k_norm, conv_w, a_log, dt_bias, dn_norm, w_attn_br, w_dn_br, w_out, w_rg, b_rg, w_re, b_re, w_gate_e, w_up_e, w_down_e):
    layer = 0
    n_lat = x_sample.shape[0]
    wts = {
        "nmix": norm_mix[layer][None, :],
        "w_in_t": jnp.swapaxes(w_in[layer], 0, 1).astype(bf16),
        "qn": jnp.tile(q_norm[layer], ATTN_HEADS)[None, :],
        "kn": jnp.tile(k_norm[layer], ATTN_KV_HEADS)[None, :],
        "ones_bd": _ones_block_diag(),
        "rep": _lane_replicator(),
        "repcat": _lane_replicator_cat(),
        "convw": conv_w[layer],
        "alog_r": a_log[layer].reshape(N_GATES, 1),
        "dtb_r": dt_bias[layer].reshape(N_GATES, 1),
        "alog_p": jnp.repeat(a_log[layer].reshape(N_GATES // 2, 2), CHUNK, axis=1),
        "dtb_p": jnp.repeat(dt_bias[layer].reshape(N_GATES // 2, 2), CHUNK, axis=1),
        "expand": jnp.asarray(np.repeat(np.eye(LANE_TILE, N_GATES, dtype=np.float32), DN_DK, axis=1), dtype=bf16),
        "dnn": jnp.tile(dn_norm[layer], DN_HEADS)[None, :],
        "wa": w_attn_br[layer].astype(bf16),
        "wd": w_dn_br[layer].astype(bf16),
        "wo": w_out[layer].astype(bf16),
        "nffn": norm_ffn[layer][None, :],
        "wr": jnp.concatenate([w_rg[layer], w_re[layer],
                               jnp.zeros((D_MODEL, ROUTER_LANES - N_GROUPS - N_EXPERTS), f32)], axis=1),
        "br": jnp.concatenate([b_rg[layer], b_re[layer],
                               jnp.zeros((ROUTER_LANES - N_GROUPS - N_EXPERTS,), f32)])[None, :],
        "wg": w_gate_e[layer].astype(bf16),
        "wu": w_up_e[layer].astype(bf16),
        "wdn": w_down_e[layer].reshape(N_GROUPS, EXPERTS_PER_GROUP * D_FF_EXPERT, D_MODEL).astype(bf16),
        "nfin": norm_final[None, :],
    }
    cond8 = jnp.concatenate([c_ctx[None, :], c, jnp.zeros((8 - 1 - n_lat, D_MODEL), f32)], axis=0)
    mod3 = _mod_call(cond8, w_mod[layer], b_mod[layer][None, :])[:, None, :]

    y_prompt, new_k, new_v, new_state = _trunk(x_prompt, mod3, lambda s: 0, wts, None)
    y_sample, _, _, _ = _trunk(x_sample, mod3, lambda s: s + 1, wts,
                               (cache_attn_k, cache_attn_v, state_delta))
    return (y_prompt, y_sample, new_k, new_v, new_state)
```

```python
import functools

import numpy as np
import jax
import jax.numpy as jnp
from jax import lax
from jax.experimental import pallas as pl
from jax.experimental.pallas import tpu as pltpu

f32 = jnp.float32
bf16 = jnp.bfloat16

D_MODEL = 1024
HEAD_DIM = 64
ATTN_HEADS = 8
ATTN_KV_HEADS = 2
ATTN_GROUP = ATTN_HEADS // ATTN_KV_HEADS
GRID_W = 64
ROPE_THETA = 10000.0
DN_HEADS = 8
DN_DK = 64
CHUNK = 64
N_GROUPS = 4
EXPERTS_PER_GROUP = 4
N_EXPERTS = 16
D_FF_EXPERT = 256
N_MOD = 6
EPS = 1e-6

ATTN_Q_W = ATTN_HEADS * HEAD_DIM
ATTN_KV_W = ATTN_KV_HEADS * HEAD_DIM
DN_W = DN_HEADS * DN_DK
N_GATES = 2 * DN_HEADS

OFF_Q = 0
OFF_K = ATTN_Q_W
OFF_V = OFF_K + ATTN_KV_W
OFF_DQKV = OFF_V + ATTN_KV_W
OFF_BA = OFF_DQKV + 3 * DN_W
OFF_DGATE = OFF_BA + 2 * N_GATES
OFF_GATTN = OFF_DGATE + DN_W
OFF_GDN = OFF_GATTN + D_MODEL

LANE_TILE = 128
BF16_ROW_TILE = 16
ROUTER_LANES = LANE_TILE
VMEM_LIMIT = 56 * 1024 * 1024
DN_CHUNKS_PER_ITER = 2

_TRANS_B = (((1,), (1,)), ((), ()))


def _cparams(sem):
    return pltpu.CompilerParams(dimension_semantics=sem, vmem_limit_bytes=VMEM_LIMIT)


def _dot(a, b):
    return jnp.dot(a, b, preferred_element_type=f32)


def _silu(x):
    return x * jax.nn.sigmoid(x)


def _softplus(x):
    return jnp.maximum(x, 0.0) + jnp.log1p(jnp.exp(-jnp.abs(x)))


def _split2(x):
    hi = x.astype(bf16)
    lo = (x - hi.astype(f32)).astype(bf16)
    return hi, lo


def _split3(x):
    a = x.astype(bf16)
    r = x - a.astype(f32)
    b = r.astype(bf16)
    c = (r - b.astype(f32)).astype(bf16)
    return a, b, c


def _group_sumsq_many(arrs, ones_pair):
    pieces = []
    for x in arrs:
        hi, lo = _split2(x * x)
        for s in range(x.shape[1] // LANE_TILE):
            pieces.append(hi[:, s * LANE_TILE:(s + 1) * LANE_TILE])
            pieces.append(lo[:, s * LANE_TILE:(s + 1) * LANE_TILE])
    res = _dot(jnp.concatenate(pieces, axis=0), ones_pair)
    out, off = [], 0
    for x in arrs:
        r = x.shape[0]
        slabs = []
        for s in range(x.shape[1] // LANE_TILE):
            slabs.append(res[off:off + r] + res[off + r:off + 2 * r])
            off += 2 * r
        out.append(slabs[0] if len(slabs) == 1 else jnp.concatenate(slabs, axis=1))
    return out


def _group_sumsq(x, ones_pair):
    return _group_sumsq_many([x], ones_pair)[0]


def _const_spec(shape):
    nd = len(shape)
    return pl.BlockSpec(shape, lambda *_: (0,) * nd, pipeline_mode=pl.Buffered(1))


def _mod_kernel(c_ref, w_ref, b_ref, o_ref):
    c = c_ref[...]
    o_ref[...] = _dot(_silu(c).astype(bf16), w_ref[...].astype(bf16)) + b_ref[...]


def _mod_call(cond8, w_mod, b_mod):
    tn = 1536
    n = w_mod.shape[1]
    return pl.pallas_call(
        _mod_kernel,
        grid=(n // tn,),
        in_specs=[_const_spec((8, D_MODEL)),
                  pl.BlockSpec((D_MODEL, tn), lambda j: (0, j)),
                  pl.BlockSpec((1, tn), lambda j: (0, j))],
        out_specs=pl.BlockSpec((8, tn), lambda j: (0, j)),
        out_shape=jax.ShapeDtypeStruct((8, n), f32),
        compiler_params=_cparams(("arbitrary",)),
        name="mod",
    )(cond8, w_mod, b_mod)


def _rope(x, cos, sin):
    w = x.shape[-1]
    half, quarter = HEAD_DIM // 2, HEAD_DIM // 4
    lane = lax.broadcasted_iota(jnp.int32, x.shape, 1)
    first = (lane % half) < quarter
    swapped = jnp.where(first, pltpu.roll(x, w - quarter, 1), pltpu.roll(x, quarter, 1))
    return x * cos + swapped * sin


def _proj_kernel(*refs, rope, seq_per_tile):
    (x_ref, mod_ref, nmix_ref, wt_ref, qn_ref, kn_ref, ones_ref) = refs[:7]
    pos = 7
    if rope:
        cos_ref, sin_ref = refs[7:9]
        pos = 9
    outs = list(refs[pos:])
    q_out, kcat_out, vcat_out = outs[:3]
    if not rope:
        k_out, v_out = outs[3:5]
        outs = outs[2:]
    (dqkv_out, ba_out, bat_out, dgate_out, sga_out, sgd_out) = outs[3:]

    x = x_ref[...]
    mod = mod_ref[0]
    shift1 = mod[:, 0:D_MODEL]
    scale1 = mod[:, D_MODEL:2 * D_MODEL]
    ms = jnp.mean(x * x, axis=-1, keepdims=True)
    h = x * lax.rsqrt(ms + EPS) * nmix_ref[...]
    h = h * (1.0 + scale1) + shift1
    hb = h.astype(bf16)

    def proj(lo, hi):
        return lax.dot_general(hb, wt_ref[lo:hi, :], _TRANS_B, preferred_element_type=f32)

    aq = proj(OFF_Q, OFF_Q + ATTN_Q_W)
    ak = proj(OFF_K, OFF_K + ATTN_KV_W)
    av = proj(OFF_V, OFF_V + ATTN_KV_W)
    ssq, ssk = _group_sumsq_many([aq, ak], ones_ref[...])
    aq = aq * lax.rsqrt(ssq * (1.0 / HEAD_DIM) + EPS) * qn_ref[...]
    ak = ak * lax.rsqrt(ssk * (1.0 / HEAD_DIM) + EPS) * kn_ref[...]
    if not rope:
        tm = x.shape[0]
        seq = tm // seq_per_tile
        akt = jnp.transpose(ak)
        avt = jnp.transpose(av)
        for s in range(seq_per_tile):
            for kv in range(ATTN_KV_HEADS):
                k_out[s, 0, kv] = akt[kv * HEAD_DIM:(kv + 1) * HEAD_DIM, s * seq:(s + 1) * seq]
                v_out[s, 0, kv] = avt[kv * HEAD_DIM:(kv + 1) * HEAD_DIM, s * seq:(s + 1) * seq]
    else:
        aq = _rope(aq, cos_ref[...], sin_ref[...])
        ak = _rope(ak, cos_ref[:, 0:ATTN_KV_W], sin_ref[:, 0:ATTN_KV_W])
    q_out[...] = (aq * (HEAD_DIM ** -0.5)).astype(bf16)
    kcat_out[...] = ak.astype(bf16)
    vcat_out[...] = av.astype(bf16)

    dqkv_out[...] = proj(OFF_DQKV, OFF_BA).astype(bf16)
    ba = proj(OFF_BA, OFF_BA + LANE_TILE)
    ba_out[...] = ba
    bat_out[...] = jnp.transpose(ba)[0:2 * N_GATES, :]
    dgate_out[...] = proj(OFF_DGATE, OFF_GATTN).astype(bf16)
    sga_out[...] = jax.nn.sigmoid(proj(OFF_GATTN, OFF_GDN)).astype(bf16)
    sgd_out[...] = jax.nn.sigmoid(proj(OFF_GDN, OFF_GDN + D_MODEL)).astype(bf16)


def _proj_call(x2d, mod3, mod_row, nmix, w_t, qn, kn, ones_bd, rope_tabs, n_seq, seq_len):
    t = x2d.shape[0]
    tm = 512
    rope = rope_tabs is not None
    tiles_per_seq = max(seq_len // tm, 1)
    seq_per_tile = max(tm // seq_len, 1)
    in_specs = [pl.BlockSpec((tm, D_MODEL), lambda i: (i, 0)),
                pl.BlockSpec((1, 1, N_MOD * D_MODEL), lambda i: (mod_row(i // tiles_per_seq), 0, 0)),
                _const_spec((1, D_MODEL)),
                _const_spec(w_t.shape),
                _const_spec((1, ATTN_Q_W)),
                _const_spec((1, ATTN_KV_W)),
                _const_spec((LANE_TILE, LANE_TILE))]
    args = [x2d, mod3, nmix, w_t, qn, kn, ones_bd]

    def row_spec(w):
        return pl.BlockSpec((tm, w), lambda i: (i, 0))

    out_shape = [jax.ShapeDtypeStruct((t, ATTN_Q_W), bf16),
                 jax.ShapeDtypeStruct((t, ATTN_KV_W), bf16),
                 jax.ShapeDtypeStruct((t, ATTN_KV_W), bf16)]
    out_specs = [row_spec(ATTN_Q_W), row_spec(ATTN_KV_W), row_spec(ATTN_KV_W)]
    if rope:
        in_specs += [pl.BlockSpec((tm, ATTN_Q_W), lambda i: (i % tiles_per_seq, 0))] * 2
        args += list(rope_tabs)
    else:
        kv_shape = jax.ShapeDtypeStruct((n_seq, 1, ATTN_KV_HEADS, HEAD_DIM, seq_len), f32)
        kv_spec = pl.BlockSpec((seq_per_tile, 1, ATTN_KV_HEADS, HEAD_DIM, seq_len), lambda i: (i, 0, 0, 0, 0))
        out_shape += [kv_shape, kv_shape]
        out_specs += [kv_spec, kv_spec]
    out_shape += [jax.ShapeDtypeStruct((t, 3 * DN_W), bf16),
                  jax.ShapeDtypeStruct((t, LANE_TILE), f32),
                  jax.ShapeDtypeStruct((2 * N_GATES, t), f32),
                  jax.ShapeDtypeStruct((t, DN_W), bf16),
                  jax.ShapeDtypeStruct((t, D_MODEL), bf16),
                  jax.ShapeDtypeStruct((t, D_MODEL), bf16)]
    out_specs += [row_spec(3 * DN_W), row_spec(LANE_TILE),
                  pl.BlockSpec((2 * N_GATES, tm), lambda i: (0, i)),
                  row_spec(DN_W), row_spec(D_MODEL), row_spec(D_MODEL)]
    return pl.pallas_call(
        functools.partial(_proj_kernel, rope=rope, seq_per_tile=seq_per_tile),
        grid=(t // tm,),
        in_specs=in_specs,
        out_specs=tuple(out_specs),
        out_shape=tuple(out_shape),
        compiler_params=_cparams(("arbitrary",)),
        name="proj_lat" if rope else "proj_ctx",
    )(*args)


def _attn_kernel(*refs, n_sets, scores_first):
    q_ref = refs[0]
    kv_refs = refs[1:1 + 2 * n_sets]
    rep_ref, repcat_ref = refs[1 + 2 * n_sets:3 + 2 * n_sets]
    o_ref = refs[-1]
    width = ATTN_GROUP * HEAD_DIM
    blk = lax.broadcasted_iota(jnp.int32, (1, width), 1) // HEAD_DIM

    def head_slab(ref, kv):
        if len(ref.shape) == 5:
            return _dot(ref[0, 0, kv].astype(bf16), rep_ref[...])
        return _dot(ref[0], repcat_ref[kv])

    def scores_of(kv, g, q, k4):
        qg = jnp.where(blk == g, q, jnp.zeros((), bf16))
        return [lax.dot_general(qg, k, _TRANS_B, preferred_element_type=f32) for k in k4]

    def head_out(scores, v4):
        m = scores[0].max(axis=-1, keepdims=True)
        for s in scores[1:]:
            m = jnp.maximum(m, s.max(axis=-1, keepdims=True))
        probs = [jnp.exp(s - m) for s in scores]
        denom = probs[0].sum(axis=-1, keepdims=True)
        for p in probs[1:]:
            denom = denom + p.sum(axis=-1, keepdims=True)
        og = _dot(probs[0].astype(bf16), v4[0])
        for p, v in zip(probs[1:], v4[1:]):
            og = og + _dot(p.astype(bf16), v)
        return og * (1.0 / denom)

    qs, k4s, v4s = [], [], []
    for kv in range(ATTN_KV_HEADS):
        qs.append(q_ref[0, :, kv * width:(kv + 1) * width])
        k4s.append([head_slab(kv_refs[2 * s], kv).astype(bf16) for s in range(n_sets)])
        v4s.append([head_slab(kv_refs[2 * s + 1], kv).astype(bf16) for s in range(n_sets)])
    heads = [(kv, g) for kv in range(ATTN_KV_HEADS) for g in range(ATTN_GROUP)]
    if scores_first:
        all_scores = [scores_of(kv, g, qs[kv], k4s[kv]) for kv, g in heads]
    accs = [jnp.zeros((qs[0].shape[0], width), f32) for _ in range(ATTN_KV_HEADS)]
    for i, (kv, g) in enumerate(heads):
        sc = all_scores[i] if scores_first else scores_of(kv, g, qs[kv], k4s[kv])
        accs[kv] = jnp.where(blk == g, head_out(sc, v4s[kv]), accs[kv])
    for kv in range(ATTN_KV_HEADS):
        o_ref[0, :, kv * width:(kv + 1) * width] = accs[kv].astype(o_ref.dtype)


def _attn_call(q3, kv_sets, rep, repcat, tq, name):
    b, l, _ = q3.shape
    in_specs = [pl.BlockSpec((1, tq, ATTN_Q_W), lambda i, t: (i, t, 0))]
    args = [q3]
    for arr in kv_sets:
        nd = arr.ndim
        blk = (1,) + arr.shape[1:]
        in_specs.append(pl.BlockSpec(blk, lambda i, t, nd=nd: (i,) + (0,) * (nd - 1)))
        args.append(arr)
    in_specs += [_const_spec(rep.shape), _const_spec(repcat.shape)]
    args += [rep, repcat]
    return pl.pallas_call(
        functools.partial(_attn_kernel, n_sets=len(kv_sets) // 2, scores_first=len(kv_sets) == 2),
        grid=(b, l // tq),
        in_specs=in_specs,
        out_specs=pl.BlockSpec((1, tq, ATTN_Q_W), lambda i, t: (i, t, 0)),
        out_shape=jax.ShapeDtypeStruct((b, l, ATTN_Q_W), bf16),
        compiler_params=_cparams(("arbitrary", "arbitrary")),
        name=name,
    )(*args)


def _dn_kernel(*refs, seq_len, nseq, has_s0, want_state):
    it = iter(refs)
    dqkv_ref = next(it)
    ba_ref = next(it)
    bar_ref = next(it)
    barp_ref = next(it)
    dgate_ref = next(it)
    s0_ref = next(it) if has_s0 else None
    convw_ref = next(it)
    alog_r = next(it)
    dtb_r = next(it)
    alog_p = next(it)
    dtb_p = next(it)
    dnn_ref = next(it)
    ones_ref = next(it)
    expand_ref = next(it)
    o_ref = next(it)
    st_ref = next(it) if want_state else None
    q_s, k_s, v_s, kt_s, gx_s, bx_s, grp_s, gr_s, of_s, ob_s, s_s, cu_s, qn_s, el_s = it
    cpi = DN_CHUNKS_PER_ITER

    n = seq_len // CHUNK
    c = CHUNK
    pw = 2 * DN_DK
    n_pairs = DN_HEADS // 2
    rp = lax.broadcasted_iota(jnp.int32, (c, pw), 0)
    cp_raw = lax.broadcasted_iota(jnp.int32, (c, pw), 1)
    cp = cp_raw % c
    triu_pad = jnp.where((rp <= cp_raw) & (cp_raw < c), 1.0, 0.0).astype(bf16)
    tril_pad = jnp.where((rp >= cp_raw) & (cp_raw < c), 1.0, 0.0).astype(bf16)
    lane_t = lax.broadcasted_iota(jnp.int32, (1, LANE_TILE), 1)
    lower_p = rp >= cp
    upper_p = rp <= cp
    eye_p = jnp.where(rp == cp, 1.0, 0.0)
    same8 = (rp // 8) == (cp // 8)
    rb2 = lax.broadcasted_iota(jnp.int32, (pw, pw), 0)
    cb2 = lax.broadcasted_iota(jnp.int32, (pw, pw), 1)
    bd_mask = (rb2 // c) == (cb2 // c)
    bd_triu = jnp.where(bd_mask & (rb2 % c <= cb2 % c), 1.0, 0.0).astype(bf16)
    bd_tril = jnp.where(bd_mask & (rb2 % c >= cb2 % c), 1.0, 0.0).astype(bf16)
    sub16 = lax.broadcasted_iota(jnp.int32, (N_GATES, 1), 0)
    sub8 = lax.broadcasted_iota(jnp.int32, (2 * n_pairs, 1), 0)
    sub_pw = lax.broadcasted_iota(jnp.int32, (pw, 1), 0)
    row_c = lax.broadcasted_iota(jnp.int32, (c, 1), 0)

    zero_b = jnp.zeros((), bf16)
    rq = rb2 % c
    cq = cb2 % c
    bd_same8 = bd_mask & ((rq // 8) == (cq // 8))
    bd_off16 = bd_mask & ((rq // 16) == (cq // 16)) & ((rq // 8) != (cq // 8))
    bd_off32 = bd_mask & ((rq // 32) == (cq // 32)) & ((rq // 16) != (cq // 16))
    bd_off64 = bd_mask & ((rq // 32) != (cq // 32))

    def bdiag(x):
        xb = x.astype(bf16)
        return jnp.where(bd_mask, jnp.concatenate([xb, xb], axis=0), zero_b)

    def conv_silu(sq, ci):
        r0 = pl.multiple_of(ci * c, c)
        xb = dqkv_ref[sq, pl.ds(r0, c), :].astype(f32)
        bt = BF16_ROW_TILE
        p0 = pl.multiple_of(jnp.maximum(r0 - bt, 0), bt)
        n0 = pl.multiple_of(jnp.minimum(r0 + c, seq_len - bt), bt)
        prev = dqkv_ref[sq, pl.ds(p0, bt), :].astype(f32)[bt - 1:bt, :]
        nxt = dqkv_ref[sq, pl.ds(n0, bt), :].astype(f32)[0:1, :]
        prev = jnp.where(ci > 0, prev, 0.0)
        nxt = jnp.where(ci < n - 1, nxt, 0.0)
        xm = jnp.where(row_c == 0, prev, pltpu.roll(xb, 1, 0))
        xp = jnp.where(row_c == c - 1, nxt, pltpu.roll(xb, c - 1, 0))
        w = convw_ref[...]
        return _silu(xm * w[0:1, :] + xb * w[1:2, :] + xp * w[2:3, :])

    def beta_col(sq, ci):
        ba = ba_ref[sq, pl.ds(pl.multiple_of(ci * c, c), c), :]
        return jnp.where(lane_t < N_GATES, jax.nn.sigmoid(ba), 0.0)

    def stacked_sums(parts_per_chunk, tri_a, tri_b):
        rows = parts_per_chunk[0][0].shape[0]
        stack = jnp.concatenate([p for parts in parts_per_chunk for p in parts], axis=0)
        res_a = _dot(stack, tri_a)
        res_b = _dot(stack, tri_b)
        out = []
        for s, parts in enumerate(parts_per_chunk):
            base = s * len(parts) * rows
            blocks = [slice(base + k * rows, base + (k + 1) * rows) for k in range(len(parts))]
            out.append((sum(res_a[b] for b in blocks), sum(res_b[b] for b in blocks)))
        return out

    def pre_phase_a(sq, it):
        cis = [it * cpi + s for s in range(cpi)]
        act = [conv_silu(sq, ci) for ci in cis]
        betas = [beta_col(sq, ci) for ci in cis]
        g_r = [_split3(-jnp.exp(alog_r[...]) * _softplus(bar_ref[sq, ci][N_GATES:2 * N_GATES, :] + dtb_r[...]))
               for ci in cis]
        g_p = [_split3(-jnp.exp(alog_p[...]) * _softplus(barp_ref[sq, ci] + dtb_p[...])) for ci in cis]
        sumsq = _group_sumsq_many([a[:, 0:2 * DN_W] for a in act], ones_ref[...])
        gr = [jnp.where(sub16 < DN_HEADS, f, b) for f, b in stacked_sums(g_r, triu_pad, tril_pad)]
        grp = [jnp.where(sub8 < n_pairs, f, b) for f, b in stacked_sums(g_p, bd_triu, bd_tril)]
        return act, sumsq, betas, gr, grp

    def pre_phase_b(pa):
        _, _, betas, gr, _ = pa
        zpad = jnp.zeros((LANE_TILE - N_GATES, LANE_TILE), f32)
        gcol = [jnp.transpose(jnp.concatenate([g16, zpad], axis=0))[0:c, :] for g16 in gr]
        parts = [p for g in gcol for p in _split3(g)] + [p for b in betas for p in _split2(b)]
        res = _dot(jnp.concatenate(parts, axis=0), expand_ref[...])
        gx = [sum(res[(3 * s + k) * c:(3 * s + k + 1) * c] for k in range(3)) for s in range(cpi)]
        off = 3 * cpi * c
        bx = [sum(res[off + (2 * s + k) * c:off + (2 * s + k + 1) * c] for k in range(2)) for s in range(cpi)]
        return gx, bx

    def pre_phase_c(pa, pb, slot):
        act, sumsq, _, gr, grp = pa
        gx, bx = pb
        for s in range(cpi):
            rs = slice(s * c, (s + 1) * c)
            a = act[s]
            scale = lax.rsqrt(sumsq[s] + EPS)
            q_s[slot, rs, :] = a[:, 0:DN_W] * scale[:, 0:DN_W] * (DN_DK ** -0.5)
            kn = a[:, DN_W:2 * DN_W] * scale[:, DN_W:2 * DN_W]
            k_s[slot, rs, :] = kn
            v_s[slot, rs, :] = a[:, 2 * DN_W:3 * DN_W]
            kt_s[slot, s] = jnp.transpose(jnp.concatenate([kn, kn], axis=0)).astype(bf16)
            gx_s[slot, rs, :] = gx[s]
            bx_s[slot, rs, :] = bx[s]
            gr_s[slot, s] = gr[s][:, 0:c]
            grp_s[slot, s] = grp[s]

    zero_blk = jnp.zeros((DN_DK, DN_DK), f32)
    for sq in range(nseq):
        for d in range(2):
            for p in range(n_pairs):
                if has_s0:
                    top = jnp.concatenate([s0_ref[sq, 0, d, 2 * p], zero_blk], axis=1)
                    bot = jnp.concatenate([zero_blk, s0_ref[sq, 0, d, 2 * p + 1]], axis=1)
                    s_s[sq, d * n_pairs + p] = jnp.concatenate([top, bot], axis=0)
                else:
                    s_s[sq, d * n_pairs + p] = jnp.zeros((pw, pw), f32)

    def each(fn, *lists):
        return [fn(*vals) for vals in zip(*lists)]

    def fused_half(sq, it, slot, nxt):
        pa = pre_phase_a(nxt[0], nxt[1]) if nxt is not None else None
        qs, ks, vs, bx, eg, decay, strict, ktd, kq, where_to = [], [], [], [], [], [], [], [], [], []
        for sub in range(cpi):
            ci = it * cpi + sub
            r0 = pl.multiple_of(ci * c, c)
            rs = slice(sub * c, (sub + 1) * c)
            grp = grp_s[slot, sub]
            gr16 = gr_s[slot, sub]
            for p in range(n_pairs):
                sl = slice(p * pw, (p + 1) * pw)
                q_p = q_s[slot, rs, sl]
                k_p = k_s[slot, rs, sl]
                v_p = v_s[slot, rs, sl]
                kt = kt_s[slot, sub, p * pw:(p + 1) * pw, :]
                kq_p = _dot(jnp.concatenate([k_p, q_p], axis=0).astype(bf16),
                            jnp.where(bd_mask, kt, jnp.zeros((), bf16)))
                for d in range(2):
                    pp = d * n_pairs + p
                    j0 = d * DN_HEADS + 2 * p
                    xl = slice(pp * pw, (pp + 1) * pw)
                    g_end = gr16[:, c - 1:c] if d == 0 else gr16[:, 0:1]
                    e_rest_t = jnp.exp(g_end - gr16)
                    incl = lower_p if d == 0 else upper_p
                    gx = gx_s[slot, rs, xl]
                    g_last = gx[c - 1:c, :] if d == 0 else gx[0:1, :]
                    el_s[sq, ci, pp:pp + 1, :] = jnp.exp(g_last)
                    qs.append(q_p)
                    ks.append(k_p)
                    vs.append(v_p)
                    kq.append(kq_p)
                    bx.append(bx_s[slot, rs, xl])
                    eg.append(jnp.exp(gx))
                    diff = gx - grp[pp:pp + 1, :]
                    decay.append(jnp.where(incl, jnp.exp(jnp.where(incl, diff, 0.0)), 0.0))
                    strict.append((rp > cp) if d == 0 else (rp < cp))
                    fac = jnp.where(sub_pw < c, e_rest_t[j0:j0 + 1, :], e_rest_t[j0 + 1:j0 + 2, :])
                    ktd.append((kt[:, 0:c].astype(f32) * fac).astype(bf16))
                    where_to.append((ci, pp, r0, sl, d))

        pb = pre_phase_b(pa) if pa is not None else None
        lmat = each(lambda m, x, b, dc: jnp.where(m, x[0:c] * b * dc, 0.0), strict, kq, bx, decay)
        amat = each(lambda x, dc: (x[c:2 * c] * dc).astype(bf16), kq, decay)
        l2 = each(lambda l: jnp.concatenate([l.astype(bf16)] * 2, axis=0), lmat)
        d1 = each(lambda l: jnp.where(same8, l, 0.0), lmat)
        d2 = each(lambda a, ll: _dot(a.astype(bf16), jnp.where(bd_same8, ll, zero_b)).astype(bf16), d1, l2)
        tmat = each(lambda a: eye_p - a, d1)
        tmat = each(lambda t, a: t + _dot(a, bdiag(t)), tmat, d2)
        d4 = each(lambda a: _dot(a, bdiag(a)).astype(bf16), d2)
        tmat = each(lambda t, a: t + _dot(a, bdiag(t)), tmat, d4)
        for bd_off in (bd_off16, bd_off32, bd_off64):
            tb = each(lambda t: t.astype(bf16), tmat)
            tl = each(lambda t, ll: _dot(t, jnp.where(bd_off, ll, zero_b)).astype(bf16), tb, l2)
            tmat = each(lambda t, a, b: t - _dot(a, bdiag(b)), tmat, tl, tb)
        uw = each(lambda t, v, k, b, e: _dot(t.astype(bf16), jnp.concatenate([bdiag(v * b), bdiag(k * (b * e))], axis=1)),
                  tmat, vs, ks, bx, eg)
        ku_kw = each(lambda kd, y: _dot(kd, y.astype(bf16)), ktd, uw)
        au_aw = each(lambda a, y: _dot(a, jnp.concatenate([bdiag(y[:, 0:pw]), bdiag(y[:, pw:2 * pw])], axis=1)),
                     amat, uw)
        if pa is not None:
            pre_phase_c(pa, pb, nxt[2])
        for (ci, pp, r0, sl, d), q_p, e, kk, aa in zip(where_to, qs, eg, ku_kw, au_aw):
            cu_s[sq, ci, pp] = jnp.where(bd_mask, kk[:, 0:pw], 0.0)
            qn_s[sq, ci, pp, 0:c] = (q_p * e - aa[:, pw:2 * pw]).astype(bf16)
            qn_s[sq, ci, pp, c:c + pw] = jnp.where(bd_mask, -kk[:, pw:2 * pw], 0.0).astype(bf16)
            dst = of_s if d == 0 else ob_s
            dst[sq, pl.ds(r0, c), sl] = aa[:, 0:pw]

    n_it = n // cpi
    first = pre_phase_a(0, 0)
    pre_phase_c(first, pre_phase_b(first), 0)
    if nseq * n_it <= 4:
        items = [(sq, it_) for sq in range(nseq) for it_ in range(n_it)]
        for k, (sq, it_) in enumerate(items):
            nxt = items[k + 1] + ((k + 1) % 2,) if k + 1 < len(items) else None
            fused_half(sq, it_, k % 2, nxt)
    else:
        assert nseq == 1 and n_it % 2 == 0

        def pipe_body(j, carry):
            fused_half(0, 2 * j, 0, (0, 2 * j + 1, 1))
            fused_half(0, 2 * j + 1, 1, (0, 2 * j + 2, 0))
            return carry

        lax.fori_loop(0, n_it // 2 - 1, pipe_body, 0)
        fused_half(0, n_it - 2, 0, (0, n_it - 1, 1))
        fused_half(0, n_it - 1, 1, None)

    def scan_body(i, carry):
        cis = [i, n - 1 - i]
        pairs = [(sq, d, p) for sq in range(nseq) for d in range(2) for p in range(n_pairs)]
        s_old = [s_s[sq, d * n_pairs + p] for sq, d, p in pairs]
        s_b = [s.astype(bf16) for s in s_old]
        both = [_dot(qn_s[sq, cis[d], d * n_pairs + p], sb) for (sq, d, p), sb in zip(pairs, s_b)]
        for (sq, d, p), s, r in zip(pairs, s_old, both):
            pp = d * n_pairs + p
            r0 = pl.multiple_of(cis[d] * c, c)
            dst = of_s if d == 0 else ob_s
            dst[sq, pl.ds(r0, c), p * pw:(p + 1) * pw] += r[0:c]
            s_s[sq, pp] = s * el_s[sq, cis[d], pp:pp + 1, :] + r[c:c + pw] + cu_s[sq, cis[d], pp]
        return carry

    lax.fori_loop(0, n, scan_body, 0)

    rb = 128
    for sq in range(nseq):
        def post_body(bi, carry, sq=sq):
            r0 = pl.multiple_of(bi * rb, rb)
            o = of_s[sq, pl.ds(r0, rb), :] + ob_s[sq, pl.ds(r0, rb), :]
            o = o * lax.rsqrt(_group_sumsq(o, ones_ref[...]) * (1.0 / DN_DK) + EPS) * dnn_ref[...]
            gate = dgate_ref[sq, pl.ds(r0, rb), :].astype(f32)
            o_ref[sq, pl.ds(r0, rb), :] = (o * _silu(gate)).astype(o_ref.dtype)
            return carry

        lax.fori_loop(0, seq_len // rb, post_body, 0)

    if want_state:
        for sq in range(nseq):
            for d in range(2):
                for p in range(n_pairs):
                    s = s_s[sq, d * n_pairs + p]
                    st_ref[sq, 0, d, 2 * p] = s[0:DN_DK, 0:DN_DK]
                    st_ref[sq, 0, d, 2 * p + 1] = s[DN_DK:pw, DN_DK:pw]


def _dn_call(dqkv3, ba3, bar4, barp4, dgate3, s0, consts, want_state, name):
    b, l, _ = dqkv3.shape
    n = l // CHUNK
    n_pairs = DN_HEADS // 2
    pw = 2 * DN_DK
    cpi = DN_CHUNKS_PER_ITER
    nseq = 2 if (n // cpi <= 2 and b % 2 == 0) else 1
    state_blk = (nseq, 1, 2, DN_HEADS, DN_DK, DN_DK)
    state_spec = pl.BlockSpec(state_blk, lambda i: (i, 0, 0, 0, 0, 0))
    in_specs = [pl.BlockSpec((nseq, l, 3 * DN_W), lambda i: (i, 0, 0)),
                pl.BlockSpec((nseq, l, LANE_TILE), lambda i: (i, 0, 0)),
                pl.BlockSpec((nseq, n, 2 * N_GATES, CHUNK), lambda i: (i, 0, 0, 0)),
                pl.BlockSpec((nseq, n, 2 * n_pairs, pw), lambda i: (i, 0, 0, 0)),
                pl.BlockSpec((nseq, l, DN_W), lambda i: (i, 0, 0))]
    args = [dqkv3, ba3, bar4, barp4, dgate3]
    if s0 is not None:
        in_specs.append(state_spec)
        args.append(s0)
    in_specs += [_const_spec(a.shape) for a in consts]
    args += consts
    out_shape = [jax.ShapeDtypeStruct((b, l, DN_W), bf16)]
    out_specs = [pl.BlockSpec((nseq, l, DN_W), lambda i: (i, 0, 0))]
    if want_state:
        out_shape.append(jax.ShapeDtypeStruct((b,) + state_blk[1:], f32))
        out_specs.append(state_spec)
    rows = cpi * CHUNK
    scratch = [pltpu.VMEM((2, rows, DN_W), f32), pltpu.VMEM((2, rows, DN_W), f32),
               pltpu.VMEM((2, rows, DN_W), f32),
               pltpu.VMEM((2, cpi, DN_W, pw), bf16),
               pltpu.VMEM((2, rows, N_GATES * DN_DK), f32), pltpu.VMEM((2, rows, N_GATES * DN_DK), f32),
               pltpu.VMEM((2, cpi, 2 * n_pairs, pw), f32), pltpu.VMEM((2, cpi, N_GATES, CHUNK), f32),
               pltpu.VMEM((nseq, l, DN_W), f32), pltpu.VMEM((nseq, l, DN_W), f32),
               pltpu.VMEM((nseq, 2 * n_pairs, pw, pw), f32),
               pltpu.VMEM((nseq, n, 2 * n_pairs, pw, pw), f32),
               pltpu.VMEM((nseq, n, 2 * n_pairs, CHUNK + pw, pw), bf16),
               pltpu.VMEM((nseq, n, 2 * n_pairs, pw), f32)]
    res = pl.pallas_call(
        functools.partial(_dn_kernel, seq_len=l, nseq=nseq, has_s0=s0 is not None, want_state=want_state),
        grid=(b // nseq,),
        in_specs=in_specs,
        out_specs=tuple(out_specs),
        out_shape=tuple(out_shape),
        scratch_shapes=scratch,
        compiler_params=_cparams(("arbitrary",)),
        name=name,
    )(*args)
    return res if want_state else (res[0], None)


def _merge_kernel(oa_ref, od_ref, sga_ref, sgd_ref, x_ref, mod_ref, wa_ref, wd_ref, wo_ref, nffn_ref,
                  wr_ref, br_ref, x1_out, h2_out, comb_out):
    mod = mod_ref[0]
    gate1 = mod[:, 2 * D_MODEL:3 * D_MODEL]
    shift2 = mod[:, 3 * D_MODEL:4 * D_MODEL]
    scale2 = mod[:, 4 * D_MODEL:5 * D_MODEL]
    merged = (sga_ref[...].astype(f32) * _dot(oa_ref[...], wa_ref[...])
              + sgd_ref[...].astype(f32) * _dot(od_ref[...], wd_ref[...]))
    m = _dot(merged.astype(bf16), wo_ref[...])
    x1 = x_ref[...] + gate1 * m
    x1_out[...] = x1
    ms = jnp.mean(x1 * x1, axis=-1, keepdims=True)
    h2 = x1 * lax.rsqrt(ms + EPS) * nffn_ref[...]
    h2 = h2 * (1.0 + scale2) + shift2
    h2_out[...] = h2.astype(bf16)

    h_hi, h_lo = _split2(h2)
    w_hi, w_lo = _split2(wr_ref[...])
    logits = _dot(h_hi, w_hi) + _dot(h_hi, w_lo) + _dot(h_lo, w_hi) + br_ref[...]
    n_live = 24
    lt = jnp.transpose(logits)[0:n_live, :]
    row = lax.broadcasted_iota(jnp.int32, lt.shape, 0)
    big = ROUTER_LANES
    neg = -jnp.inf

    def first_row(hit):
        return jnp.min(jnp.where(hit, row, big), axis=0, keepdims=True)

    is_g = row < N_GROUPS
    gl = jnp.where(is_g, lt, neg)
    gexp = jnp.where(is_g, jnp.exp(gl - gl.max(axis=0, keepdims=True)), 0.0)
    gp = gexp / gexp.sum(axis=0, keepdims=True)
    g_top = gp.max(axis=0, keepdims=True)
    g_idx = first_row(is_g & (gp == g_top))
    lo_row = N_GROUPS + g_idx * EXPERTS_PER_GROUP
    is_e = (row >= lo_row) & (row < lo_row + EXPERTS_PER_GROUP)
    el = jnp.where(is_e, lt, neg)
    eexp = jnp.where(is_e, jnp.exp(el - el.max(axis=0, keepdims=True)), 0.0)
    ep = jnp.where(is_e, eexp / eexp.sum(axis=0, keepdims=True), -1.0)
    p1 = ep.max(axis=0, keepdims=True)
    i1 = first_row(ep == p1)
    ep2 = jnp.where(row == i1, -1.0, ep)
    p2 = ep2.max(axis=0, keepdims=True)
    i2 = first_row(ep2 == p2)
    tot = p1 + p2
    comb_t = (jnp.where(row == i1, p1 / tot * g_top, 0.0) + jnp.where(row == i2, p2 / tot * g_top, 0.0))
    pad = jnp.zeros((ROUTER_LANES - n_live, lt.shape[1]), f32)
    comb_out[...] = jnp.transpose(jnp.concatenate([comb_t, pad], axis=0))


def _merge_call(oa, od, sga, sgd, x2d, mod3, mod_row, wa, wd, wo, nffn, wr, br, seq_len, name):
    t = x2d.shape[0]
    tm = 1024
    tiles_per_seq = max(seq_len // tm, 1)

    def row_spec(w):
        return pl.BlockSpec((tm, w), lambda i: (i, 0))

    return pl.pallas_call(
        _merge_kernel,
        grid=(t // tm,),
        in_specs=[row_spec(ATTN_Q_W), row_spec(DN_W), row_spec(D_MODEL), row_spec(D_MODEL), row_spec(D_MODEL),
                  pl.BlockSpec((1, 1, N_MOD * D_MODEL), lambda i: (mod_row(i // tiles_per_seq), 0, 0)),
                  _const_spec(wa.shape), _const_spec(wd.shape), _const_spec(wo.shape),
                  _const_spec((1, D_MODEL)), _const_spec(wr.shape), _const_spec(br.shape)],
        out_specs=(row_spec(D_MODEL), row_spec(D_MODEL), row_spec(ROUTER_LANES)),
        out_shape=(jax.ShapeDtypeStruct((t, D_MODEL), f32),
                   jax.ShapeDtypeStruct((t, D_MODEL), bf16),
                   jax.ShapeDtypeStruct((t, ROUTER_LANES), f32)),
        compiler_params=_cparams(("arbitrary",)),
        name=name,
    )(oa, od, sga, sgd, x2d, mod3, wa, wd, wo, nffn, wr, br)


def _moe_kernel(h2_ref, comb_ref, x1_ref, mod_ref, wg_ref, wu_ref, wdn_ref, nfin_ref, y_ref, acc_ref):
    g = pl.program_id(1)

    @pl.when(g == 0)
    def _():
        acc_ref[...] = jnp.zeros_like(acc_ref)

    h2 = h2_ref[...]
    comb = comb_ref[...]
    lane = lax.broadcasted_iota(jnp.int32, comb.shape, 1)
    first = N_GROUPS + g * EXPERTS_PER_GROUP
    parts = []
    for e in range(EXPERTS_PER_GROUP):
        hid = _silu(_dot(h2, wg_ref[e].astype(bf16))) * _dot(h2, wu_ref[e].astype(bf16))
        ce = jnp.sum(jnp.where(lane == first + e, comb, 0.0), axis=-1, keepdims=True)
        parts.append((hid * ce).astype(bf16))
    acc_ref[...] += _dot(jnp.concatenate(parts, axis=1), wdn_ref[0].astype(bf16))

    @pl.when(g == N_GROUPS - 1)
    def _():
        gate2 = mod_ref[0][:, 5 * D_MODEL:6 * D_MODEL]
        x2 = x1_ref[...] + gate2 * acc_ref[...]
        ms = jnp.mean(x2 * x2, axis=-1, keepdims=True)
        y_ref[...] = x2 * lax.rsqrt(ms + EPS) * nfin_ref[...]


def _moe_call(h2, comb, x1, mod3, mod_row, wg, wu, wdn, nfin, seq_len, name):
    t = h2.shape[0]
    tm = 1024
    tiles_per_seq = max(seq_len // tm, 1)
    return pl.pallas_call(
        _moe_kernel,
        grid=(t // tm, N_GROUPS),
        in_specs=[pl.BlockSpec((tm, D_MODEL), lambda i, e: (i, 0)),
                  pl.BlockSpec((tm, ROUTER_LANES), lambda i, e: (i, 0)),
                  pl.BlockSpec((tm, D_MODEL), lambda i, e: (i, 0)),
                  pl.BlockSpec((1, 1, N_MOD * D_MODEL), lambda i, e: (mod_row(i // tiles_per_seq), 0, 0)),
                  pl.BlockSpec((EXPERTS_PER_GROUP, D_MODEL, D_FF_EXPERT), lambda i, e: (e, 0, 0)),
                  pl.BlockSpec((EXPERTS_PER_GROUP, D_MODEL, D_FF_EXPERT), lambda i, e: (e, 0, 0)),
                  pl.BlockSpec((1, EXPERTS_PER_GROUP * D_FF_EXPERT, D_MODEL), lambda i, e: (e, 0, 0)),
                  pl.BlockSpec((1, D_MODEL), lambda i, e: (0, 0))],
        out_specs=pl.BlockSpec((tm, D_MODEL), lambda i, e: (i, 0)),
        out_shape=jax.ShapeDtypeStruct((t, D_MODEL), f32),
        scratch_shapes=[pltpu.VMEM((tm, D_MODEL), f32)],
        compiler_params=_cparams(("arbitrary", "arbitrary")),
        name=name,
    )(h2, comb, x1, mod3, wg, wu, wdn, nfin)


def _rope_tables(seq_len):
    t = np.arange(seq_len)
    row = (t // GRID_W).astype(np.float64)
    col = (t % GRID_W).astype(np.float64)
    half = HEAD_DIM // 2
    inv = np.power(ROPE_THETA, -np.arange(0, half, 2, dtype=np.float64) / half)
    d = np.arange(HEAD_DIM)
    freq = inv[d % (half // 2)]
    pos = np.where(d[None, :] < half, row[:, None], col[:, None])
    ang = pos * freq[None, :]
    sign = np.where((d % half) < half // 2, -1.0, 1.0)
    cos = np.tile(np.cos(ang), (1, ATTN_HEADS)).astype(np.float32)
    sin = np.tile(np.sin(ang) * sign[None, :], (1, ATTN_HEADS)).astype(np.float32)
    return jnp.asarray(cos), jnp.asarray(sin)


def _ones_block_diag():
    i = np.arange(LANE_TILE)
    return jnp.asarray((i[:, None] // HEAD_DIM == i[None, :] // HEAD_DIM).astype(np.float32), dtype=bf16)


def _lane_replicator():
    i = np.arange(HEAD_DIM)
    j = np.arange(ATTN_GROUP * HEAD_DIM)
    return jnp.asarray((i[:, None] == j[None, :] % HEAD_DIM).astype(np.float32), dtype=bf16)


def _lane_replicator_cat():
    i = np.arange(ATTN_KV_W)
    j = np.arange(ATTN_GROUP * HEAD_DIM)
    mats = [(i[:, None] == kv * HEAD_DIM + j[None, :] % HEAD_DIM).astype(np.float32) for kv in range(ATTN_KV_HEADS)]
    return jnp.asarray(np.stack(mats), dtype=bf16)


def _trunk(x3, mod3, mod_row, wts, prefix):
    n_seq, seq_len, _ = x3.shape
    t = n_seq * seq_len
    x2d = x3.reshape(t, D_MODEL)
    is_lat = prefix is not None
    rope_tabs = _rope_tables(seq_len) if is_lat else None

    res = _proj_call(
        x2d, mod3, mod_row, wts["nmix"], wts["w_in_t"], wts["qn"],
        wts["kn"], wts["ones_bd"], rope_tabs, n_seq, seq_len)
    q, kcat, vcat = res[:3]
    k, v = (None, None) if is_lat else (jnp.swapaxes(res[3], 3, 4), jnp.swapaxes(res[4], 3, 4))
    dqkv, ba, bat, dgate, sga, sgd = res[-6:]

    q3 = q.reshape(n_seq, seq_len, ATTN_Q_W)
    kcat3 = kcat.reshape(n_seq, seq_len, ATTN_KV_W)
    vcat3 = vcat.reshape(n_seq, seq_len, ATTN_KV_W)
    if is_lat:
        pk, pv, s0 = prefix
        o_attn = _attn_call(q3, [pk, pv, kcat3, vcat3], wts["rep"], wts["repcat"], 1024, "attn_lat")
    else:
        s0 = None
        o_attn = _attn_call(q3, [kcat3, vcat3], wts["rep"], wts["repcat"], seq_len, "attn_ctx")

    n_chunks = seq_len // CHUNK
    bar4 = bat.reshape(2 * N_GATES, n_seq, n_chunks, CHUNK).transpose(1, 2, 0, 3)
    barp4 = bat[N_GATES:].reshape(N_GATES // 2, 2, n_seq, n_chunks, CHUNK).transpose(2, 3, 0, 1, 4).reshape(
        n_seq, n_chunks, N_GATES // 2, 2 * CHUNK)
    dn_consts = [wts[name] for name in ("convw", "alog_r", "dtb_r", "alog_p", "dtb_p", "dnn",
                                        "ones_bd", "expand")]
    o_dn, state = _dn_call(
        dqkv.reshape(n_seq, seq_len, 3 * DN_W), ba.reshape(n_seq, seq_len, LANE_TILE), bar4, barp4,
        dgate.reshape(n_seq, seq_len, DN_W), s0, dn_consts, not is_lat, "dn_lat" if is_lat else "dn_ctx")

    x1, h2, comb = _merge_call(
        o_attn.reshape(t, ATTN_Q_W), o_dn.reshape(t, DN_W), sga, sgd, x2d, mod3, mod_row,
        wts["wa"], wts["wd"], wts["wo"], wts["nffn"], wts["wr"], wts["br"], seq_len,
        "merge_lat" if is_lat else "merge_ctx")

    y = _moe_call(h2, comb, x1, mod3, mod_row, wts["wg"], wts["wu"], wts["wdn"], wts["nfin"], seq_len,
                  "moe_lat" if is_lat else "moe_ctx")
    return y.reshape(n_seq, seq_len, D_MODEL), k, v, state


def kernel(x_prompt, x_sample, c, cache_attn_k, cache_attn_v, state_delta, c_ctx, w_mod, b_mod, norm_mix, norm_ffn, norm_final, w_in, q_norm, k_norm, conv_w, a_log, dt_bias, dn_norm, w_attn_br, w_dn_br, w_out, w_rg, b_rg, w_re, b_re, w_gate_e, w_up_e, w_down_e):
    layer = 0
    n_lat = x_sample.shape[0]
    wts = {
        "nmix": norm_mix[layer][None, :],
        "w_in_t": jnp.swapaxes(w_in[layer], 0, 1).astype(bf16),
        "qn": jnp.tile(q_norm[layer], ATTN_HEADS)[None, :],
        "kn": jnp.tile(k_norm[layer], ATTN_KV_HEADS)[None, :],
        "ones_bd": _ones_block_diag(),
        "rep": _lane_replicator(),
        "repcat": _lane_replicator_cat(),
        "convw": conv_w[layer],
        "alog_r": a_log[layer].reshape(N_GATES, 1),
        "dtb_r": dt_bias[layer].reshape(N_GATES, 1),
        "alog_p": jnp.repeat(a_log[layer].reshape(N_GATES // 2, 2), CHUNK, axis=1),
        "dtb_p": jnp.repeat(dt_bias[layer].reshape(N_GATES // 2, 2), CHUNK, axis=1),
        "expand": jnp.asarray(np.repeat(np.eye(LANE_TILE, N_GATES, dtype=np.float32), DN_DK, axis=1), dtype=bf16),
        "dnn": jnp.tile(dn_norm[layer], DN_HEADS)[None, :],
        "wa": w_attn_br[layer].astype(bf16),
        "wd": w_dn_br[layer].astype(bf16),
        "wo": w_out[layer].astype(bf16),
        "nffn": norm_ffn[layer][None, :],
        "wr": jnp.concatenate([w_rg[layer], w_re[layer],
                               jnp.zeros((D_MODEL, ROUTER_LANES - N_GROUPS - N_EXPERTS), f32)], axis=1),
        "br": jnp.concatenate([b_rg[layer], b_re[layer],
                               jnp.zeros((ROUTER_LANES - N_GROUPS - N_EXPERTS,), f32)])[None, :],
        "wg": w_gate_e[layer],
        "wu": w_up_e[layer],
        "wdn": w_down_e[layer].reshape(N_GROUPS, EXPERTS_PER_GROUP * D_FF_EXPERT, D_MODEL),
        "nfin": norm_final[None, :],
    }
    cond8 = jnp.concatenate([c_ctx[None, :], c, jnp.zeros((8 - 1 - n_lat, D_MODEL), f32)], axis=0)
    mod3 = _mod_call(cond8, w_mod[layer], b_mod[layer][None, :])[:, None, :]

    y_prompt, new_k, new_v, new_state = _trunk(x_prompt, mod3, lambda s: 0, wts, None)
    y_sample, _, _, _ = _trunk(x_sample, mod3, lambda s: s + 1, wts,
                               (cache_attn_k, cache_attn_v, state_delta))
    return (y_prompt, y_sample, new_k, new_v, new_state)
```
